```python
import jax
import jax.numpy as jnp
from jax import lax
import numpy as np

D_MODEL = 1024
BATCH = 2
SEQ = 8192
DEPTH = 1
DEC_BATCH = 128
DEC_SEQ = 1
PAST_LEN = 2048
PAGE_SIZE = 128

N_HEADS = 8
HEAD_DIM = 64
D_ATTN = N_HEADS * HEAD_DIM
MOBA_BLOCK = 256
MOBA_TOPK = 3
Q_BLOCK = 128
D_CONV = 512
CONV_WIDTH = 31
N_EXPERTS = 32
TOP_K = 4
D_FF = D_MODEL
SWIGLU_ALPHA = 1.702
SWIGLU_LIMIT = 7.0
MOE_BLOCK = 128
EPS = 1e-6
D_IN = 3 * D_ATTN + 2 * D_CONV + 2 * D_MODEL
SPLITS = [D_ATTN, 2 * D_ATTN, 3 * D_ATTN, 3 * D_ATTN + 2 * D_CONV, 3 * D_ATTN + 2 * D_CONV + D_MODEL]

kernel_name = 'moba_conformer_moe_hybrid_step'


def alibi_slopes():
    return jnp.exp2(-8.0 * jnp.arange(1, N_HEADS + 1, dtype=jnp.float32) / N_HEADS)


def rms_norm(x, g):
    xf = x.astype(jnp.float32)
    y = xf * lax.rsqrt(jnp.mean(xf * xf, axis=-1, keepdims=True) + EPS)
    return y.astype(x.dtype) * g


def layer_norm(x, g, b):
    xf = x.astype(jnp.float32)
    mu = jnp.mean(xf, axis=-1, keepdims=True)
    var = jnp.mean(jnp.square(xf - mu), axis=-1, keepdims=True)
    return ((xf - mu) * lax.rsqrt(var + EPS)).astype(x.dtype) * g + b


def to_blocks(k):
    b, t, h, d = k.shape
    nb = -(-t // MOBA_BLOCK)
    k = jnp.pad(k, ((0, 0), (0, nb * MOBA_BLOCK - t), (0, 0), (0, 0)))
    return k.reshape(b, nb, MOBA_BLOCK, h, d).transpose(0, 3, 1, 2, 4)


def block_means(kb):
    return jnp.mean(kb.astype(jnp.float32), axis=3)


def moba_attend(q, kb, vb, k_mean, q_pos):
    bsz, nq = q.shape[0], q.shape[1]
    nb = kb.shape[2]
    own = q_pos // MOBA_BLOCK
    qf = q.astype(jnp.float32)
    blk = jnp.einsum('bqhd,bhnd->bhqn', qf, k_mean)
    past = jnp.arange(nb)[None, :] < own[:, None]
    blk = jnp.where(past[None, None], blk, -jnp.inf)
    n_top = min(MOBA_TOPK, nb)
    _, top_idx = lax.top_k(blk, n_top)
    top_idx = top_idx.astype(jnp.int32)
    own_b = jnp.broadcast_to(own[None, None, :, None].astype(jnp.int32), top_idx.shape[:3] + (1,))
    sel = jnp.concatenate([top_idx, own_b], axis=-1)
    sel_ok = jnp.concatenate([top_idx < own[None, None, :, None], jnp.ones(own_b.shape, dtype=bool)], axis=-1)
    b_ix = jnp.arange(bsz)[:, None, None, None]
    h_ix = jnp.arange(N_HEADS)[None, :, None, None]
    k_g = kb[b_ix, h_ix, sel]
    v_g = vb[b_ix, h_ix, sel]
    key_pos = sel[..., None] * MOBA_BLOCK + jnp.arange(MOBA_BLOCK, dtype=jnp.int32)
    dist = q_pos[None, None, :, None, None] - key_pos
    ok = sel_ok[..., None] & (dist >= 0)
    s = jnp.einsum('bqhd,bhqjkd->bhqjk', qf, k_g.astype(jnp.float32)) * (HEAD_DIM ** -0.5)
    s = s - alibi_slopes()[None, :, None, None, None] * dist.astype(jnp.float32)
    s = jnp.where(ok, s, -jnp.inf)
    shp = s.shape
    p = jax.nn.softmax(s.reshape(shp[:3] + (-1,)), axis=-1).reshape(shp)
    o = jnp.einsum('bhqjk,bhqjkd->bqhd', p.astype(v_g.dtype), v_g)
    return o.reshape(bsz, nq, D_ATTN)


def attend_prompt(q, k, v):
    b, s = q.shape[0], q.shape[1]
    kb, vb = to_blocks(k), to_blocks(v)
    km = block_means(kb)
    nqb = s // Q_BLOCK
    qb = q.reshape(b, nqb, Q_BLOCK, N_HEADS, HEAD_DIM).transpose(1, 0, 2, 3, 4)
    pos = jnp.arange(s, dtype=jnp.int32).reshape(nqb, Q_BLOCK)
    o = lax.map(lambda a: moba_attend(a[0], kb, vb, km, a[1]), (qb, pos))
    return o.transpose(1, 0, 2, 3).reshape(b, s, D_ATTN)


def attend_sample(q, k, v, past_k, past_v):
    past_len = past_k.shape[1]
    kb = to_blocks(jnp.concatenate([past_k.astype(k.dtype), k], axis=1))
    vb = to_blocks(jnp.concatenate([past_v.astype(v.dtype), v], axis=1))
    km = block_means(kb)
    pos = past_len + jnp.arange(q.shape[1], dtype=jnp.int32)
    return moba_attend(q, kb, vb, km, pos)


def conformer_conv(u_in, prefix, w_dw, b_dw, ln_g, ln_b, w_pw2):
    a, g = jnp.split(u_in, 2, axis=-1)
    u = a * jax.nn.sigmoid(g)
    full = jnp.concatenate([prefix.astype(u.dtype), u], axis=1)
    y = lax.conv_general_dilated(full, w_dw[:, None, :].astype(full.dtype), window_strides=(1,), padding='VALID',
                                 dimension_numbers=('NWC', 'WIO', 'NWC'), feature_group_count=D_CONV) + b_dw
    y = layer_norm(y, ln_g, ln_b)
    y = y * jax.nn.sigmoid(y)
    return y @ w_pw2, full[:, -(CONV_WIDTH - 1):]


def moe(h, w_router, b_router, w_gate, w_up, w_down):
    b, t, d = h.shape
    xt = h.reshape(-1, d)
    n = xt.shape[0]
    logits = (xt @ w_router + b_router).astype(jnp.float32)
    top_val, top_idx = lax.top_k(logits, TOP_K)
    gates = jax.nn.softmax(top_val, axis=-1)
    n_assign = n * TOP_K
    e_flat = top_idx.reshape(-1).astype(jnp.int32)
    t_flat = jnp.repeat(jnp.arange(n, dtype=jnp.int32), TOP_K)
    g_flat = gates.reshape(-1)
    order = jnp.argsort(e_flat)
    e_sorted = e_flat[order]
    counts = jnp.zeros((N_EXPERTS,), jnp.int32).at[e_flat].add(1)
    start = jnp.cumsum(counts) - counts
    padded = (counts + MOE_BLOCK - 1) // MOE_BLOCK * MOE_BLOCK
    pend = jnp.cumsum(padded)
    pstart = pend - padded
    dest = pstart[e_sorted] + (jnp.arange(n_assign, dtype=jnp.int32) - start[e_sorted])
    n_blocks = -(-n_assign // MOE_BLOCK) + N_EXPERTS
    cap = n_blocks * MOE_BLOCK
    slot_tok = jnp.zeros((cap,), jnp.int32).at[dest].set(t_flat[order])
    slot_w = jnp.zeros((cap,), jnp.float32).at[dest].set(g_flat[order])
    blk_exp = jnp.minimum(jnp.searchsorted(pend, jnp.arange(n_blocks, dtype=jnp.int32) * MOE_BLOCK, side='right'),
                          N_EXPERTS - 1)

    def expert_block(args):
        tok, e = args
        xb = xt[tok]
        gt = jnp.minimum(xb @ w_gate[e], SWIGLU_LIMIT)
        up = jnp.clip(xb @ w_up[e], -SWIGLU_LIMIT, SWIGLU_LIMIT)
        act = gt * jax.nn.sigmoid(SWIGLU_ALPHA * gt) * (up + 1)
        return act @ w_down[e]

    yb = lax.map(expert_block, (slot_tok.reshape(n_blocks, MOE_BLOCK), blk_exp))
    out = jnp.zeros_like(xt).at[slot_tok].add(yb.reshape(cap, d) * slot_w[:, None].astype(xt.dtype))
    return out.reshape(b, t, d)


def decoder_layer(x, c, attend, conv_prefix, norm1_g, norm2_g, w_ada, b_ada, w_in, w_attn_out, w_dw, b_dw,
                  ln_g, ln_b, w_pw2, w_out, w_router, b_router, w_gate, w_up, w_down):
    b, t, _ = x.shape
    ada = (c @ w_ada + b_ada)[:, None, :]
    sh1, sc1, g1, sh2, sc2, g2 = jnp.split(ada, 6, axis=-1)
    h = rms_norm(x, norm1_g) * (1 + sc1) + sh1
    z = h @ w_in
    q, k, v, u_in, ga, gb = jnp.split(z, SPLITS, axis=-1)
    q = q.reshape(b, t, N_HEADS, HEAD_DIM)
    k = k.reshape(b, t, N_HEADS, HEAD_DIM)
    v = v.reshape(b, t, N_HEADS, HEAD_DIM)
    a_out = attend(q, k, v) @ w_attn_out
    c_out, conv_state = conformer_conv(u_in, conv_prefix, w_dw, b_dw, ln_g, ln_b, w_pw2)
    merged = jax.nn.sigmoid(ga) * a_out + jax.nn.sigmoid(gb) * c_out
    x = x + g1 * (merged @ w_out)
    h2 = rms_norm(x, norm2_g) * (1 + sc2) + sh2
    x = x + g2 * moe(h2, w_router, b_router, w_gate, w_up, w_down)
    return x, k, v, conv_state


def setup_inputs(seed: int = 0) -> dict:
    key = jax.random.key(seed)
    ks = jax.random.split(key, 28)

    def nrm(k, shp, s):
        return jax.random.normal(k, shp, jnp.float32) * s

    n_pages = PAST_LEN // PAGE_SIZE
    n_phys = (DEC_BATCH * n_pages * 5) // 4
    page_table = jax.random.permutation(ks[7], n_phys)[: DEC_BATCH * n_pages].reshape(DEC_BATCH, n_pages).astype(jnp.int32)
    return {
        'x_prompt': nrm(ks[0], (BATCH, SEQ, D_MODEL), 1.0),
        'x_sample': nrm(ks[1], (DEC_BATCH, DEC_SEQ, D_MODEL), 1.0),
        'c_prompt': nrm(ks[2], (BATCH, D_MODEL), 1.0),
        'c_sample': nrm(ks[3], (DEC_BATCH, D_MODEL), 1.0),
        'cache_k': nrm(ks[4], (DEPTH, n_phys, PAGE_SIZE, N_HEADS, HEAD_DIM), 1.0),
        'cache_v': nrm(ks[5], (DEPTH, n_phys, PAGE_SIZE, N_HEADS, HEAD_DIM), 1.0),
        'state_conv': nrm(ks[6], (DEPTH, DEC_BATCH, CONV_WIDTH - 1, D_CONV), 0.7),
        'page_table': page_table,
        'norm1_g': 1.0 + nrm(ks[8], (DEPTH, D_MODEL), 0.02),
        'norm2_g': 1.0 + nrm(ks[9], (DEPTH, D_MODEL), 0.02),
        'w_ada': nrm(ks[10], (DEPTH, D_MODEL, 6 * D_MODEL), 0.5 * D_MODEL ** -0.5),
        'b_ada': nrm(ks[11], (DEPTH, 6 * D_MODEL), 0.01),
        'w_in': nrm(ks[12], (DEPTH, D_MODEL, D_IN), D_MODEL ** -0.5),
        'w_attn_out': nrm(ks[13], (DEPTH, D_ATTN, D_MODEL), D_ATTN ** -0.5),
        'w_dw': nrm(ks[14], (DEPTH, CONV_WIDTH, D_CONV), CONV_WIDTH ** -0.5),
        'b_dw': nrm(ks[15], (DEPTH, D_CONV), 0.01),
        'ln_g': 1.0 + nrm(ks[16], (DEPTH, D_CONV), 0.02),
        'ln_b': nrm(ks[17], (DEPTH, D_CONV), 0.01),
        'w_pw2': nrm(ks[18], (DEPTH, D_CONV, D_MODEL), D_CONV ** -0.5),
        'w_out': nrm(ks[19], (DEPTH, D_MODEL, D_MODEL), D_MODEL ** -0.5),
        'w_router': nrm(ks[20], (DEPTH, D_MODEL, N_EXPERTS), D_MODEL ** -0.5),
        'b_router': nrm(ks[21], (DEPTH, N_EXPERTS), 0.01),
        'w_gate': nrm(ks[22], (DEPTH, N_EXPERTS, D_MODEL, D_FF), D_MODEL ** -0.5),
        'w_up': nrm(ks[23], (DEPTH, N_EXPERTS, D_MODEL, D_FF), D_MODEL ** -0.5),
        'w_down': nrm(ks[24], (DEPTH, N_EXPERTS, D_FF, D_MODEL), D_FF ** -0.5),
        'final_g': 1.0 + nrm(ks[25], (D_MODEL,), 0.02),
    }


def reference(x_prompt, x_sample, c_prompt, c_sample, cache_k, cache_v, state_conv, page_table,
              norm1_g, norm2_g, w_ada, b_ada, w_in, w_attn_out, w_dw, b_dw, ln_g, ln_b, w_pw2, w_out,
              w_router, b_router, w_gate, w_up, w_down, final_g):
    dec_b = x_sample.shape[0]
    xp, xs = x_prompt, x_sample
    kp_l, vp_l, cp_l, ks_l, vs_l, cs_l = [], [], [], [], [], []
    for l in range(DEPTH):
        lw = (norm1_g[l], norm2_g[l], w_ada[l], b_ada[l], w_in[l], w_attn_out[l], w_dw[l], b_dw[l],
              ln_g[l], ln_b[l], w_pw2[l], w_out[l], w_router[l], b_router[l], w_gate[l], w_up[l], w_down[l])
        zero_prefix = jnp.zeros((xp.shape[0], CONV_WIDTH - 1, D_CONV), xp.dtype)
        xp, kp, vp, cp = decoder_layer(xp, c_prompt, attend_prompt, zero_prefix, *lw)
        past_k = cache_k[l][page_table].reshape(dec_b, -1, N_HEADS, HEAD_DIM)
        past_v = cache_v[l][page_table].reshape(dec_b, -1, N_HEADS, HEAD_DIM)
        att_s = lambda q, k, v: attend_sample(q, k, v, past_k, past_v)
        xs, ks_, vs_, cs_ = decoder_layer(xs, c_sample, att_s, state_conv[l], *lw)
        kp_l.append(kp)
        vp_l.append(vp)
        cp_l.append(cp)
        ks_l.append(ks_)
        vs_l.append(vs_)
        cs_l.append(cs_)
    y_prompt = rms_norm(xp, final_g)
    y_sample = rms_norm(xs, final_g)
    return (y_prompt, y_sample, jnp.stack(kp_l), jnp.stack(vp_l), jnp.stack(cp_l),
            jnp.stack(ks_l), jnp.stack(vs_l), jnp.stack(cs_l))
```

```python
import functools

import jax
import jax.numpy as jnp
from jax import lax
from jax.experimental import pallas as pl
from jax.experimental.pallas import tpu as pltpu

F32 = jnp.float32
BF16 = jnp.bfloat16
I32 = jnp.int32

D_MODEL = 1024
N_HEADS = 8
HEAD_DIM = 64
D_ATTN = N_HEADS * HEAD_DIM
D_CONV = 512
CONV_WIDTH = 31
N_EXPERTS = 32
TOP_K = 4
MOBA_BLOCK = 256
MOBA_TOPK = 3
PAGE_SIZE = 128
SWIGLU_ALPHA = 1.702
SWIGLU_LIMIT = 7.0
EPS = 1e-6
D_IN = 3 * D_ATTN + 2 * D_CONV + 2 * D_MODEL

LANES = 128
HALO = 32
EXPERT_BLOCK = 256
HEADS_PER_TILE = LANES // HEAD_DIM
VMEM_LIMIT = 56 * 1024 * 1024

NT_DIMS = (((1,), (1,)), ((), ()))
NEG_INF = float("-inf")


def _params(n_axes, **kw):
    return pltpu.CompilerParams(dimension_semantics=("arbitrary",) * n_axes,
                                vmem_limit_bytes=VMEM_LIMIT, **kw)


def _const_spec(shape):
    nd = len(shape)
    return pl.BlockSpec(shape, lambda *_: (0,) * nd, pipeline_mode=pl.Buffered(1))


def _rms(x, g):
    return (x * lax.rsqrt(jnp.mean(x * x, axis=-1, keepdims=True) + EPS)) * g


def _sigmoid(x):
    return 1.0 / (1.0 + jnp.exp(-x))


def _ada_kernel(c_ref, w_ref, b_ref, o_ref):
    o_ref[...] = jnp.dot(c_ref[...].astype(BF16), w_ref[...].astype(BF16),
                         preferred_element_type=F32) + b_ref[...]


def _ada(c, w_ada, b_ada):
    rows = c.shape[0]
    return pl.pallas_call(
        _ada_kernel,
        grid=(6,),
        in_specs=[pl.BlockSpec((rows, D_MODEL), lambda j: (0, 0)),
                  pl.BlockSpec((D_MODEL, D_MODEL), lambda j: (0, j)),
                  pl.BlockSpec((1, D_MODEL), lambda j: (0, j))],
        out_specs=pl.BlockSpec((rows, D_MODEL), lambda j: (0, j)),
        out_shape=jax.ShapeDtypeStruct((rows, 6 * D_MODEL), F32),
        compiler_params=_params(1),
        name="ada",
    )(c, w_ada, b_ada.reshape(1, -1))


def _mod_spec(per_token, tm, tiles_per_seq):
    if per_token:
        return pl.BlockSpec((tm, 6 * D_MODEL), lambda i: (i, 0))
    return pl.BlockSpec((1, 1, 6 * D_MODEL), lambda i: (i // tiles_per_seq, 0, 0))


def _load_mod(mod_ref, per_token):
    return mod_ref[...] if per_token else mod_ref[0]


def _inproj_kernel(x_ref, mod_ref, g_ref, w_ref, q_ref, k_ref, v_ref, kb_ref, vb_ref, u_ref, sg_ref,
                   *km_ref, per_token):
    mod = _load_mod(mod_ref, per_token)
    sh1, sc1 = mod[:, 0:D_MODEL], mod[:, D_MODEL:2 * D_MODEL]
    h = (_rms(x_ref[...], g_ref[...]) * (1.0 + sc1) + sh1).astype(BF16)

    def proj(lo, hi):
        return jnp.dot(h, w_ref[:, lo:hi], preferred_element_type=F32)

    q_ref[...] = proj(0, D_ATTN)
    k = proj(D_ATTN, 2 * D_ATTN)
    k_ref[...] = k
    kb_ref[...] = k.astype(BF16)
    v = proj(2 * D_ATTN, 3 * D_ATTN)
    v_ref[...] = v
    vb_ref[...] = v.astype(BF16)
    o = 3 * D_ATTN
    u_ref[...] = proj(o, o + D_CONV) * _sigmoid(proj(o + D_CONV, o + 2 * D_CONV))
    o += 2 * D_CONV
    sg_ref[...] = _sigmoid(proj(o, o + 2 * D_MODEL))
    if km_ref:
        km_ref[0][0] = jnp.mean(k, axis=0, keepdims=True)


def _inproj(x, mod, norm_g, w_in_bf, *, per_token, tm, tiles_per_seq):
    n = x.shape[0]
    row = lambda i: (i, 0)
    outs = [(D_ATTN, F32), (D_ATTN, F32), (D_ATTN, F32), (D_ATTN, BF16), (D_ATTN, BF16),
            (D_CONV, F32), (2 * D_MODEL, F32)]
    out_shape = [jax.ShapeDtypeStruct((n, w), dt) for w, dt in outs]
    out_specs = [pl.BlockSpec((tm, w), row) for w, _ in outs]
    if not per_token:
        assert tm == MOBA_BLOCK
        out_shape.append(jax.ShapeDtypeStruct((n // tm, 1, D_ATTN), F32))
        out_specs.append(pl.BlockSpec((1, 1, D_ATTN), lambda i: (i, 0, 0)))
    return pl.pallas_call(
        functools.partial(_inproj_kernel, per_token=per_token),
        grid=(n // tm,),
        in_specs=[pl.BlockSpec((tm, D_MODEL), row), _mod_spec(per_token, tm, tiles_per_seq),
                  _const_spec((1, D_MODEL)), _const_spec((D_MODEL, D_IN))],
        out_specs=out_specs,
        out_shape=out_shape,
        compiler_params=_params(1),
        name="inproj",
    )(x, mod, norm_g, w_in_bf)


def _select_top(g, idx, n_pick, n_idx, axis):
    sel = jnp.zeros(g.shape, F32)
    for _ in range(n_pick):
        mx = jnp.max(g, axis=axis, keepdims=True)
        first = jnp.min(jnp.where(g == mx, idx, float(n_idx)), axis=axis, keepdims=True)
        hit = idx == first
        sel = jnp.where(hit & (mx > NEG_INF), 1.0, sel)
        g = jnp.where(hit, NEG_INF, g)
    return sel


def _attn_kernel(slopes_ref, q_ref, k_ref, v_ref, km_ref, o_ref, bias_ref):
    pair, own = pl.program_id(1), pl.program_id(2)
    blk = MOBA_BLOCK
    n_blk = km_ref.shape[1]
    row = lax.broadcasted_iota(I32, (blk, blk), 0)
    col = lax.broadcasted_iota(I32, (blk, blk), 1)

    @pl.when(own == 0)
    def _():
        rel = (row - col).astype(F32)
        for hh in range(HEADS_PER_TILE):
            bias_ref[hh] = -slopes_ref[pair * HEADS_PER_TILE + hh] * rel

    q = q_ref[...]
    lane = lax.broadcasted_iota(I32, (1, LANES), 1)
    low = lane < HEAD_DIM
    km = km_ref[0]
    bidx = lax.broadcasted_iota(I32, (blk, n_blk), 1)
    bidx_f = bidx.astype(F32)
    k_own = k_ref[pl.ds(pl.multiple_of(own * blk, blk), blk), :]
    v_own = v_ref[pl.ds(pl.multiple_of(own * blk, blk), blk), :]

    qs, sels, ms, ls, pvs = [], [], [], [], []
    for hh in range(HEADS_PER_TILE):
        qh = jnp.where((lane // HEAD_DIM) == hh, q, 0.0)
        gate = lax.dot_general(qh, km, NT_DIMS, precision=lax.Precision.HIGHEST,
                               preferred_element_type=F32)
        gate = jnp.where(bidx < own, gate, NEG_INF)
        sels.append(_select_top(gate, bidx_f, MOBA_TOPK, n_blk, 1))
        qb = (qh * (HEAD_DIM ** -0.5)).astype(BF16)
        qs.append(qb)
        s = lax.dot_general(qb, k_own, NT_DIMS, preferred_element_type=F32) + bias_ref[hh]
        s = jnp.where(col <= row, s, NEG_INF)
        m = jnp.max(s, axis=1, keepdims=True)
        p = jnp.exp(s - m)
        ms.append(m)
        ls.append(jnp.sum(p, axis=1, keepdims=True))
        pvs.append(jnp.dot(p.astype(BF16), v_own, preferred_element_type=F32))
    acc0 = jnp.where(low, pvs[0], pvs[1])

    def body(n, carry):
        m0, l0, m1, l1, acc = carry
        off = pl.multiple_of(n * blk, blk)
        kb = k_ref[pl.ds(off, blk), :]
        vb = v_ref[pl.ds(off, blk), :]
        dist = ((own - n) * blk).astype(F32)
        new, alphas, pv = [], [], []
        for hh, (m, l) in enumerate(((m0, l0), (m1, l1))):
            picked = jnp.sum(jnp.where(bidx == n, sels[hh], 0.0), axis=1, keepdims=True) > 0.0
            c = -slopes_ref[pair * HEADS_PER_TILE + hh] * dist
            s = lax.dot_general(qs[hh], kb, NT_DIMS, preferred_element_type=F32) + bias_ref[hh]
            m_blk = jnp.where(picked, jnp.max(s, axis=1, keepdims=True) + c, NEG_INF)
            m_new = jnp.maximum(m, m_blk)
            p = jnp.exp(s - jnp.where(picked, m_new - c, float("inf")))
            alpha = jnp.exp(m - m_new)
            new += [m_new, alpha * l + jnp.sum(p, axis=1, keepdims=True)]
            alphas.append(alpha)
            pv.append(jnp.dot(p.astype(BF16), vb, preferred_element_type=F32))
        acc = acc * jnp.where(low, alphas[0], alphas[1]) + jnp.where(low, pv[0], pv[1])
        return (*new, acc)

    _, l0, _, l1, acc = lax.fori_loop(0, own, body, (ms[0], ls[0], ms[1], ls[1], acc0))
    o_ref[...] = (acc / jnp.where(low, l0, l1)).astype(o_ref.dtype)


def _attend_prompt(q, k_bf, v_bf, kmean, slopes, batch, seq):
    n_qb = seq // MOBA_BLOCK
    tile = lambda b, p, i: (b * n_qb + i, p)
    return pl.pallas_call(
        _attn_kernel,
        grid=(batch, N_HEADS // HEADS_PER_TILE, n_qb),
        in_specs=[pl.BlockSpec(memory_space=pltpu.SMEM),
                  pl.BlockSpec((MOBA_BLOCK, LANES), tile),
                  pl.BlockSpec((seq, LANES), lambda b, p, i: (b, p)),
                  pl.BlockSpec((seq, LANES), lambda b, p, i: (b, p)),
                  pl.BlockSpec((1, n_qb, LANES), lambda b, p, i: (b, 0, p))],
        out_specs=pl.BlockSpec((MOBA_BLOCK, LANES), tile),
        out_shape=jax.ShapeDtypeStruct((batch * seq, D_ATTN), BF16),
        scratch_shapes=[pltpu.VMEM((HEADS_PER_TILE, MOBA_BLOCK, MOBA_BLOCK), F32)],
        compiler_params=_params(3),
        name="attn_prompt",
    )(slopes, q, k_bf, v_bf, kmean.reshape(batch, n_qb, D_ATTN))


def _decode_attn_kernel(pt_ref, qt_ref, knt_ref, vnt_ref, slope_ref, *refs, n_pages):
    k_pages, v_pages, o_ref = refs[:n_pages], refs[n_pages:2 * n_pages], refs[2 * n_pages]
    pages_per_blk = MOBA_BLOCK // PAGE_SIZE
    n_blk = n_pages // pages_per_blk
    past_len = n_pages * PAGE_SIZE
    req = pl.program_id(0)
    mine = lax.broadcasted_iota(I32, (D_ATTN, LANES), 1) == req

    @pl.when(req == 0)
    def _():
        o_ref[...] = jnp.zeros_like(o_ref)

    def column(ref):
        return jnp.sum(jnp.where(mine, ref[...], 0.0), axis=1, keepdims=True)

    def head_sums(x):
        head = lax.broadcasted_iota(I32, (N_HEADS, x.shape[1]), 0)
        out = jnp.zeros((N_HEADS, x.shape[1]), F32)
        for h in range(N_HEADS):
            part = jnp.sum(x[h * HEAD_DIM:(h + 1) * HEAD_DIM], axis=0, keepdims=True)
            out = jnp.where(head == h, part, out)
        return out

    def head_rows(x):
        return jnp.concatenate([jnp.broadcast_to(x[h:h + 1], (HEAD_DIM, x.shape[1]))
                                for h in range(N_HEADS)], axis=0)

    q, k_new, v_new = column(qt_ref), column(knt_ref), column(vnt_ref)
    raw = [head_sums(k_pages[p][0] * q) for p in range(n_pages)]

    lane = lax.broadcasted_iota(I32, (N_HEADS, LANES), 1)
    gate = jnp.full((N_HEADS, LANES), NEG_INF, F32)
    for n in range(n_blk):
        tot = raw[n * pages_per_blk]
        for j in range(1, pages_per_blk):
            tot = tot + raw[n * pages_per_blk + j]
        gate = jnp.where(lane == n, jnp.sum(tot, axis=1, keepdims=True) * (1.0 / MOBA_BLOCK), gate)
    sel = _select_top(gate, lane.astype(F32), MOBA_TOPK, LANES, 1)

    slope = slope_ref[...]
    scale = HEAD_DIM ** -0.5
    scores = []
    for p in range(n_pages):
        dist = (past_len - p * PAGE_SIZE - lane).astype(F32)
        n = p // pages_per_blk
        scores.append(jnp.where(sel[:, n:n + 1] > 0.0, raw[p] * scale - slope * dist, NEG_INF))
    s_new = head_sums(q * k_new) * scale

    m = s_new
    for s in scores:
        m = jnp.maximum(m, jnp.max(s, axis=1, keepdims=True))
    p_new = jnp.exp(s_new - m)
    l = p_new
    acc = jnp.zeros((D_ATTN, PAGE_SIZE), F32)
    for p in range(n_pages):
        prob = jnp.exp(scores[p] - m)
        l = l + jnp.sum(prob, axis=1, keepdims=True)
        acc = acc + v_pages[p][0] * head_rows(prob)
    out = (jnp.sum(acc, axis=1, keepdims=True) + head_rows(p_new) * v_new) / head_rows(l)
    o_ref[...] = jnp.where(mine, out, o_ref[...])


def _attend_sample(q, k_new, v_new, cache_k, cache_v, page_table, slopes_rows):
    n_req, n_pages = page_table.shape
    assert n_req == LANES
    n_phys = cache_k.shape[0]
    as_tiles = lambda c: c.transpose(0, 2, 3, 1).reshape(n_phys, D_ATTN, PAGE_SIZE)
    ck, cv = as_tiles(cache_k), as_tiles(cache_v)
    cols = pl.BlockSpec((D_ATTN, n_req), lambda r, pt: (0, 0))

    def page_spec(p):
        return pl.BlockSpec((1, D_ATTN, PAGE_SIZE), lambda r, pt: (pt[r * n_pages + p], 0, 0))

    pages = [page_spec(p) for p in range(n_pages)]
    grid_spec = pltpu.PrefetchScalarGridSpec(
        num_scalar_prefetch=1,
        grid=(n_req,),
        in_specs=[cols, cols, cols, pl.BlockSpec((N_HEADS, LANES), lambda r, pt: (0, 0))] + pages + pages,
        out_specs=cols,
    )
    out_t = pl.pallas_call(
        functools.partial(_decode_attn_kernel, n_pages=n_pages),
        grid_spec=grid_spec,
        out_shape=jax.ShapeDtypeStruct((D_ATTN, n_req), F32),
        compiler_params=_params(1),
        name="attn_sample",
    )(page_table.reshape(-1), q.T, k_new.T, v_new.T, slopes_rows,
      *([ck] * n_pages), *([cv] * n_pages))
    return out_t.T


def _branch_merge(y_conv, x, attn_bf, sg, mod, w, xo_ref, h2_ref, eidx_ref, gate_ref):
    (b_dw, ln_g, ln_b, w_ao, w_pw2, w_out, n2g, w_router, b_router) = w
    y = y_conv + b_dw[...]
    mu = jnp.mean(y, axis=-1, keepdims=True)
    yc = y - mu
    var = jnp.mean(yc * yc, axis=-1, keepdims=True)
    y = (yc * lax.rsqrt(var + EPS)) * ln_g[...] + ln_b[...]
    y = y * _sigmoid(y)
    c_out = jnp.dot(y.astype(BF16), w_pw2[...], preferred_element_type=F32)
    a_out = jnp.dot(attn_bf, w_ao[...], preferred_element_type=F32)
    merged = sg[:, 0:D_MODEL] * a_out + sg[:, D_MODEL:2 * D_MODEL] * c_out
    g1 = mod[:, 2 * D_MODEL:3 * D_MODEL]
    xo = x + g1 * jnp.dot(merged.astype(BF16), w_out[...], preferred_element_type=F32)
    xo_ref[...] = xo
    sh2, sc2 = mod[:, 3 * D_MODEL:4 * D_MODEL], mod[:, 4 * D_MODEL:5 * D_MODEL]
    h2 = _rms(xo, n2g[...]) * (1.0 + sc2) + sh2
    h2_ref[...] = h2

    logits = jnp.dot(h2, w_router[...], precision=lax.Precision.HIGHEST,
                     preferred_element_type=F32) + b_router[...]
    tm = logits.shape[0]
    lane = lax.broadcasted_iota(I32, (tm, LANES), 1)
    logits = jnp.where(lane < N_EXPERTS, logits, NEG_INF)
    eidx = lane.astype(F32)
    vals, e_out = [], jnp.zeros((tm, LANES), F32)
    for kk in range(TOP_K):
        mx = jnp.max(logits, axis=1, keepdims=True)
        first = jnp.min(jnp.where(logits == mx, eidx, float(N_EXPERTS)), axis=1, keepdims=True)
        logits = jnp.where(eidx == first, NEG_INF, logits)
        vals.append(mx)
        e_out = jnp.where(lane == kk, first, e_out)
    ex = [jnp.exp(v - vals[0]) for v in vals]
    den = ex[0] + ex[1] + ex[2] + ex[3]
    g_out = jnp.zeros((tm, LANES), F32)
    for kk in range(TOP_K):
        g_out = jnp.where(lane == kk, ex[kk] / den, g_out)
    eidx_ref[...] = e_out.astype(I32)
    gate_ref[...] = g_out


def _merge_prompt_kernel(x_ref, attn_ref, u_ref, halo_ref, sg_ref, mod_ref, wdw_ref, *refs, tiles_per_seq):
    w, (xo_ref, h2_ref, eidx_ref, gate_ref, full_ref) = refs[:9], refs[9:]
    tm = u_ref.shape[0]
    first = (pl.program_id(0) % tiles_per_seq) == 0
    full_ref[0:HALO, :] = jnp.where(first, 0.0, halo_ref[...])
    full_ref[HALO:HALO + tm, :] = u_ref[...]
    base = HALO - (CONV_WIDTH - 1)
    y = jnp.zeros((tm, D_CONV), F32)
    for j in range(CONV_WIDTH):
        y = y + wdw_ref[j:j + 1, :] * full_ref[base + j:base + j + tm, :]
    _branch_merge(y, x_ref[...], attn_ref[...], sg_ref[...], mod_ref[0], w,
                  xo_ref, h2_ref, eidx_ref, gate_ref)


def _merge_sample_kernel(x_ref, attn_ref, u_ref, state_ref, sg_ref, mod_ref, wdw_ref, *refs):
    w, (xo_ref, h2_ref, eidx_ref, gate_ref) = refs[:9], refs[9:]
    y = wdw_ref[CONV_WIDTH - 1:CONV_WIDTH, :] * u_ref[...]
    for j in range(CONV_WIDTH - 1):
        y = y + wdw_ref[j:j + 1, :] * state_ref[j]
    _branch_merge(y, x_ref[...], attn_ref[...].astype(BF16), sg_ref[...], mod_ref[...], w,
                  xo_ref, h2_ref, eidx_ref, gate_ref)


def _merge(x, attn, u, hist, sg, mod, weights, *, per_token, tm, tiles_per_seq):
    n = x.shape[0]
    row = lambda i: (i, 0)
    if per_token:
        kernel = _merge_sample_kernel
        hist_spec = _const_spec(hist.shape)
        scratch = []
    else:
        kernel = functools.partial(_merge_prompt_kernel, tiles_per_seq=tiles_per_seq)
        hist_spec = pl.BlockSpec((HALO, D_CONV), lambda i: (jnp.maximum(i * (tm // HALO) - 1, 0), 0))
        scratch = [pltpu.VMEM((HALO + tm, D_CONV), F32)]
    outs = [(D_MODEL, F32), (D_MODEL, F32), (LANES, I32), (LANES, F32)]
    return pl.pallas_call(
        kernel,
        grid=(n // tm,),
        in_specs=[pl.BlockSpec((tm, D_MODEL), row), pl.BlockSpec((tm, D_ATTN), row),
                  pl.BlockSpec((tm, D_CONV), row), hist_spec, pl.BlockSpec((tm, 2 * D_MODEL), row),
                  _mod_spec(per_token, tm, tiles_per_seq)] + [_const_spec(a.shape) for a in weights],
        out_specs=[pl.BlockSpec((tm, w), row) for w, _ in outs],
        out_shape=[jax.ShapeDtypeStruct((n, w), dt) for w, dt in outs],
        scratch_shapes=scratch,
        compiler_params=_params(1),
        name="merge",
    )(x, attn, u, hist, sg, mod, *weights)


def _onehots(e, lane):
    return [(e[:, kk:kk + 1] == lane) for kk in range(TOP_K)]


def _rank_kernel(e_ref, rank_ref, count_ref, carry_ref):
    tm = e_ref.shape[0]

    @pl.when(pl.program_id(0) == 0)
    def _():
        carry_ref[...] = jnp.zeros_like(carry_ref)

    lane = lax.broadcasted_iota(I32, (tm, LANES), 1)
    hots = _onehots(e_ref[...], lane)
    hot = jnp.zeros((tm, LANES), F32)
    for h in hots:
        hot = hot + h.astype(F32)
    r = lax.broadcasted_iota(I32, (tm, tm), 0)
    c = lax.broadcasted_iota(I32, (tm, tm), 1)
    earlier = (c < r).astype(BF16)
    before = carry_ref[...] + jnp.dot(earlier, hot.astype(BF16), preferred_element_type=F32)
    rank = jnp.zeros((tm, LANES), F32)
    for kk, h in enumerate(hots):
        rank = jnp.where(lane == kk, jnp.sum(jnp.where(h, before, 0.0), axis=1, keepdims=True), rank)
    rank_ref[...] = rank.astype(I32)
    carry_ref[...] = carry_ref[...] + jnp.sum(hot, axis=0, keepdims=True)
    count_ref[...] = jnp.broadcast_to(carry_ref[...], count_ref.shape)


def _slot_kernel(e_ref, rank_ref, count_ref, dest_ref, blk_ref, info_ref, *, n_blocks_pad):
    tm = e_ref.shape[0]
    counts = count_ref[0:1, :].astype(I32)
    padded = ((counts + (EXPERT_BLOCK - 1)) // EXPERT_BLOCK) * EXPERT_BLOCK
    r = lax.broadcasted_iota(I32, (LANES, LANES), 0)
    c = lax.broadcasted_iota(I32, (LANES, LANES), 1)
    pstart = jnp.dot(jnp.broadcast_to(padded.astype(F32), (8, LANES)), (r < c).astype(F32),
                     precision=lax.Precision.HIGHEST, preferred_element_type=F32)[0:1]
    pend = pstart + padded.astype(F32)
    lane = lax.broadcasted_iota(I32, (tm, LANES), 1)
    e = e_ref[...]
    dest = rank_ref[...]
    for kk, h in enumerate(_onehots(e, lane)):
        off = jnp.sum(jnp.where(h, pstart, 0.0), axis=1, keepdims=True).astype(I32)
        dest = dest + jnp.where(lane == kk, off, 0)
    dest_ref[...] = dest

    lane1 = lax.broadcasted_iota(I32, (1, LANES), 1)
    used = jnp.max(pend, axis=1, keepdims=True)
    blk_start = (lax.broadcasted_iota(I32, (n_blocks_pad, 1), 0) * EXPERT_BLOCK).astype(F32)
    blk_start = jnp.minimum(blk_start, used - EXPERT_BLOCK)
    done = jnp.where((lane1 < N_EXPERTS) & (pend <= blk_start), 1.0, 0.0)
    blk_exp = jnp.minimum(jnp.sum(done, axis=1, keepdims=True), N_EXPERTS - 1.0)
    blk_ref[...] = jnp.broadcast_to(blk_exp, blk_ref.shape).astype(I32)
    row8 = lax.broadcasted_iota(I32, (8, LANES), 0)
    info = jnp.where(row8 == 0, pstart, jnp.where(row8 == 1, pend, jnp.where(row8 == 2, used, 0.0)))
    info_ref[...] = info.astype(I32)


def _route(eidx, tm, n_blocks):
    n = eidx.shape[0]
    row = lambda i: (i, 0)
    fixed = lambda i: (0, 0)
    rank, counts = pl.pallas_call(
        _rank_kernel,
        grid=(n // tm,),
        in_specs=[pl.BlockSpec((tm, LANES), row)],
        out_specs=[pl.BlockSpec((tm, LANES), row), pl.BlockSpec((8, LANES), fixed)],
        out_shape=[jax.ShapeDtypeStruct((n, LANES), I32), jax.ShapeDtypeStruct((8, LANES), F32)],
        scratch_shapes=[pltpu.VMEM((1, LANES), F32)],
        compiler_params=_params(1),
        name="moe_rank",
    )(eidx)
    n_blocks_pad = -(-n_blocks // 8) * 8
    dest, blk, info = pl.pallas_call(
        functools.partial(_slot_kernel, n_blocks_pad=n_blocks_pad),
        grid=(n // tm,),
        in_specs=[pl.BlockSpec((tm, LANES), row), pl.BlockSpec((tm, LANES), row),
                  pl.BlockSpec((8, LANES), fixed)],
        out_specs=[pl.BlockSpec((tm, LANES), row), pl.BlockSpec((n_blocks_pad, LANES), fixed),
                   pl.BlockSpec((8, LANES), fixed)],
        out_shape=[jax.ShapeDtypeStruct((n, LANES), I32), jax.ShapeDtypeStruct((n_blocks_pad, LANES), I32),
                   jax.ShapeDtypeStruct((8, LANES), I32)],
        compiler_params=_params(1),
        name="moe_slot",
    )(eidx, rank, counts)
    dest_flat = dest[:, :TOP_K].reshape(-1)
    blk_exp = blk[:n_blocks, 0]
    pend = info[1, :N_EXPERTS]
    n_used = jnp.right_shift(info[2, 0:1], EXPERT_BLOCK.bit_length() - 1)
    return dest_flat, blk_exp, pend, n_used


def _row_copy(src_ref, src_row, dst_ref, dst_row, sem):
    return pltpu.make_async_copy(src_ref.at[pl.ds(src_row, 1)], dst_ref.at[pl.ds(dst_row, 1)], sem)


def _dispatch_kernel(dest_ref, pend_ref, h_ref, xs_ref, zero_ref, sem):
    tm = h_ref.shape[0]

    @pl.when(pl.program_id(0) == 0)
    def _():
        zero_ref[...] = jnp.zeros_like(zero_ref)
        n_blocks = xs_ref.shape[0] // EXPERT_BLOCK
        first_unused = pend_ref[N_EXPERTS - 1] // EXPERT_BLOCK

        def fill(start):
            return pltpu.make_async_copy(zero_ref, xs_ref.at[pl.ds(pl.multiple_of(start, EXPERT_BLOCK), EXPERT_BLOCK)], sem)

        def last_block(e):
            return jnp.maximum(pend_ref[e] - EXPERT_BLOCK, 0)

        @pl.loop(0, N_EXPERTS)
        def _(e):
            fill(last_block(e)).start()

        @pl.loop(first_unused, n_blocks)
        def _(j):
            fill(j * EXPERT_BLOCK).start()

        @pl.loop(0, N_EXPERTS)
        def _(e):
            fill(last_block(e)).wait()

        @pl.loop(first_unused, n_blocks)
        def _(j):
            fill(j * EXPERT_BLOCK).wait()

    @pl.loop(0, tm)
    def _(t):
        for kk in range(TOP_K):
            _row_copy(h_ref, t, xs_ref, dest_ref[t * TOP_K + kk], sem).start()

    @pl.loop(0, tm)
    def _(t):
        for kk in range(TOP_K):
            _row_copy(h_ref, t, xs_ref, dest_ref[t * TOP_K + kk], sem).wait()


def _dispatch(h2, dest_flat, pend, tm, cap):
    n = h2.shape[0]
    return pl.pallas_call(
        _dispatch_kernel,
        grid=(n // tm,),
        in_specs=[pl.BlockSpec((tm * TOP_K,), lambda i: (i,), memory_space=pltpu.SMEM),
                  pl.BlockSpec(memory_space=pltpu.SMEM),
                  pl.BlockSpec((tm, D_MODEL), lambda i: (i, 0))],
        out_specs=pl.BlockSpec(memory_space=pl.ANY),
        out_shape=jax.ShapeDtypeStruct((cap, D_MODEL), F32),
        scratch_shapes=[pltpu.VMEM((EXPERT_BLOCK, D_MODEL), F32), pltpu.SemaphoreType.DMA(())],
        compiler_params=_params(1, has_side_effects=True),
        name="moe_dispatch",
    )(dest_flat, pend, h2)


def _expert_kernel(blk_ref, used_ref, x_ref, wg_ref, wu_ref, wd_ref, y_ref, wg_bf, wu_bf, wd_bf):
    j = pl.program_id(0)
    active = j < used_ref[0]
    changed = (j == 0) | (blk_ref[j] != blk_ref[jnp.maximum(j - 1, 0)])

    @pl.when(active & changed)
    def _():
        wg_bf[...] = wg_ref[0].astype(BF16)
        wu_bf[...] = wu_ref[0].astype(BF16)
        wd_bf[...] = wd_ref[0].astype(BF16)

    @pl.when(active)
    def _():
        x = x_ref[...].astype(BF16)
        gt = jnp.minimum(jnp.dot(x, wg_bf[...], preferred_element_type=F32), SWIGLU_LIMIT)
        up = jnp.clip(jnp.dot(x, wu_bf[...], preferred_element_type=F32), -SWIGLU_LIMIT, SWIGLU_LIMIT)
        act = gt * _sigmoid(SWIGLU_ALPHA * gt) * (up + 1.0)
        y_ref[...] = jnp.dot(act.astype(BF16), wd_bf[...], preferred_element_type=F32)

    @pl.when(jnp.logical_not(active))
    def _():
        y_ref[...] = jnp.zeros_like(y_ref)


def _experts(xs, blk_exp, n_used, w_gate, w_up, w_down):
    n_blocks = blk_exp.shape[0]
    d_ff = w_gate.shape[2]
    xrow = lambda j, blk, used: (jnp.minimum(j, used[0] - 1), 0)
    wsel = lambda j, blk, used: (blk[j], 0, 0)
    grid_spec = pltpu.PrefetchScalarGridSpec(
        num_scalar_prefetch=2,
        grid=(n_blocks,),
        in_specs=[pl.BlockSpec((EXPERT_BLOCK, D_MODEL), xrow),
                  pl.BlockSpec((1, D_MODEL, d_ff), wsel),
                  pl.BlockSpec((1, D_MODEL, d_ff), wsel),
                  pl.BlockSpec((1, d_ff, D_MODEL), wsel)],
        out_specs=pl.BlockSpec((EXPERT_BLOCK, D_MODEL), lambda j, blk, used: (j, 0)),
        scratch_shapes=[pltpu.VMEM((D_MODEL, d_ff), BF16), pltpu.VMEM((D_MODEL, d_ff), BF16),
                        pltpu.VMEM((d_ff, D_MODEL), BF16)],
    )
    return pl.pallas_call(
        _expert_kernel,
        grid_spec=grid_spec,
        out_shape=jax.ShapeDtypeStruct(xs.shape, F32),
        compiler_params=_params(1),
        name="moe_experts",
    )(blk_exp, n_used, xs, w_gate, w_up, w_down)


def _combine_kernel(dest_ref, y_ref, xo_ref, gate_ref, mod_ref, fg_ref, o_ref, buf_ref, sem, *, per_token):
    tm = xo_ref.shape[0]

    @pl.loop(0, tm)
    def _(t):
        for kk in range(TOP_K):
            _row_copy(y_ref, dest_ref[t * TOP_K + kk], buf_ref.at[kk], t, sem).start()

    @pl.loop(0, tm)
    def _(t):
        for kk in range(TOP_K):
            _row_copy(y_ref, dest_ref[t * TOP_K + kk], buf_ref.at[kk], t, sem).wait()

    gates = gate_ref[...]
    moe = gates[:, 0:1] * buf_ref[0]
    for kk in range(1, TOP_K):
        moe = moe + gates[:, kk:kk + 1] * buf_ref[kk]
    g2 = _load_mod(mod_ref, per_token)[:, 5 * D_MODEL:6 * D_MODEL]
    o_ref[...] = _rms(xo_ref[...] + g2 * moe, fg_ref[...])


def _combine(y, dest_flat, xo, gates, mod, final_g, *, per_token, tm, tiles_per_seq):
    n = xo.shape[0]
    row = lambda i: (i, 0)
    return pl.pallas_call(
        functools.partial(_combine_kernel, per_token=per_token),
        grid=(n // tm,),
        in_specs=[pl.BlockSpec((tm * TOP_K,), lambda i: (i,), memory_space=pltpu.SMEM),
                  pl.BlockSpec(memory_space=pl.ANY),
                  pl.BlockSpec((tm, D_MODEL), row), pl.BlockSpec((tm, LANES), row),
                  _mod_spec(per_token, tm, tiles_per_seq), _const_spec((1, D_MODEL))],
        out_specs=pl.BlockSpec((tm, D_MODEL), row),
        out_shape=jax.ShapeDtypeStruct((n, D_MODEL), F32),
        scratch_shapes=[pltpu.VMEM((TOP_K, tm, D_MODEL), F32), pltpu.SemaphoreType.DMA(())],
        compiler_params=_params(1),
        name="moe_combine",
    )(dest_flat, y, xo, gates, mod, final_g)


def _moe(h2, eidx, gates, xo, mod, final_g, w_gate, w_up, w_down, *, per_token, tm, tiles_per_seq):
    n = h2.shape[0]
    n_blocks = -(-n * TOP_K // EXPERT_BLOCK) + N_EXPERTS
    dest_flat, blk_exp, pend, n_used = _route(eidx, tm, n_blocks)
    xs = _dispatch(h2, dest_flat, pend, tm, n_blocks * EXPERT_BLOCK)
    y = _experts(xs, blk_exp, n_used, w_gate, w_up, w_down)
    return _combine(y, dest_flat, xo, gates, mod, final_g,
                    per_token=per_token, tm=tm, tiles_per_seq=tiles_per_seq)


def kernel(x_prompt, x_sample, c_prompt, c_sample, cache_k, cache_v, state_conv, page_table, norm1_g, norm2_g,
           w_ada, b_ada, w_in, w_attn_out, w_dw, b_dw, ln_g, ln_b, w_pw2, w_out, w_router, b_router, w_gate,
           w_up, w_down, final_g):
    depth = w_in.shape[0]
    assert depth == 1, "single-layer trunk"
    batch, seq, _ = x_prompt.shape
    n_req = x_sample.shape[0]
    assert x_sample.shape[1] == 1 and seq % MOBA_BLOCK == 0 and n_req % 8 == 0
    layer = 0
    row2 = lambda a: a.reshape(1, -1)

    pad = (-batch) % 8
    c_all = jnp.concatenate([c_prompt, jnp.zeros((pad, D_MODEL), F32), c_sample], axis=0)
    ada = _ada(c_all, w_ada[layer], b_ada[layer])
    mod_p = ada[:batch].reshape(batch, 1, 6 * D_MODEL)
    mod_s = ada[batch + pad:]

    w_in_bf = w_in[layer].astype(BF16)
    merge_w = (row2(b_dw[layer]), row2(ln_g[layer]), row2(ln_b[layer]), w_attn_out[layer].astype(BF16),
               w_pw2[layer].astype(BF16), w_out[layer].astype(BF16), row2(norm2_g[layer]),
               jnp.pad(w_router[layer], ((0, 0), (0, LANES - N_EXPERTS))),
               jnp.pad(row2(b_router[layer]), ((0, 0), (0, LANES - N_EXPERTS))))
    w_dw_pad = jnp.pad(w_dw[layer], ((0, HALO - CONV_WIDTH), (0, 0)))
    slopes = jnp.exp2(-8.0 * jnp.arange(1, N_HEADS + 1, dtype=F32) / N_HEADS)
    slopes_rows = jnp.broadcast_to(slopes[:, None], (N_HEADS, LANES))
    fg = row2(final_g)
    experts = (w_gate[layer], w_up[layer], w_down[layer])

    tm = MOBA_BLOCK
    tps = seq // tm
    xp = x_prompt.reshape(batch * seq, D_MODEL)
    q, k, v, k_bf, v_bf, u, sg, kmean = _inproj(xp, mod_p, row2(norm1_g[layer]), w_in_bf,
                                                per_token=False, tm=tm, tiles_per_seq=tps)
    attn = _attend_prompt(q, k_bf, v_bf, kmean, slopes, batch, seq)
    xo, h2, eidx, gates = _merge(xp, attn, u, u, sg, mod_p, (w_dw_pad,) + merge_w,
                                 per_token=False, tm=tm, tiles_per_seq=tps)
    y_prompt = _moe(h2, eidx, gates, xo, mod_p, fg, *experts, per_token=False, tm=tm, tiles_per_seq=tps)
    k_prompt = k.reshape(1, batch, seq, N_HEADS, HEAD_DIM)
    v_prompt = v.reshape(1, batch, seq, N_HEADS, HEAD_DIM)
    conv_prompt = u.reshape(batch, seq, D_CONV)[None, :, seq - (CONV_WIDTH - 1):]

    xs = x_sample.reshape(n_req, D_MODEL)
    qs, ks, vs, _, _, us, sgs = _inproj(xs, mod_s, row2(norm1_g[layer]), w_in_bf,
                                        per_token=True, tm=n_req, tiles_per_seq=1)
    attn_s = _attend_sample(qs, ks, vs, cache_k[layer], cache_v[layer], page_table, slopes_rows)
    state = state_conv[layer]
    xo_s, h2_s, eidx_s, gates_s = _merge(xs, attn_s, us, state.transpose(1, 0, 2), sgs, mod_s,
                                         (w_dw_pad,) + merge_w, per_token=True, tm=n_req, tiles_per_seq=1)
    y_sample = _moe(h2_s, eidx_s, gates_s, xo_s, mod_s, fg, *experts, per_token=True, tm=n_req, tiles_per_seq=1)
    k_sample = ks.reshape(1, n_req, 1, N_HEADS, HEAD_DIM)
    v_sample = vs.reshape(1, n_req, 1, N_HEADS, HEAD_DIM)
    conv_sample = jnp.concatenate([state[:, 1:], us[:, None, :]], axis=1)[None]

    return (y_prompt.reshape(batch, seq, D_MODEL), y_sample.reshape(n_req, 1, D_MODEL),
            k_prompt, v_prompt, conv_prompt, k_sample, v_sample, conv_sample)
```

```python
import functools

import jax
import jax.numpy as jnp
from jax import lax
from jax.experimental import pallas as pl
from jax.experimental.pallas import tpu as pltpu

F32 = jnp.float32
BF16 = jnp.bfloat16
I32 = jnp.int32

D_MODEL = 1024
N_HEADS = 8
HEAD_DIM = 64
D_ATTN = N_HEADS * HEAD_DIM
D_CONV = 512
CONV_WIDTH = 31
N_EXPERTS = 32
TOP_K = 4
MOBA_BLOCK = 256
MOBA_TOPK = 3
PAGE_SIZE = 128
SWIGLU_ALPHA = 1.702
SWIGLU_LIMIT = 7.0
EPS = 1e-6
D_IN = 3 * D_ATTN + 2 * D_CONV + 2 * D_MODEL

LANES = 128
HALO = 32
EXPERT_BLOCK = 256
HEADS_PER_TILE = LANES // HEAD_DIM
KV_GROUP = 4
VMEM_LIMIT = 56 * 1024 * 1024

NT_DIMS = (((1,), (1,)), ((), ()))
NEG_INF = float("-inf")


def _params(n_axes, **kw):
    return pltpu.CompilerParams(dimension_semantics=("arbitrary",) * n_axes,
                                vmem_limit_bytes=VMEM_LIMIT, **kw)


def _const_spec(shape):
    nd = len(shape)
    return pl.BlockSpec(shape, lambda *_: (0,) * nd, pipeline_mode=pl.Buffered(1))


def _rms(x, g):
    return (x * lax.rsqrt(jnp.mean(x * x, axis=-1, keepdims=True) + EPS)) * g


def _sigmoid(x):
    return 1.0 / (1.0 + jnp.exp(-x))


def _ada_kernel(c_ref, w_ref, b_ref, o_ref):
    o_ref[...] = jnp.dot(c_ref[...].astype(BF16), w_ref[...].astype(BF16),
                         preferred_element_type=F32) + b_ref[...]


def _ada(c, w_ada, b_ada):
    rows = c.shape[0]
    return pl.pallas_call(
        _ada_kernel,
        grid=(6,),
        in_specs=[pl.BlockSpec((rows, D_MODEL), lambda j: (0, 0)),
                  pl.BlockSpec((D_MODEL, D_MODEL), lambda j: (0, j)),
                  pl.BlockSpec((1, D_MODEL), lambda j: (0, j))],
        out_specs=pl.BlockSpec((rows, D_MODEL), lambda j: (0, j)),
        out_shape=jax.ShapeDtypeStruct((rows, 6 * D_MODEL), F32),
        compiler_params=_params(1),
        name="ada",
    )(c, w_ada, b_ada.reshape(1, -1))


def _mod_spec(per_token, tm, tiles_per_seq):
    if per_token:
        return pl.BlockSpec((tm, 6 * D_MODEL), lambda i: (i, 0))
    return pl.BlockSpec((1, 1, 6 * D_MODEL), lambda i: (i // tiles_per_seq, 0, 0))


def _load_mod(mod_ref, per_token):
    return mod_ref[...] if per_token else mod_ref[0]


def _inproj_common(x_ref, mod_ref, g_ref, w_ref, u_ref, sg_ref, per_token):
    mod = _load_mod(mod_ref, per_token)
    sh1, sc1 = mod[:, 0:D_MODEL], mod[:, D_MODEL:2 * D_MODEL]
    h = (_rms(x_ref[...], g_ref[...]) * (1.0 + sc1) + sh1).astype(BF16)

    def proj(lo, hi):
        return jnp.dot(h, w_ref[:, lo:hi], preferred_element_type=F32)

    o = 3 * D_ATTN
    u_ref[...] = proj(o, o + D_CONV) * _sigmoid(proj(o + D_CONV, o + 2 * D_CONV))
    o += 2 * D_CONV
    sg_ref[...] = _sigmoid(proj(o, o + 2 * D_MODEL))
    return h, proj


def _inproj_sample_kernel(x_ref, mod_ref, g_ref, w_ref, q_ref, k_ref, v_ref, u_ref, sg_ref):
    _, proj = _inproj_common(x_ref, mod_ref, g_ref, w_ref, u_ref, sg_ref, True)
    q_ref[...] = proj(0, D_ATTN)
    k_ref[...] = proj(D_ATTN, 2 * D_ATTN)
    v_ref[...] = proj(2 * D_ATTN, 3 * D_ATTN)


def _inproj_prompt_kernel(x_ref, mod_ref, g_ref, w_ref, wt_ref, q_ref, kb_ref, kt_ref, vt_ref, vtb_ref,
                          u_ref, sg_ref, km_ref):
    h, proj = _inproj_common(x_ref, mod_ref, g_ref, w_ref, u_ref, sg_ref, False)
    q_ref[...] = proj(0, D_ATTN)
    k = proj(D_ATTN, 2 * D_ATTN)
    kb_ref[...] = k.astype(BF16)
    km_ref[0] = jnp.mean(k, axis=0, keepdims=True)
    kt_ref[0] = lax.dot_general(wt_ref[0:D_ATTN, :], h, NT_DIMS, preferred_element_type=F32)
    vt = lax.dot_general(wt_ref[D_ATTN:2 * D_ATTN, :], h, NT_DIMS, preferred_element_type=F32)
    vt_ref[0] = vt
    vtb_ref[0] = vt.astype(BF16)


def _inproj(x, mod, norm_g, w_in_bf, w_kvt_bf=None, *, per_token, tm, tiles_per_seq):
    n = x.shape[0]
    row = lambda i: (i, 0)
    in_specs = [pl.BlockSpec((tm, D_MODEL), row), _mod_spec(per_token, tm, tiles_per_seq),
                _const_spec((1, D_MODEL)), _const_spec((D_MODEL, D_IN))]
    tail = [(D_CONV, F32), (2 * D_MODEL, F32)]
    if per_token:
        kernel, args = _inproj_sample_kernel, (x, mod, norm_g, w_in_bf)
        outs = [(D_ATTN, F32)] * 3 + tail
        out_shape = [jax.ShapeDtypeStruct((n, w), dt) for w, dt in outs]
        out_specs = [pl.BlockSpec((tm, w), row) for w, _ in outs]
    else:
        assert tm == MOBA_BLOCK
        kernel, args = _inproj_prompt_kernel, (x, mod, norm_g, w_in_bf, w_kvt_bf)
        in_specs.append(_const_spec((2 * D_ATTN, D_MODEL)))
        batch = n // (tm * tiles_per_seq)
        seq = tm * tiles_per_seq
        tok = lambda w, dt: (jax.ShapeDtypeStruct((n, w), dt), pl.BlockSpec((tm, w), row))
        tr = lambda dt: (jax.ShapeDtypeStruct((batch, D_ATTN, seq), dt),
                         pl.BlockSpec((1, D_ATTN, tm), lambda i: (i // tiles_per_seq, 0, i % tiles_per_seq)))
        km = (jax.ShapeDtypeStruct((n // tm, 1, D_ATTN), F32), pl.BlockSpec((1, 1, D_ATTN), lambda i: (i, 0, 0)))
        pairs = [tok(D_ATTN, F32), tok(D_ATTN, BF16), tr(F32), tr(F32), tr(BF16)] + [tok(*t) for t in tail] + [km]
        out_shape, out_specs = [p[0] for p in pairs], [p[1] for p in pairs]
    return pl.pallas_call(
        kernel,
        grid=(n // tm,),
        in_specs=in_specs,
        out_specs=out_specs,
        out_shape=out_shape,
        compiler_params=_params(1),
        name="inproj",
    )(*args)


def _select_top(g, idx, n_pick, n_idx, axis):
    sel = jnp.zeros(g.shape, F32)
    for _ in range(n_pick):
        mx = jnp.max(g, axis=axis, keepdims=True)
        first = jnp.min(jnp.where(g == mx, idx, float(n_idx)), axis=axis, keepdims=True)
        hit = idx == first
        sel = jnp.where(hit & (mx > NEG_INF), 1.0, sel)
        g = jnp.where(hit, NEG_INF, g)
    return sel


def _attn_kernel(slopes_ref, q_ref, k_ref, vt_ref, km_ref, o_ref, bias_ref, sel_ref):
    pair, own = pl.program_id(1), pl.program_id(2)
    blk = MOBA_BLOCK
    n_blk = km_ref.shape[1]
    key = lax.broadcasted_iota(I32, (blk, blk), 0)
    qry = lax.broadcasted_iota(I32, (blk, blk), 1)

    @pl.when(own == 0)
    def _():
        rel = (qry - key).astype(F32)
        for hh in range(HEADS_PER_TILE):
            bias_ref[hh] = -slopes_ref[pair * HEADS_PER_TILE + hh] * rel

    q = q_ref[...]
    lane = lax.broadcasted_iota(I32, (1, LANES), 1)
    km = km_ref[0]
    bidx = lax.broadcasted_iota(I32, (n_blk, blk), 0)
    bidx_f = bidx.astype(F32)
    own_off = pl.multiple_of(own * blk, blk)
    k_own = k_ref[pl.ds(own_off, blk), :]
    vt_own = vt_ref[0, :, pl.ds(own_off, blk)]

    qs, ms, ls, pvs = [], [], [], []
    for hh in range(HEADS_PER_TILE):
        qh = jnp.where((lane // HEAD_DIM) == hh, q, 0.0)
        gate = lax.dot_general(km, qh, NT_DIMS, precision=lax.Precision.HIGHEST,
                               preferred_element_type=F32)
        gate = jnp.where(bidx < own, gate, NEG_INF)
        sel_ref[hh] = _select_top(gate, bidx_f, MOBA_TOPK, n_blk, 0)
        qb = (qh * (HEAD_DIM ** -0.5)).astype(BF16)
        qs.append(qb)
        s = lax.dot_general(k_own, qb, NT_DIMS, preferred_element_type=F32) + bias_ref[hh]
        s = jnp.where(key <= qry, s, NEG_INF)
        m = jnp.max(s, axis=0, keepdims=True)
        p = jnp.exp(s - m)
        ms.append(m)
        ls.append(jnp.sum(p, axis=0, keepdims=True))
        pvs.append(jnp.dot(vt_own[hh * HEAD_DIM:(hh + 1) * HEAD_DIM, :], p.astype(BF16),
                           preferred_element_type=F32))

    def body(it, carry):
        n0 = it * KV_GROUP
        off = pl.multiple_of(n0 * blk, KV_GROUP * blk)
        kb = k_ref[pl.ds(off, KV_GROUP * blk), :]
        vtb = vt_ref[0, :, pl.ds(off, KV_GROUP * blk)]
        scores = [lax.dot_general(kb, qs[hh], NT_DIMS, preferred_element_type=F32)
                  for hh in range(HEADS_PER_TILE)]
        new = []
        for hh in range(HEADS_PER_TILE):
            m, l, acc = carry[3 * hh:3 * hh + 3]
            slope = slopes_ref[pair * HEADS_PER_TILE + hh]
            bias = bias_ref[hh]
            subs, m_new = [], m
            for j in range(KV_GROUP):
                s = scores[hh][j * blk:(j + 1) * blk] + bias
                c = -slope * ((own - n0 - j) * blk).astype(F32)
                picked = sel_ref[hh, pl.ds(n0 + j, 1), :] > 0.0
                m_new = jnp.maximum(m_new, jnp.where(picked, jnp.max(s, axis=0, keepdims=True) + c, NEG_INF))
                subs.append((s, c, picked))
            alpha = jnp.exp(m - m_new)
            l = alpha * l
            probs = []
            for s, c, picked in subs:
                p = jnp.exp(s - jnp.where(picked, m_new - c, float("inf")))
                l = l + jnp.sum(p, axis=0, keepdims=True)
                probs.append(p.astype(BF16))
            pv = jnp.dot(vtb[hh * HEAD_DIM:(hh + 1) * HEAD_DIM, :], jnp.concatenate(probs, axis=0),
                         preferred_element_type=F32)
            new += [m_new, l, acc * alpha + pv]
        return tuple(new)

    init = (ms[0], ls[0], pvs[0], ms[1], ls[1], pvs[1])
    n_trips = (own + KV_GROUP - 1) // KV_GROUP
    _, l0, acc0, _, l1, acc1 = lax.fori_loop(0, n_trips, body, init)
    out_t = jnp.concatenate([acc0 / l0, acc1 / l1], axis=0)
    o_ref[...] = out_t.T.astype(o_ref.dtype)


def _attend_prompt(q, k_bf, vt_bf, kmean, slopes, batch, seq):
    n_qb = seq // MOBA_BLOCK
    assert n_qb % KV_GROUP == 0
    tile = lambda b, p, i: (b * n_qb + i, p)
    return pl.pallas_call(
        _attn_kernel,
        grid=(batch, N_HEADS // HEADS_PER_TILE, n_qb),
        in_specs=[pl.BlockSpec(memory_space=pltpu.SMEM),
                  pl.BlockSpec((MOBA_BLOCK, LANES), tile),
                  pl.BlockSpec((seq, LANES), lambda b, p, i: (b, p)),
                  pl.BlockSpec((1, LANES, seq), lambda b, p, i: (b, p, 0)),
                  pl.BlockSpec((1, n_qb, LANES), lambda b, p, i: (b, 0, p))],
        out_specs=pl.BlockSpec((MOBA_BLOCK, LANES), tile),
        out_shape=jax.ShapeDtypeStruct((batch * seq, D_ATTN), BF16),
        scratch_shapes=[pltpu.VMEM((HEADS_PER_TILE, MOBA_BLOCK, MOBA_BLOCK), F32),
                        pltpu.VMEM((HEADS_PER_TILE, n_qb, MOBA_BLOCK), F32)],
        compiler_params=_params(3),
        name="attn_prompt",
    )(slopes, q, k_bf, vt_bf, kmean.reshape(batch, n_qb, D_ATTN))


def _decode_attn_kernel(pt_ref, qt_ref, knt_ref, vnt_ref, slope_ref, *refs, n_pages):
    k_pages, v_pages, o_ref = refs[:n_pages], refs[n_pages:2 * n_pages], refs[2 * n_pages]
    pages_per_blk = MOBA_BLOCK // PAGE_SIZE
    n_blk = n_pages // pages_per_blk
    past_len = n_pages * PAGE_SIZE
    req = pl.program_id(0)
    mine = lax.broadcasted_iota(I32, (D_ATTN, LANES), 1) == req

    @pl.when(req == 0)
    def _():
        o_ref[...] = jnp.zeros_like(o_ref)

    def column(ref):
        return jnp.sum(jnp.where(mine, ref[...], 0.0), axis=1, keepdims=True)

    def head_sums(x):
        head = lax.broadcasted_iota(I32, (N_HEADS, x.shape[1]), 0)
        out = jnp.zeros((N_HEADS, x.shape[1]), F32)
        for h in range(N_HEADS):
            part = jnp.sum(x[h * HEAD_DIM:(h + 1) * HEAD_DIM], axis=0, keepdims=True)
            out = jnp.where(head == h, part, out)
        return out

    def head_rows(x):
        return jnp.concatenate([jnp.broadcast_to(x[h:h + 1], (HEAD_DIM, x.shape[1]))
                                for h in range(N_HEADS)], axis=0)

    q, k_new, v_new = column(qt_ref), column(knt_ref), column(vnt_ref)
    raw = [head_sums(k_pages[p][0] * q) for p in range(n_pages)]

    lane = lax.broadcasted_iota(I32, (N_HEADS, LANES), 1)
    gate = jnp.full((N_HEADS, LANES), NEG_INF, F32)
    for n in range(n_blk):
        tot = raw[n * pages_per_blk]
        for j in range(1, pages_per_blk):
            tot = tot + raw[n * pages_per_blk + j]
        gate = jnp.where(lane == n, jnp.sum(tot, axis=1, keepdims=True) * (1.0 / MOBA_BLOCK), gate)
    sel = _select_top(gate, lane.astype(F32), MOBA_TOPK, LANES, 1)

    slope = slope_ref[...]
    scale = HEAD_DIM ** -0.5
    scores = []
    for p in range(n_pages):
        dist = (past_len - p * PAGE_SIZE - lane).astype(F32)
        n = p // pages_per_blk
        scores.append(jnp.where(sel[:, n:n + 1] > 0.0, raw[p] * scale - slope * dist, NEG_INF))
    s_new = head_sums(q * k_new) * scale

    m = s_new
    for s in scores:
        m = jnp.maximum(m, jnp.max(s, axis=1, keepdims=True))
    p_new = jnp.exp(s_new - m)
    l = p_new
    acc = jnp.zeros((D_ATTN, PAGE_SIZE), F32)
    for p in range(n_pages):
        prob = jnp.exp(scores[p] - m)
        l = l + jnp.sum(prob, axis=1, keepdims=True)
        acc = acc + v_pages[p][0] * head_rows(prob)
    out = (jnp.sum(acc, axis=1, keepdims=True) + head_rows(p_new) * v_new) / head_rows(l)
    o_ref[...] = jnp.where(mine, out, o_ref[...])


def _attend_sample(q, k_new, v_new, cache_k, cache_v, page_table, slopes_rows):
    n_req, n_pages = page_table.shape
    assert n_req == LANES
    n_phys = cache_k.shape[0]
    as_tiles = lambda c: c.transpose(0, 2, 3, 1).reshape(n_phys, D_ATTN, PAGE_SIZE)
    ck, cv = as_tiles(cache_k), as_tiles(cache_v)
    cols = pl.BlockSpec((D_ATTN, n_req), lambda r, pt: (0, 0))

    def page_spec(p):
        return pl.BlockSpec((1, D_ATTN, PAGE_SIZE), lambda r, pt: (pt[r * n_pages + p], 0, 0))

    pages = [page_spec(p) for p in range(n_pages)]
    grid_spec = pltpu.PrefetchScalarGridSpec(
        num_scalar_prefetch=1,
        grid=(n_req,),
        in_specs=[cols, cols, cols, pl.BlockSpec((N_HEADS, LANES), lambda r, pt: (0, 0))] + pages + pages,
        out_specs=cols,
    )
    out_t = pl.pallas_call(
        functools.partial(_decode_attn_kernel, n_pages=n_pages),
        grid_spec=grid_spec,
        out_shape=jax.ShapeDtypeStruct((D_ATTN, n_req), F32),
        compiler_params=_params(1),
        name="attn_sample",
    )(page_table.reshape(-1), q.T, k_new.T, v_new.T, slopes_rows,
      *([ck] * n_pages), *([cv] * n_pages))
    return out_t.T


def _branch_merge(y_conv, x, attn_bf, sg, mod, w, xo_ref, h2_ref, eidx_ref, gate_ref):
    (b_dw, ln_g, ln_b, w_ao, w_pw2, w_out, n2g, w_router, b_router) = w
    y = y_conv + b_dw[...]
    mu = jnp.mean(y, axis=-1, keepdims=True)
    yc = y - mu
    var = jnp.mean(yc * yc, axis=-1, keepdims=True)
    y = (yc * lax.rsqrt(var + EPS)) * ln_g[...] + ln_b[...]
    y = y * _sigmoid(y)
    c_out = jnp.dot(y.astype(BF16), w_pw2[...], preferred_element_type=F32)
    a_out = jnp.dot(attn_bf, w_ao[...], preferred_element_type=F32)
    merged = sg[:, 0:D_MODEL] * a_out + sg[:, D_MODEL:2 * D_MODEL] * c_out
    g1 = mod[:, 2 * D_MODEL:3 * D_MODEL]
    xo = x + g1 * jnp.dot(merged.astype(BF16), w_out[...], preferred_element_type=F32)
    xo_ref[...] = xo
    sh2, sc2 = mod[:, 3 * D_MODEL:4 * D_MODEL], mod[:, 4 * D_MODEL:5 * D_MODEL]
    h2 = _rms(xo, n2g[...]) * (1.0 + sc2) + sh2
    h2_ref[...] = h2

    logits = jnp.dot(h2, w_router[...], precision=lax.Precision.HIGHEST,
                     preferred_element_type=F32) + b_router[...]
    tm = logits.shape[0]
    lane = lax.broadcasted_iota(I32, (tm, LANES), 1)
    logits = jnp.where(lane < N_EXPERTS, logits, NEG_INF)
    eidx = lane.astype(F32)
    vals, e_out = [], jnp.zeros((tm, LANES), F32)
    for kk in range(TOP_K):
        mx = jnp.max(logits, axis=1, keepdims=True)
        first = jnp.min(jnp.where(logits == mx, eidx, float(N_EXPERTS)), axis=1, keepdims=True)
        logits = jnp.where(eidx == first, NEG_INF, logits)
        vals.append(mx)
        e_out = jnp.where(lane == kk, first, e_out)
    ex = [jnp.exp(v - vals[0]) for v in vals]
    den = ex[0] + ex[1] + ex[2] + ex[3]
    g_out = jnp.zeros((tm, LANES), F32)
    for kk in range(TOP_K):
        g_out = jnp.where(lane == kk, ex[kk] / den, g_out)
    eidx_ref[...] = e_out.astype(I32)
    gate_ref[...] = g_out


def _merge_prompt_kernel(x_ref, attn_ref, u_ref, halo_ref, sg_ref, mod_ref, wdw_ref, *refs, tiles_per_seq):
    w, (xo_ref, h2_ref, eidx_ref, gate_ref, full_ref) = refs[:9], refs[9:]
    tm = u_ref.shape[0]
    first = (pl.program_id(0) % tiles_per_seq) == 0
    full_ref[0:HALO, :] = jnp.where(first, 0.0, halo_ref[...])
    full_ref[HALO:HALO + tm, :] = u_ref[...]
    base = HALO - (CONV_WIDTH - 1)
    y = jnp.zeros((tm, D_CONV), F32)
    for j in range(CONV_WIDTH):
        y = y + wdw_ref[j:j + 1, :] * full_ref[base + j:base + j + tm, :]
    _branch_merge(y, x_ref[...], attn_ref[...], sg_ref[...], mod_ref[0], w,
                  xo_ref, h2_ref, eidx_ref, gate_ref)


def _merge_sample_kernel(x_ref, attn_ref, u_ref, state_ref, sg_ref, mod_ref, wdw_ref, *refs):
    w, (xo_ref, h2_ref, eidx_ref, gate_ref) = refs[:9], refs[9:]
    y = wdw_ref[CONV_WIDTH - 1:CONV_WIDTH, :] * u_ref[...]
    for j in range(CONV_WIDTH - 1):
        y = y + wdw_ref[j:j + 1, :] * state_ref[j]
    _branch_merge(y, x_ref[...], attn_ref[...].astype(BF16), sg_ref[...], mod_ref[...], w,
                  xo_ref, h2_ref, eidx_ref, gate_ref)


def _merge(x, attn, u, hist, sg, mod, weights, *, per_token, tm, tiles_per_seq):
    n = x.shape[0]
    row = lambda i: (i, 0)
    if per_token:
        kernel = _merge_sample_kernel
        hist_spec = _const_spec(hist.shape)
        scratch = []
    else:
        kernel = functools.partial(_merge_prompt_kernel, tiles_per_seq=tiles_per_seq)
        hist_spec = pl.BlockSpec((HALO, D_CONV), lambda i: (jnp.maximum(i * (tm // HALO) - 1, 0), 0))
        scratch = [pltpu.VMEM((HALO + tm, D_CONV), F32)]
    outs = [(D_MODEL, F32), (D_MODEL, F32), (LANES, I32), (LANES, F32)]
    return pl.pallas_call(
        kernel,
        grid=(n // tm,),
        in_specs=[pl.BlockSpec((tm, D_MODEL), row), pl.BlockSpec((tm, D_ATTN), row),
                  pl.BlockSpec((tm, D_CONV), row), hist_spec, pl.BlockSpec((tm, 2 * D_MODEL), row),
                  _mod_spec(per_token, tm, tiles_per_seq)] + [_const_spec(a.shape) for a in weights],
        out_specs=[pl.BlockSpec((tm, w), row) for w, _ in outs],
        out_shape=[jax.ShapeDtypeStruct((n, w), dt) for w, dt in outs],
        scratch_shapes=scratch,
        compiler_params=_params(1),
        name="merge",
    )(x, attn, u, hist, sg, mod, *weights)


def _onehots(e, lane):
    return [(e[:, kk:kk + 1] == lane) for kk in range(TOP_K)]


def _rank_kernel(e_ref, rank_ref, count_ref, carry_ref):
    tm = e_ref.shape[0]

    @pl.when(pl.program_id(0) == 0)
    def _():
        carry_ref[...] = jnp.zeros_like(carry_ref)

    lane = lax.broadcasted_iota(I32, (tm, LANES), 1)
    hots = _onehots(e_ref[...], lane)
    hot = jnp.zeros((tm, LANES), F32)
    for h in hots:
        hot = hot + h.astype(F32)
    r = lax.broadcasted_iota(I32, (tm, tm), 0)
    c = lax.broadcasted_iota(I32, (tm, tm), 1)
    earlier = (c < r).astype(BF16)
    before = carry_ref[...] + jnp.dot(earlier, hot.astype(BF16), preferred_element_type=F32)
    rank = jnp.zeros((tm, LANES), F32)
    for kk, h in enumerate(hots):
        rank = jnp.where(lane == kk, jnp.sum(jnp.where(h, before, 0.0), axis=1, keepdims=True), rank)
    rank_ref[...] = rank.astype(I32)
    carry_ref[...] = carry_ref[...] + jnp.sum(hot, axis=0, keepdims=True)
    count_ref[...] = jnp.broadcast_to(carry_ref[...], count_ref.shape)


def _slot_kernel(e_ref, rank_ref, count_ref, dest_ref, blk_ref, info_ref, *, n_blocks_pad):
    tm = e_ref.shape[0]
    counts = count_ref[0:1, :].astype(I32)
    padded = ((counts + (EXPERT_BLOCK - 1)) // EXPERT_BLOCK) * EXPERT_BLOCK
    r = lax.broadcasted_iota(I32, (LANES, LANES), 0)
    c = lax.broadcasted_iota(I32, (LANES, LANES), 1)
    pstart = jnp.dot(jnp.broadcast_to(padded.astype(F32), (8, LANES)), (r < c).astype(F32),
                     precision=lax.Precision.HIGHEST, preferred_element_type=F32)[0:1]
    pend = pstart + padded.astype(F32)
    lane = lax.broadcasted_iota(I32, (tm, LANES), 1)
    e = e_ref[...]
    dest = rank_ref[...]
    for kk, h in enumerate(_onehots(e, lane)):
        off = jnp.sum(jnp.where(h, pstart, 0.0), axis=1, keepdims=True).astype(I32)
        dest = dest + jnp.where(lane == kk, off, 0)
    dest_ref[...] = dest

    lane1 = lax.broadcasted_iota(I32, (1, LANES), 1)
    used = jnp.max(pend, axis=1, keepdims=True)
    blk_start = (lax.broadcasted_iota(I32, (n_blocks_pad, 1), 0) * EXPERT_BLOCK).astype(F32)
    blk_start = jnp.minimum(blk_start, used - EXPERT_BLOCK)
    done = jnp.where((lane1 < N_EXPERTS) & (pend <= blk_start), 1.0, 0.0)
    blk_exp = jnp.minimum(jnp.sum(done, axis=1, keepdims=True), N_EXPERTS - 1.0)
    blk_ref[...] = jnp.broadcast_to(blk_exp, blk_ref.shape).astype(I32)
    row8 = lax.broadcasted_iota(I32, (8, LANES), 0)
    info = jnp.where(row8 == 0, pstart, jnp.where(row8 == 1, pend, jnp.where(row8 == 2, used, 0.0)))
    info_ref[...] = info.astype(I32)


def _route(eidx, tm, n_blocks):
    n = eidx.shape[0]
    row = lambda i: (i, 0)
    fixed = lambda i: (0, 0)
    rank, counts = pl.pallas_call(
        _rank_kernel,
        grid=(n // tm,),
        in_specs=[pl.BlockSpec((tm, LANES), row)],
        out_specs=[pl.BlockSpec((tm, LANES), row), pl.BlockSpec((8, LANES), fixed)],
        out_shape=[jax.ShapeDtypeStruct((n, LANES), I32), jax.ShapeDtypeStruct((8, LANES), F32)],
        scratch_shapes=[pltpu.VMEM((1, LANES), F32)],
        compiler_params=_params(1),
        name="moe_rank",
    )(eidx)
    n_blocks_pad = -(-n_blocks // 8) * 8
    dest, blk, info = pl.pallas_call(
        functools.partial(_slot_kernel, n_blocks_pad=n_blocks_pad),
        grid=(n // tm,),
        in_specs=[pl.BlockSpec((tm, LANES), row), pl.BlockSpec((tm, LANES), row),
                  pl.BlockSpec((8, LANES), fixed)],
        out_specs=[pl.BlockSpec((tm, LANES), row), pl.BlockSpec((n_blocks_pad, LANES), fixed),
                   pl.BlockSpec((8, LANES), fixed)],
        out_shape=[jax.ShapeDtypeStruct((n, LANES), I32), jax.ShapeDtypeStruct((n_blocks_pad, LANES), I32),
                   jax.ShapeDtypeStruct((8, LANES), I32)],
        compiler_params=_params(1),
        name="moe_slot",
    )(eidx, rank, counts)
    dest_flat = dest[:, :TOP_K].reshape(-1)
    blk_exp = blk[:n_blocks, 0]
    pend = info[1, :N_EXPERTS]
    n_used = jnp.right_shift(info[2, 0:1], EXPERT_BLOCK.bit_length() - 1)
    return dest_flat, blk_exp, pend, n_used


def _row_copy(src_ref, src_row, dst_ref, dst_row, sem):
    return pltpu.make_async_copy(src_ref.at[pl.ds(src_row, 1)], dst_ref.at[pl.ds(dst_row, 1)], sem)


def _dispatch_kernel(dest_ref, pend_ref, h_ref, xs_ref, zero_ref, sem):
    tm = h_ref.shape[0]

    @pl.when(pl.program_id(0) == 0)
    def _():
        zero_ref[...] = jnp.zeros_like(zero_ref)
        n_blocks = xs_ref.shape[0] // EXPERT_BLOCK
        first_unused = pend_ref[N_EXPERTS - 1] // EXPERT_BLOCK

        def fill(start):
            return pltpu.make_async_copy(zero_ref, xs_ref.at[pl.ds(pl.multiple_of(start, EXPERT_BLOCK), EXPERT_BLOCK)], sem)

        def last_block(e):
            return jnp.maximum(pend_ref[e] - EXPERT_BLOCK, 0)

        @pl.loop(0, N_EXPERTS)
        def _(e):
            fill(last_block(e)).start()

        @pl.loop(first_unused, n_blocks)
        def _(j):
            fill(j * EXPERT_BLOCK).start()

        @pl.loop(0, N_EXPERTS)
        def _(e):
            fill(last_block(e)).wait()

        @pl.loop(first_unused, n_blocks)
        def _(j):
            fill(j * EXPERT_BLOCK).wait()

    @pl.loop(0, tm)
    def _(t):
        for kk in range(TOP_K):
            _row_copy(h_ref, t, xs_ref, dest_ref[t * TOP_K + kk], sem).start()

    @pl.loop(0, tm)
    def _(t):
        for kk in range(TOP_K):
            _row_copy(h_ref, t, xs_ref, dest_ref[t * TOP_K + kk], sem).wait()


def _dispatch(h2, dest_flat, pend, tm, cap):
    n = h2.shape[0]
    return pl.pallas_call(
        _dispatch_kernel,
        grid=(n // tm,),
        in_specs=[pl.BlockSpec((tm * TOP_K,), lambda i: (i,), memory_space=pltpu.SMEM),
                  pl.BlockSpec(memory_space=pltpu.SMEM),
                  pl.BlockSpec((tm, D_MODEL), lambda i: (i, 0))],
        out_specs=pl.BlockSpec(memory_space=pl.ANY),
        out_shape=jax.ShapeDtypeStruct((cap, D_MODEL), F32),
        scratch_shapes=[pltpu.VMEM((EXPERT_BLOCK, D_MODEL), F32), pltpu.SemaphoreType.DMA(())],
        compiler_params=_params(1, has_side_effects=True),
        name="moe_dispatch",
    )(dest_flat, pend, h2)


def _expert_kernel(blk_ref, used_ref, x_ref, wg_ref, wu_ref, wd_ref, y_ref, wg_bf, wu_bf, wd_bf):
    j = pl.program_id(0)
    active = j < used_ref[0]
    changed = (j == 0) | (blk_ref[j] != blk_ref[jnp.maximum(j - 1, 0)])

    @pl.when(active & changed)
    def _():
        wg_bf[...] = wg_ref[0].astype(BF16)
        wu_bf[...] = wu_ref[0].astype(BF16)
        wd_bf[...] = wd_ref[0].astype(BF16)

    @pl.when(active)
    def _():
        x = x_ref[...].astype(BF16)
        gt = jnp.minimum(jnp.dot(x, wg_bf[...], preferred_element_type=F32), SWIGLU_LIMIT)
        up = jnp.clip(jnp.dot(x, wu_bf[...], preferred_element_type=F32), -SWIGLU_LIMIT, SWIGLU_LIMIT)
        act = gt * _sigmoid(SWIGLU_ALPHA * gt) * (up + 1.0)
        y_ref[...] = jnp.dot(act.astype(BF16), wd_bf[...], preferred_element_type=F32)

    @pl.when(jnp.logical_not(active))
    def _():
        y_ref[...] = jnp.zeros_like(y_ref)


def _experts(xs, blk_exp, n_used, w_gate, w_up, w_down):
    n_blocks = blk_exp.shape[0]
    d_ff = w_gate.shape[2]
    xrow = lambda j, blk, used: (jnp.minimum(j, used[0] - 1), 0)
    wsel = lambda j, blk, used: (blk[j], 0, 0)
    grid_spec = pltpu.PrefetchScalarGridSpec(
        num_scalar_prefetch=2,
        grid=(n_blocks,),
        in_specs=[pl.BlockSpec((EXPERT_BLOCK, D_MODEL), xrow),
                  pl.BlockSpec((1, D_MODEL, d_ff), wsel),
                  pl.BlockSpec((1, D_MODEL, d_ff), wsel),
                  pl.BlockSpec((1, d_ff, D_MODEL), wsel)],
        out_specs=pl.BlockSpec((EXPERT_BLOCK, D_MODEL), lambda j, blk, used: (j, 0)),
        scratch_shapes=[pltpu.VMEM((D_MODEL, d_ff), BF16), pltpu.VMEM((D_MODEL, d_ff), BF16),
                        pltpu.VMEM((d_ff, D_MODEL), BF16)],
    )
    return pl.pallas_call(
        _expert_kernel,
        grid_spec=grid_spec,
        out_shape=jax.ShapeDtypeStruct(xs.shape, F32),
        compiler_params=_params(1),
        name="moe_experts",
    )(blk_exp, n_used, xs, w_gate, w_up, w_down)


def _combine_kernel(dest_ref, y_ref, xo_ref, gate_ref, mod_ref, fg_ref, o_ref, buf_ref, sem, *, per_token):
    tm = xo_ref.shape[0]

    @pl.loop(0, tm)
    def _(t):
        for kk in range(TOP_K):
            _row_copy(y_ref, dest_ref[t * TOP_K + kk], buf_ref.at[kk], t, sem).start()

    @pl.loop(0, tm)
    def _(t):
        for kk in range(TOP_K):
            _row_copy(y_ref, dest_ref[t * TOP_K + kk], buf_ref.at[kk], t, sem).wait()

    gates = gate_ref[...]
    moe = gates[:, 0:1] * buf_ref[0]
    for kk in range(1, TOP_K):
        moe = moe + gates[:, kk:kk + 1] * buf_ref[kk]
    g2 = _load_mod(mod_ref, per_token)[:, 5 * D_MODEL:6 * D_MODEL]
    o_ref[...] = _rms(xo_ref[...] + g2 * moe, fg_ref[...])


def _combine(y, dest_flat, xo, gates, mod, final_g, *, per_token, tm, tiles_per_seq):
    n = xo.shape[0]
    row = lambda i: (i, 0)
    return pl.pallas_call(
        functools.partial(_combine_kernel, per_token=per_token),
        grid=(n // tm,),
        in_specs=[pl.BlockSpec((tm * TOP_K,), lambda i: (i,), memory_space=pltpu.SMEM),
                  pl.BlockSpec(memory_space=pl.ANY),
                  pl.BlockSpec((tm, D_MODEL), row), pl.BlockSpec((tm, LANES), row),
                  _mod_spec(per_token, tm, tiles_per_seq), _const_spec((1, D_MODEL))],
        out_specs=pl.BlockSpec((tm, D_MODEL), row),
        out_shape=jax.ShapeDtypeStruct((n, D_MODEL), F32),
        scratch_shapes=[pltpu.VMEM((TOP_K, tm, D_MODEL), F32), pltpu.SemaphoreType.DMA(())],
        compiler_params=_params(1),
        name="moe_combine",
    )(dest_flat, y, xo, gates, mod, final_g)


def _moe(h2, eidx, gates, xo, mod, final_g, w_gate, w_up, w_down, *, per_token, tm, tiles_per_seq):
    n = h2.shape[0]
    n_blocks = -(-n * TOP_K // EXPERT_BLOCK) + N_EXPERTS
    dest_flat, blk_exp, pend, n_used = _route(eidx, tm, n_blocks)
    xs = _dispatch(h2, dest_flat, pend, tm, n_blocks * EXPERT_BLOCK)
    y = _experts(xs, blk_exp, n_used, w_gate, w_up, w_down)
    return _combine(y, dest_flat, xo, gates, mod, final_g,
                    per_token=per_token, tm=tm, tiles_per_seq=tiles_per_seq)


def kernel(x_prompt, x_sample, c_prompt, c_sample, cache_k, cache_v, state_conv, page_table, norm1_g, norm2_g,
           w_ada, b_ada, w_in, w_attn_out, w_dw, b_dw, ln_g, ln_b, w_pw2, w_out, w_router, b_router, w_gate,
           w_up, w_down, final_g):
    depth = w_in.shape[0]
    assert depth == 1, "single-layer trunk"
    batch, seq, _ = x_prompt.shape
    n_req = x_sample.shape[0]
    assert x_sample.shape[1] == 1 and seq % MOBA_BLOCK == 0 and n_req % 8 == 0
    layer = 0
    row2 = lambda a: a.reshape(1, -1)

    pad = (-batch) % 8
    c_all = jnp.concatenate([c_prompt, jnp.zeros((pad, D_MODEL), F32), c_sample], axis=0)
    ada = _ada(c_all, w_ada[layer], b_ada[layer])
    mod_p = ada[:batch].reshape(batch, 1, 6 * D_MODEL)
    mod_s = ada[batch + pad:]

    w_in_bf = w_in[layer].astype(BF16)
    merge_w = (row2(b_dw[layer]), row2(ln_g[layer]), row2(ln_b[layer]), w_attn_out[layer].astype(BF16),
               w_pw2[layer].astype(BF16), w_out[layer].astype(BF16), row2(norm2_g[layer]),
               jnp.pad(w_router[layer], ((0, 0), (0, LANES - N_EXPERTS))),
               jnp.pad(row2(b_router[layer]), ((0, 0), (0, LANES - N_EXPERTS))))
    w_dw_pad = jnp.pad(w_dw[layer], ((0, HALO - CONV_WIDTH), (0, 0)))
    slopes = jnp.exp2(-8.0 * jnp.arange(1, N_HEADS + 1, dtype=F32) / N_HEADS)
    slopes_rows = jnp.broadcast_to(slopes[:, None], (N_HEADS, LANES))
    fg = row2(final_g)
    experts = (w_gate[layer], w_up[layer], w_down[layer])

    tm = MOBA_BLOCK
    tps = seq // tm
    xp = x_prompt.reshape(batch * seq, D_MODEL)
    w_kvt_bf = w_in_bf[:, D_ATTN:3 * D_ATTN].T
    q, k_bf, kt, vt, vt_bf, u, sg, kmean = _inproj(xp, mod_p, row2(norm1_g[layer]), w_in_bf, w_kvt_bf,
                                                   per_token=False, tm=tm, tiles_per_seq=tps)
    attn = _attend_prompt(q, k_bf, vt_bf, kmean, slopes, batch, seq)
    xo, h2, eidx, gates = _merge(xp, attn, u, u, sg, mod_p, (w_dw_pad,) + merge_w,
                                 per_token=False, tm=tm, tiles_per_seq=tps)
    y_prompt = _moe(h2, eidx, gates, xo, mod_p, fg, *experts, per_token=False, tm=tm, tiles_per_seq=tps)
    to_cache = lambda t: t.reshape(batch, N_HEADS, HEAD_DIM, seq).transpose(0, 3, 1, 2)[None]
    k_prompt, v_prompt = to_cache(kt), to_cache(vt)
    conv_prompt = u.reshape(batch, seq, D_CONV)[None, :, seq - (CONV_WIDTH - 1):]

    xs = x_sample.reshape(n_req, D_MODEL)
    qs, ks, vs, us, sgs = _inproj(xs, mod_s, row2(norm1_g[layer]), w_in_bf,
                                  per_token=True, tm=n_req, tiles_per_seq=1)
    attn_s = _attend_sample(qs, ks, vs, cache_k[layer], cache_v[layer], page_table, slopes_rows)
    state = state_conv[layer]
    xo_s, h2_s, eidx_s, gates_s = _merge(xs, attn_s, us, state.transpose(1, 0, 2), sgs, mod_s,
                                         (w_dw_pad,) + merge_w, per_token=True, tm=n_req, tiles_per_seq=1)
    y_sample = _moe(h2_s, eidx_s, gates_s, xo_s, mod_s, fg, *experts, per_token=True, tm=n_req, tiles_per_seq=1)
    k_sample = ks.reshape(1, n_req, 1, N_HEADS, HEAD_DIM)
    v_sample = vs.reshape(1, n_req, 1, N_HEADS, HEAD_DIM)
    conv_sample = jnp.concatenate([state[:, 1:], us[:, None, :]], axis=1)[None]

    return (y_prompt.reshape(batch, seq, D_MODEL), y_sample.reshape(n_req, 1, D_MODEL),
            k_prompt, v_prompt, conv_prompt, k_sample, v_sample, conv_sample)
```

```python
import functools

import jax
import jax.numpy as jnp
from jax import lax
from jax.experimental import pallas as pl
from jax.experimental.pallas import tpu as pltpu

F32 = jnp.float32
BF16 = jnp.bfloat16
I32 = jnp.int32

D_MODEL = 1024
N_HEADS = 8
HEAD_DIM = 64
D_ATTN = N_HEADS * HEAD_DIM
D_CONV = 512
CONV_WIDTH = 31
N_EXPERTS = 32
TOP_K = 4
MOBA_BLOCK = 256
MOBA_TOPK = 3
PAGE_SIZE = 128
SWIGLU_ALPHA = 1.702
SWIGLU_LIMIT = 7.0
EPS = 1e-6
D_IN = 3 * D_ATTN + 2 * D_CONV + 2 * D_MODEL

LANES = 128
HALO = 32
EXPERT_BLOCK = 256
HEADS_PER_TILE = LANES // HEAD_DIM
KV_GROUP = 2
VMEM_LIMIT = 56 * 1024 * 1024

NT_DIMS = (((1,), (1,)), ((), ()))
NEG_INF = float("-inf")
LOG2E = 1.4426950408889634


def _params(n_axes, **kw):
    return pltpu.CompilerParams(dimension_semantics=("arbitrary",) * n_axes,
                                vmem_limit_bytes=VMEM_LIMIT, **kw)


def _const_spec(shape):
    nd = len(shape)
    return pl.BlockSpec(shape, lambda *_: (0,) * nd, pipeline_mode=pl.Buffered(1))


def _rms(x, g):
    return (x * lax.rsqrt(jnp.mean(x * x, axis=-1, keepdims=True) + EPS)) * g


def _sigmoid(x):
    return 1.0 / (1.0 + jnp.exp(-x))


def _ada_kernel(c_ref, w_ref, b_ref, o_ref):
    o_ref[...] = jnp.dot(c_ref[...].astype(BF16), w_ref[...].astype(BF16),
                         preferred_element_type=F32) + b_ref[...]


def _ada(c, w_ada, b_ada):
    rows = c.shape[0]
    return pl.pallas_call(
        _ada_kernel,
        grid=(6,),
        in_specs=[pl.BlockSpec((rows, D_MODEL), lambda j: (0, 0)),
                  pl.BlockSpec((D_MODEL, D_MODEL), lambda j: (0, j)),
                  pl.BlockSpec((1, D_MODEL), lambda j: (0, j))],
        out_specs=pl.BlockSpec((rows, D_MODEL), lambda j: (0, j)),
        out_shape=jax.ShapeDtypeStruct((rows, 6 * D_MODEL), F32),
        compiler_params=_params(1),
        name="ada",
    )(c, w_ada, b_ada.reshape(1, -1))


def _mod_spec(per_token, tm, tiles_per_seq):
    if per_token:
        return pl.BlockSpec((tm, 6 * D_MODEL), lambda i: (i, 0))
    return pl.BlockSpec((1, 1, 6 * D_MODEL), lambda i: (i // tiles_per_seq, 0, 0))


def _load_mod(mod_ref, per_token):
    return mod_ref[...] if per_token else mod_ref[0]


def _inproj_common(x_ref, mod_ref, g_ref, w_ref, u_ref, sg_ref, per_token):
    mod = _load_mod(mod_ref, per_token)
    sh1, sc1 = mod[:, 0:D_MODEL], mod[:, D_MODEL:2 * D_MODEL]
    h = (_rms(x_ref[...], g_ref[...]) * (1.0 + sc1) + sh1).astype(BF16)

    def proj(lo, hi):
        return jnp.dot(h, w_ref[:, lo:hi], preferred_element_type=F32)

    o = 3 * D_ATTN
    u_ref[...] = proj(o, o + D_CONV) * _sigmoid(proj(o + D_CONV, o + 2 * D_CONV))
    o += 2 * D_CONV
    sg_ref[...] = _sigmoid(proj(o, o + 2 * D_MODEL))
    return h, proj


def _inproj_sample_kernel(x_ref, mod_ref, g_ref, w_ref, q_ref, k_ref, v_ref, u_ref, sg_ref):
    _, proj = _inproj_common(x_ref, mod_ref, g_ref, w_ref, u_ref, sg_ref, True)
    q_ref[...] = proj(0, D_ATTN)
    k_ref[...] = proj(D_ATTN, 2 * D_ATTN)
    v_ref[...] = proj(2 * D_ATTN, 3 * D_ATTN)


def _inproj_prompt_kernel(x_ref, mod_ref, g_ref, w_ref, wt_ref, q_ref, kb_ref, kt_ref, vt_ref, vtb_ref,
                          u_ref, sg_ref, km_ref):
    h, proj = _inproj_common(x_ref, mod_ref, g_ref, w_ref, u_ref, sg_ref, False)
    q_ref[...] = proj(0, D_ATTN)
    k = proj(D_ATTN, 2 * D_ATTN)
    kb_ref[...] = k.astype(BF16)
    km_ref[0] = jnp.mean(k, axis=0, keepdims=True)
    kt_ref[0] = lax.dot_general(wt_ref[0:D_ATTN, :], h, NT_DIMS, preferred_element_type=F32)
    vt = lax.dot_general(wt_ref[D_ATTN:2 * D_ATTN, :], h, NT_DIMS, preferred_element_type=F32)
    vt_ref[0] = vt
    vtb_ref[0] = vt.astype(BF16)


def _inproj(x, mod, norm_g, w_in_bf, w_kvt_bf=None, *, per_token, tm, tiles_per_seq):
    n = x.shape[0]
    row = lambda i: (i, 0)
    in_specs = [pl.BlockSpec((tm, D_MODEL), row), _mod_spec(per_token, tm, tiles_per_seq),
                _const_spec((1, D_MODEL)), _const_spec((D_MODEL, D_IN))]
    tail = [(D_CONV, F32), (2 * D_MODEL, F32)]
    if per_token:
        kernel, args = _inproj_sample_kernel, (x, mod, norm_g, w_in_bf)
        outs = [(D_ATTN, F32)] * 3 + tail
        out_shape = [jax.ShapeDtypeStruct((n, w), dt) for w, dt in outs]
        out_specs = [pl.BlockSpec((tm, w), row) for w, _ in outs]
    else:
        assert tm == MOBA_BLOCK
        kernel, args = _inproj_prompt_kernel, (x, mod, norm_g, w_in_bf, w_kvt_bf)
        in_specs.append(_const_spec((2 * D_ATTN, D_MODEL)))
        batch = n // (tm * tiles_per_seq)
        seq = tm * tiles_per_seq
        tok = lambda w, dt: (jax.ShapeDtypeStruct((n, w), dt), pl.BlockSpec((tm, w), row))
        tr = lambda dt: (jax.ShapeDtypeStruct((batch, D_ATTN, seq), dt),
                         pl.BlockSpec((1, D_ATTN, tm), lambda i: (i // tiles_per_seq, 0, i % tiles_per_seq)))
        km = (jax.ShapeDtypeStruct((n // tm, 1, D_ATTN), F32), pl.BlockSpec((1, 1, D_ATTN), lambda i: (i, 0, 0)))
        pairs = [tok(D_ATTN, F32), tok(D_ATTN, BF16), tr(F32), tr(F32), tr(BF16)] + [tok(*t) for t in tail] + [km]
        out_shape, out_specs = [p[0] for p in pairs], [p[1] for p in pairs]
    return pl.pallas_call(
        kernel,
        grid=(n // tm,),
        in_specs=in_specs,
        out_specs=out_specs,
        out_shape=out_shape,
        compiler_params=_params(1),
        name="inproj",
    )(*args)


def _select_top(g, idx, n_pick, n_idx, axis):
    sel = jnp.zeros(g.shape, F32)
    for _ in range(n_pick):
        mx = jnp.max(g, axis=axis, keepdims=True)
        first = jnp.min(jnp.where(g == mx, idx, float(n_idx)), axis=axis, keepdims=True)
        hit = idx == first
        sel = jnp.where(hit & (mx > NEG_INF), 1.0, sel)
        g = jnp.where(hit, NEG_INF, g)
    return sel


def _attn_kernel(slopes_ref, q_ref, k_ref, vt_ref, km_ref, o_ref, bias_ref, sel_ref, sa_ref, sb_ref):
    pair, own = pl.program_id(1), pl.program_id(2)
    blk = MOBA_BLOCK
    n_blk = km_ref.shape[1]
    key = lax.broadcasted_iota(I32, (blk, blk), 0)
    qry = lax.broadcasted_iota(I32, (blk, blk), 1)

    @pl.when(own == 0)
    def _():
        rel = (qry - key).astype(F32)
        for hh in range(HEADS_PER_TILE):
            bias_ref[hh] = -(slopes_ref[pair * HEADS_PER_TILE + hh] * LOG2E) * rel

    q = q_ref[...]
    lane = lax.broadcasted_iota(I32, (1, LANES), 1)
    km = km_ref[0]
    bidx = lax.broadcasted_iota(I32, (n_blk, blk), 0)
    bidx_f = bidx.astype(F32)
    own_off = pl.multiple_of(own * blk, blk)
    k_own = k_ref[pl.ds(own_off, blk), :]
    vt_own = vt_ref[0, :, pl.ds(own_off, blk)]

    qs, ms, ls, pvs = [], [], [], []
    for hh in range(HEADS_PER_TILE):
        qh = jnp.where((lane // HEAD_DIM) == hh, q, 0.0)
        gate = lax.dot_general(km, qh, NT_DIMS, precision=lax.Precision.HIGHEST,
                               preferred_element_type=F32)
        gate = jnp.where(bidx < own, gate, NEG_INF)
        sel_ref[hh] = _select_top(gate, bidx_f, MOBA_TOPK, n_blk, 0)
        qb = (qh * (HEAD_DIM ** -0.5 * LOG2E)).astype(BF16)
        qs.append(qb)
        s = lax.dot_general(k_own, qb, NT_DIMS, preferred_element_type=F32) + bias_ref[hh]
        s = jnp.where(key <= qry, s, NEG_INF)
        m = jnp.max(s, axis=0, keepdims=True)
        p = jnp.exp2(s - m)
        ms.append(m)
        ls.append(jnp.sum(p, axis=0, keepdims=True))
        pvs.append(jnp.dot(vt_own[hh * HEAD_DIM:(hh + 1) * HEAD_DIM, :], p.astype(BF16),
                           preferred_element_type=F32))

    span = KV_GROUP * blk

    def score_stage(dst_ref, n0):
        n0 = jnp.minimum(n0, n_blk - KV_GROUP)
        kb = k_ref[pl.ds(pl.multiple_of(n0 * blk, span), span), :]
        for hh in range(HEADS_PER_TILE):
            dst_ref[hh] = lax.dot_general(kb, qs[hh], NT_DIMS, preferred_element_type=F32)

    def softmax_stage(src_ref, n0, carry):
        vtb = vt_ref[0, :, pl.ds(pl.multiple_of(n0 * blk, span), span)]
        new = []
        for hh in range(HEADS_PER_TILE):
            m, l, acc = carry[3 * hh:3 * hh + 3]
            slope = slopes_ref[pair * HEADS_PER_TILE + hh] * LOG2E
            bias = bias_ref[hh]
            subs, m_new = [], m
            for j in range(KV_GROUP):
                s = src_ref[hh, j * blk:(j + 1) * blk, :] + bias
                c = -slope * ((own - n0 - j) * blk).astype(F32)
                picked = sel_ref[hh, pl.ds(n0 + j, 1), :] > 0.0
                m_new = jnp.maximum(m_new, jnp.where(picked, jnp.max(s, axis=0, keepdims=True) + c, NEG_INF))
                subs.append((s, c, picked))
            alpha = jnp.exp2(m - m_new)
            l = alpha * l
            probs = []
            for s, c, picked in subs:
                p = jnp.exp2(s - jnp.where(picked, m_new - c, float("inf")))
                l = l + jnp.sum(p, axis=0, keepdims=True)
                probs.append(p.astype(BF16))
            pv = jnp.dot(vtb[hh * HEAD_DIM:(hh + 1) * HEAD_DIM, :], jnp.concatenate(probs, axis=0),
                         preferred_element_type=F32)
            new += [m_new, l, acc * alpha + pv]
        return tuple(new)

    score_stage(sa_ref, 0)

    def body(it, carry):
        n0 = it * (2 * KV_GROUP)
        score_stage(sb_ref, n0 + KV_GROUP)
        carry = softmax_stage(sa_ref, n0, carry)
        score_stage(sa_ref, n0 + 2 * KV_GROUP)
        return softmax_stage(sb_ref, n0 + KV_GROUP, carry)

    init = (ms[0], ls[0], pvs[0], ms[1], ls[1], pvs[1])
    n_trips = (own + 2 * KV_GROUP - 1) // (2 * KV_GROUP)
    _, l0, acc0, _, l1, acc1 = lax.fori_loop(0, n_trips, body, init)
    out_t = jnp.concatenate([acc0 / l0, acc1 / l1], axis=0)
    o_ref[...] = out_t.T.astype(o_ref.dtype)


def _attend_prompt(q, k_bf, vt_bf, kmean, slopes, batch, seq):
    n_qb = seq // MOBA_BLOCK
    assert n_qb % (2 * KV_GROUP) == 0
    score_buf = pltpu.VMEM((HEADS_PER_TILE, KV_GROUP * MOBA_BLOCK, MOBA_BLOCK), F32)
    tile = lambda b, p, i: (b * n_qb + i, p)
    return pl.pallas_call(
        _attn_kernel,
        grid=(batch, N_HEADS // HEADS_PER_TILE, n_qb),
        in_specs=[pl.BlockSpec(memory_space=pltpu.SMEM),
                  pl.BlockSpec((MOBA_BLOCK, LANES), tile),
                  pl.BlockSpec((seq, LANES), lambda b, p, i: (b, p)),
                  pl.BlockSpec((1, LANES, seq), lambda b, p, i: (b, p, 0)),
                  pl.BlockSpec((1, n_qb, LANES), lambda b, p, i: (b, 0, p))],
        out_specs=pl.BlockSpec((MOBA_BLOCK, LANES), tile),
        out_shape=jax.ShapeDtypeStruct((batch * seq, D_ATTN), BF16),
        scratch_shapes=[pltpu.VMEM((HEADS_PER_TILE, MOBA_BLOCK, MOBA_BLOCK), F32),
                        pltpu.VMEM((HEADS_PER_TILE, n_qb, MOBA_BLOCK), F32), score_buf, score_buf],
        compiler_params=_params(3),
        name="attn_prompt",
    )(slopes, q, k_bf, vt_bf, kmean.reshape(batch, n_qb, D_ATTN))


def _decode_attn_kernel(pt_ref, qt_ref, knt_ref, vnt_ref, slope_ref, *refs, n_pages):
    k_pages, v_pages, o_ref = refs[:n_pages], refs[n_pages:2 * n_pages], refs[2 * n_pages]
    pages_per_blk = MOBA_BLOCK // PAGE_SIZE
    n_blk = n_pages // pages_per_blk
    past_len = n_pages * PAGE_SIZE
    req = pl.program_id(0)
    mine = lax.broadcasted_iota(I32, (D_ATTN, LANES), 1) == req

    @pl.when(req == 0)
    def _():
        o_ref[...] = jnp.zeros_like(o_ref)

    def column(ref):
        return jnp.sum(jnp.where(mine, ref[...], 0.0), axis=1, keepdims=True)

    def head_sums(x):
        head = lax.broadcasted_iota(I32, (N_HEADS, x.shape[1]), 0)
        out = jnp.zeros((N_HEADS, x.shape[1]), F32)
        for h in range(N_HEADS):
            part = jnp.sum(x[h * HEAD_DIM:(h + 1) * HEAD_DIM], axis=0, keepdims=True)
            out = jnp.where(head == h, part, out)
        return out

    def head_rows(x):
        return jnp.concatenate([jnp.broadcast_to(x[h:h + 1], (HEAD_DIM, x.shape[1]))
                                for h in range(N_HEADS)], axis=0)

    q, k_new, v_new = column(qt_ref), column(knt_ref), column(vnt_ref)
    raw = [head_sums(k_pages[p][0] * q) for p in range(n_pages)]

    lane = lax.broadcasted_iota(I32, (N_HEADS, LANES), 1)
    gate = jnp.full((N_HEADS, LANES), NEG_INF, F32)
    for n in range(n_blk):
        tot = raw[n * pages_per_blk]
        for j in range(1, pages_per_blk):
            tot = tot + raw[n * pages_per_blk + j]
        gate = jnp.where(lane == n, jnp.sum(tot, axis=1, keepdims=True) * (1.0 / MOBA_BLOCK), gate)
    sel = _select_top(gate, lane.astype(F32), MOBA_TOPK, LANES, 1)

    slope = slope_ref[...]
    scale = HEAD_DIM ** -0.5
    scores = []
    for p in range(n_pages):
        dist = (past_len - p * PAGE_SIZE - lane).astype(F32)
        n = p // pages_per_blk
        scores.append(jnp.where(sel[:, n:n + 1] > 0.0, raw[p] * scale - slope * dist, NEG_INF))
    s_new = head_sums(q * k_new) * scale

    m = s_new
    for s in scores:
        m = jnp.maximum(m, jnp.max(s, axis=1, keepdims=True))
    p_new = jnp.exp(s_new - m)
    l = p_new
    acc = jnp.zeros((D_ATTN, PAGE_SIZE), F32)
    for p in range(n_pages):
        prob = jnp.exp(scores[p] - m)
        l = l + jnp.sum(prob, axis=1, keepdims=True)
        acc = acc + v_pages[p][0] * head_rows(prob)
    out = (jnp.sum(acc, axis=1, keepdims=True) + head_rows(p_new) * v_new) / head_rows(l)
    o_ref[...] = jnp.where(mine, out, o_ref[...])


def _attend_sample(q, k_new, v_new, cache_k, cache_v, page_table, slopes_rows):
    n_req, n_pages = page_table.shape
    assert n_req == LANES
    n_phys = cache_k.shape[0]
    as_tiles = lambda c: c.transpose(0, 2, 3, 1).reshape(n_phys, D_ATTN, PAGE_SIZE)
    ck, cv = as_tiles(cache_k), as_tiles(cache_v)
    cols = pl.BlockSpec((D_ATTN, n_req), lambda r, pt: (0, 0))

    def page_spec(p):
        return pl.BlockSpec((1, D_ATTN, PAGE_SIZE), lambda r, pt: (pt[r * n_pages + p], 0, 0))

    pages = [page_spec(p) for p in range(n_pages)]
    grid_spec = pltpu.PrefetchScalarGridSpec(
        num_scalar_prefetch=1,
        grid=(n_req,),
        in_specs=[cols, cols, cols, pl.BlockSpec((N_HEADS, LANES), lambda r, pt: (0, 0))] + pages + pages,
        out_specs=cols,
    )
    out_t = pl.pallas_call(
        functools.partial(_decode_attn_kernel, n_pages=n_pages),
        grid_spec=grid_spec,
        out_shape=jax.ShapeDtypeStruct((D_ATTN, n_req), F32),
        compiler_params=_params(1),
        name="attn_sample",
    )(page_table.reshape(-1), q.T, k_new.T, v_new.T, slopes_rows,
      *([ck] * n_pages), *([cv] * n_pages))
    return out_t.T


def _branch_merge(y_conv, x, attn_bf, sg, mod, w, xo_ref, h2_ref, eidx_ref, gate_ref):
    (b_dw, ln_g, ln_b, w_ao, w_pw2, w_out, n2g, w_router, b_router) = w
    y = y_conv + b_dw[...]
    mu = jnp.mean(y, axis=-1, keepdims=True)
    yc = y - mu
    var = jnp.mean(yc * yc, axis=-1, keepdims=True)
    y = (yc * lax.rsqrt(var + EPS)) * ln_g[...] + ln_b[...]
    y = y * _sigmoid(y)
    c_out = jnp.dot(y.astype(BF16), w_pw2[...], preferred_element_type=F32)
    a_out = jnp.dot(attn_bf, w_ao[...], preferred_element_type=F32)
    merged = sg[:, 0:D_MODEL] * a_out + sg[:, D_MODEL:2 * D_MODEL] * c_out
    g1 = mod[:, 2 * D_MODEL:3 * D_MODEL]
    xo = x + g1 * jnp.dot(merged.astype(BF16), w_out[...], preferred_element_type=F32)
    xo_ref[...] = xo
    sh2, sc2 = mod[:, 3 * D_MODEL:4 * D_MODEL], mod[:, 4 * D_MODEL:5 * D_MODEL]
    h2 = _rms(xo, n2g[...]) * (1.0 + sc2) + sh2
    h2_ref[...] = h2

    logits = jnp.dot(h2, w_router[...], precision=lax.Precision.HIGHEST,
                     preferred_element_type=F32) + b_router[...]
    tm = logits.shape[0]
    lane = lax.broadcasted_iota(I32, (tm, LANES), 1)
    logits = jnp.where(lane < N_EXPERTS, logits, NEG_INF)
    eidx = lane.astype(F32)
    vals, e_out = [], jnp.zeros((tm, LANES), F32)
    for kk in range(TOP_K):
        mx = jnp.max(logits, axis=1, keepdims=True)
        first = jnp.min(jnp.where(logits == mx, eidx, float(N_EXPERTS)), axis=1, keepdims=True)
        logits = jnp.where(eidx == first, NEG_INF, logits)
        vals.append(mx)
        e_out = jnp.where(lane == kk, first, e_out)
    ex = [jnp.exp(v - vals[0]) for v in vals]
    den = ex[0] + ex[1] + ex[2] + ex[3]
    g_out = jnp.zeros((tm, LANES), F32)
    for kk in range(TOP_K):
        g_out = jnp.where(lane == kk, ex[kk] / den, g_out)
    eidx_ref[...] = e_out.astype(I32)
    gate_ref[...] = g_out


def _merge_prompt_kernel(x_ref, attn_ref, u_ref, halo_ref, sg_ref, mod_ref, wdw_ref, *refs, tiles_per_seq):
    w, (xo_ref, h2_ref, eidx_ref, gate_ref, full_ref) = refs[:9], refs[9:]
    tm = u_ref.shape[0]
    first = (pl.program_id(0) % tiles_per_seq) == 0
    full_ref[0:HALO, :] = jnp.where(first, 0.0, halo_ref[...])
    full_ref[HALO:HALO + tm, :] = u_ref[...]
    base = HALO - (CONV_WIDTH - 1)
    y = jnp.zeros((tm, D_CONV), F32)
    for j in range(CONV_WIDTH):
        y = y + wdw_ref[j:j + 1, :] * full_ref[base + j:base + j + tm, :]
    _branch_merge(y, x_ref[...], attn_ref[...], sg_ref[...], mod_ref[0], w,
                  xo_ref, h2_ref, eidx_ref, gate_ref)


def _merge_sample_kernel(x_ref, attn_ref, u_ref, state_ref, sg_ref, mod_ref, wdw_ref, *refs):
    w, (xo_ref, h2_ref, eidx_ref, gate_ref) = refs[:9], refs[9:]
    y = wdw_ref[CONV_WIDTH - 1:CONV_WIDTH, :] * u_ref[...]
    for j in range(CONV_WIDTH - 1):
        y = y + wdw_ref[j:j + 1, :] * state_ref[j]
    _branch_merge(y, x_ref[...], attn_ref[...].astype(BF16), sg_ref[...], mod_ref[...], w,
                  xo_ref, h2_ref, eidx_ref, gate_ref)


def _merge(x, attn, u, hist, sg, mod, weights, *, per_token, tm, tiles_per_seq):
    n = x.shape[0]
    row = lambda i: (i, 0)
    if per_token:
        kernel = _merge_sample_kernel
        hist_spec = _const_spec(hist.shape)
        scratch = []
    else:
        kernel = functools.partial(_merge_prompt_kernel, tiles_per_seq=tiles_per_seq)
        hist_spec = pl.BlockSpec((HALO, D_CONV), lambda i: (jnp.maximum(i * (tm // HALO) - 1, 0), 0))
        scratch = [pltpu.VMEM((HALO + tm, D_CONV), F32)]
    outs = [(D_MODEL, F32), (D_MODEL, F32), (LANES, I32), (LANES, F32)]
    return pl.pallas_call(
        kernel,
        grid=(n // tm,),
        in_specs=[pl.BlockSpec((tm, D_MODEL), row), pl.BlockSpec((tm, D_ATTN), row),
                  pl.BlockSpec((tm, D_CONV), row), hist_spec, pl.BlockSpec((tm, 2 * D_MODEL), row),
                  _mod_spec(per_token, tm, tiles_per_seq)] + [_const_spec(a.shape) for a in weights],
        out_specs=[pl.BlockSpec((tm, w), row) for w, _ in outs],
        out_shape=[jax.ShapeDtypeStruct((n, w), dt) for w, dt in outs],
        scratch_shapes=scratch,
        compiler_params=_params(1),
        name="merge",
    )(x, attn, u, hist, sg, mod, *weights)


def _onehots(e, lane):
    return [(e[:, kk:kk + 1] == lane) for kk in range(TOP_K)]


def _rank_kernel(e_ref, rank_ref, count_ref, carry_ref):
    tm = e_ref.shape[0]

    @pl.when(pl.program_id(0) == 0)
    def _():
        carry_ref[...] = jnp.zeros_like(carry_ref)

    lane = lax.broadcasted_iota(I32, (tm, LANES), 1)
    hots = _onehots(e_ref[...], lane)
    hot = jnp.zeros((tm, LANES), F32)
    for h in hots:
        hot = hot + h.astype(F32)
    r = lax.broadcasted_iota(I32, (tm, tm), 0)
    c = lax.broadcasted_iota(I32, (tm, tm), 1)
    earlier = (c < r).astype(BF16)
    before = carry_ref[...] + jnp.dot(earlier, hot.astype(BF16), preferred_element_type=F32)
    rank = jnp.zeros((tm, LANES), F32)
    for kk, h in enumerate(hots):
        rank = jnp.where(lane == kk, jnp.sum(jnp.where(h, before, 0.0), axis=1, keepdims=True), rank)
    rank_ref[...] = rank.astype(I32)
    carry_ref[...] = carry_ref[...] + jnp.sum(hot, axis=0, keepdims=True)
    count_ref[...] = jnp.broadcast_to(carry_ref[...], count_ref.shape)


def _slot_kernel(e_ref, rank_ref, count_ref, dest_ref, blk_ref, info_ref, *, n_blocks_pad):
    tm = e_ref.shape[0]
    counts = count_ref[0:1, :].astype(I32)
    padded = ((counts + (EXPERT_BLOCK - 1)) // EXPERT_BLOCK) * EXPERT_BLOCK
    r = lax.broadcasted_iota(I32, (LANES, LANES), 0)
    c = lax.broadcasted_iota(I32, (LANES, LANES), 1)
    pstart = jnp.dot(jnp.broadcast_to(padded.astype(F32), (8, LANES)), (r < c).astype(F32),
                     precision=lax.Precision.HIGHEST, preferred_element_type=F32)[0:1]
    pend = pstart + padded.astype(F32)
    lane = lax.broadcasted_iota(I32, (tm, LANES), 1)
    e = e_ref[...]
    dest = rank_ref[...]
    for kk, h in enumerate(_onehots(e, lane)):
        off = jnp.sum(jnp.where(h, pstart, 0.0), axis=1, keepdims=True).astype(I32)
        dest = dest + jnp.where(lane == kk, off, 0)
    dest_ref[...] = dest

    lane1 = lax.broadcasted_iota(I32, (1, LANES), 1)
    used = jnp.max(pend, axis=1, keepdims=True)
    blk_start = (lax.broadcasted_iota(I32, (n_blocks_pad, 1), 0) * EXPERT_BLOCK).astype(F32)
    blk_start = jnp.minimum(blk_start, used - EXPERT_BLOCK)
    done = jnp.where((lane1 < N_EXPERTS) & (pend <= blk_start), 1.0, 0.0)
    blk_exp = jnp.minimum(jnp.sum(done, axis=1, keepdims=True), N_EXPERTS - 1.0)
    blk_ref[...] = jnp.broadcast_to(blk_exp, blk_ref.shape).astype(I32)
    row8 = lax.broadcasted_iota(I32, (8, LANES), 0)
    info = jnp.where(row8 == 0, pstart, jnp.where(row8 == 1, pend, jnp.where(row8 == 2, used, 0.0)))
    info_ref[...] = info.astype(I32)


def _route(eidx, tm, n_blocks):
    n = eidx.shape[0]
    row = lambda i: (i, 0)
    fixed = lambda i: (0, 0)
    rank, counts = pl.pallas_call(
        _rank_kernel,
        grid=(n // tm,),
        in_specs=[pl.BlockSpec((tm, LANES), row)],
        out_specs=[pl.BlockSpec((tm, LANES), row), pl.BlockSpec((8, LANES), fixed)],
        out_shape=[jax.ShapeDtypeStruct((n, LANES), I32), jax.ShapeDtypeStruct((8, LANES), F32)],
        scratch_shapes=[pltpu.VMEM((1, LANES), F32)],
        compiler_params=_params(1),
        name="moe_rank",
    )(eidx)
    n_blocks_pad = -(-n_blocks // 8) * 8
    dest, blk, info = pl.pallas_call(
        functools.partial(_slot_kernel, n_blocks_pad=n_blocks_pad),
        grid=(n // tm,),
        in_specs=[pl.BlockSpec((tm, LANES), row), pl.BlockSpec((tm, LANES), row),
                  pl.BlockSpec((8, LANES), fixed)],
        out_specs=[pl.BlockSpec((tm, LANES), row), pl.BlockSpec((n_blocks_pad, LANES), fixed),
                   pl.BlockSpec((8, LANES), fixed)],
        out_shape=[jax.ShapeDtypeStruct((n, LANES), I32), jax.ShapeDtypeStruct((n_blocks_pad, LANES), I32),
                   jax.ShapeDtypeStruct((8, LANES), I32)],
        compiler_params=_params(1),
        name="moe_slot",
    )(eidx, rank, counts)
    dest_flat = dest[:, :TOP_K].reshape(-1)
    blk_exp = blk[:n_blocks, 0]
    pend = info[1, :N_EXPERTS]
    n_used = jnp.right_shift(info[2, 0:1], EXPERT_BLOCK.bit_length() - 1)
    return dest_flat, blk_exp, pend, n_used


def _row_copy(src_ref, src_row, dst_ref, dst_row, sem):
    return pltpu.make_async_copy(src_ref.at[pl.ds(src_row, 1)], dst_ref.at[pl.ds(dst_row, 1)], sem)


def _dispatch_kernel(dest_ref, pend_ref, h_ref, xs_ref, zero_ref, sem):
    tm = h_ref.shape[0]

    @pl.when(pl.program_id(0) == 0)
    def _():
        zero_ref[...] = jnp.zeros_like(zero_ref)
        n_blocks = xs_ref.shape[0] // EXPERT_BLOCK
        first_unused = pend_ref[N_EXPERTS - 1] // EXPERT_BLOCK

        def fill(start):
            return pltpu.make_async_copy(zero_ref, xs_ref.at[pl.ds(pl.multiple_of(start, EXPERT_BLOCK), EXPERT_BLOCK)], sem)

        def last_block(e):
            return jnp.maximum(pend_ref[e] - EXPERT_BLOCK, 0)

        @pl.loop(0, N_EXPERTS)
        def _(e):
            fill(last_block(e)).start()

        @pl.loop(first_unused, n_blocks)
        def _(j):
            fill(j * EXPERT_BLOCK).start()

        @pl.loop(0, N_EXPERTS)
        def _(e):
            fill(last_block(e)).wait()

        @pl.loop(first_unused, n_blocks)
        def _(j):
            fill(j * EXPERT_BLOCK).wait()

    @pl.loop(0, tm)
    def _(t):
        for kk in range(TOP_K):
            _row_copy(h_ref, t, xs_ref, dest_ref[t * TOP_K + kk], sem).start()

    @pl.loop(0, tm)
    def _(t):
        for kk in range(TOP_K):
            _row_copy(h_ref, t, xs_ref, dest_ref[t * TOP_K + kk], sem).wait()


def _dispatch(h2, dest_flat, pend, tm, cap):
    n = h2.shape[0]
    return pl.pallas_call(
        _dispatch_kernel,
        grid=(n // tm,),
        in_specs=[pl.BlockSpec((tm * TOP_K,), lambda i: (i,), memory_space=pltpu.SMEM),
                  pl.BlockSpec(memory_space=pltpu.SMEM),
                  pl.BlockSpec((tm, D_MODEL), lambda i: (i, 0))],
        out_specs=pl.BlockSpec(memory_space=pl.ANY),
        out_shape=jax.ShapeDtypeStruct((cap, D_MODEL), F32),
        scratch_shapes=[pltpu.VMEM((EXPERT_BLOCK, D_MODEL), F32), pltpu.SemaphoreType.DMA(())],
        compiler_params=_params(1, has_side_effects=True),
        name="moe_dispatch",
    )(dest_flat, pend, h2)


def _expert_kernel(blk_ref, used_ref, x_ref, wg_ref, wu_ref, wd_ref, y_ref, wg_bf, wu_bf, wd_bf):
    j = pl.program_id(0)
    active = j < used_ref[0]
    changed = (j == 0) | (blk_ref[j] != blk_ref[jnp.maximum(j - 1, 0)])

    @pl.when(active & changed)
    def _():
        wg_bf[...] = wg_ref[0].astype(BF16)
        wu_bf[...] = wu_ref[0].astype(BF16)
        wd_bf[...] = wd_ref[0].astype(BF16)

    @pl.when(active)
    def _():
        x = x_ref[...].astype(BF16)
        gt = jnp.minimum(jnp.dot(x, wg_bf[...], preferred_element_type=F32), SWIGLU_LIMIT)
        up = jnp.clip(jnp.dot(x, wu_bf[...], preferred_element_type=F32), -SWIGLU_LIMIT, SWIGLU_LIMIT)
        act = gt * _sigmoid(SWIGLU_ALPHA * gt) * (up + 1.0)
        y_ref[...] = jnp.dot(act.astype(BF16), wd_bf[...], preferred_element_type=F32)

    @pl.when(jnp.logical_not(active))
    def _():
        y_ref[...] = jnp.zeros_like(y_ref)


def _experts(xs, blk_exp, n_used, w_gate, w_up, w_down):
    n_blocks = blk_exp.shape[0]
    d_ff = w_gate.shape[2]
    xrow = lambda j, blk, used: (jnp.minimum(j, used[0] - 1), 0)
    wsel = lambda j, blk, used: (blk[j], 0, 0)
    grid_spec = pltpu.PrefetchScalarGridSpec(
        num_scalar_prefetch=2,
        grid=(n_blocks,),
        in_specs=[pl.BlockSpec((EXPERT_BLOCK, D_MODEL), xrow),
                  pl.BlockSpec((1, D_MODEL, d_ff), wsel),
                  pl.BlockSpec((1, D_MODEL, d_ff), wsel),
                  pl.BlockSpec((1, d_ff, D_MODEL), wsel)],
        out_specs=pl.BlockSpec((EXPERT_BLOCK, D_MODEL), lambda j, blk, used: (j, 0)),
        scratch_shapes=[pltpu.VMEM((D_MODEL, d_ff), BF16), pltpu.VMEM((D_MODEL, d_ff), BF16),
                        pltpu.VMEM((d_ff, D_MODEL), BF16)],
    )
    return pl.pallas_call(
        _expert_kernel,
        grid_spec=grid_spec,
        out_shape=jax.ShapeDtypeStruct(xs.shape, F32),
        compiler_params=_params(1),
        name="moe_experts",
    )(blk_exp, n_used, xs, w_gate, w_up, w_down)


def _combine_kernel(dest_ref, y_ref, xo_ref, gate_ref, mod_ref, fg_ref, o_ref, buf_ref, sem, *, per_token):
    tm = xo_ref.shape[0]

    @pl.loop(0, tm)
    def _(t):
        for kk in range(TOP_K):
            _row_copy(y_ref, dest_ref[t * TOP_K + kk], buf_ref.at[kk], t, sem).start()

    @pl.loop(0, tm)
    def _(t):
        for kk in range(TOP_K):
            _row_copy(y_ref, dest_ref[t * TOP_K + kk], buf_ref.at[kk], t, sem).wait()

    gates = gate_ref[...]
    moe = gates[:, 0:1] * buf_ref[0]
    for kk in range(1, TOP_K):
        moe = moe + gates[:, kk:kk + 1] * buf_ref[kk]
    g2 = _load_mod(mod_ref, per_token)[:, 5 * D_MODEL:6 * D_MODEL]
    o_ref[...] = _rms(xo_ref[...] + g2 * moe, fg_ref[...])


def _combine(y, dest_flat, xo, gates, mod, final_g, *, per_token, tm, tiles_per_seq):
    n = xo.shape[0]
    row = lambda i: (i, 0)
    return pl.pallas_call(
        functools.partial(_combine_kernel, per_token=per_token),
        grid=(n // tm,),
        in_specs=[pl.BlockSpec((tm * TOP_K,), lambda i: (i,), memory_space=pltpu.SMEM),
                  pl.BlockSpec(memory_space=pl.ANY),
                  pl.BlockSpec((tm, D_MODEL), row), pl.BlockSpec((tm, LANES), row),
                  _mod_spec(per_token, tm, tiles_per_seq), _const_spec((1, D_MODEL))],
        out_specs=pl.BlockSpec((tm, D_MODEL), row),
        out_shape=jax.ShapeDtypeStruct((n, D_MODEL), F32),
        scratch_shapes=[pltpu.VMEM((TOP_K, tm, D_MODEL), F32), pltpu.SemaphoreType.DMA(())],
        compiler_params=_params(1),
        name="moe_combine",
    )(dest_flat, y, xo, gates, mod, final_g)


def _moe(h2, eidx, gates, xo, mod, final_g, w_gate, w_up, w_down, *, per_token, tm, tiles_per_seq):
    n = h2.shape[0]
    n_blocks = -(-n * TOP_K // EXPERT_BLOCK) + N_EXPERTS
    dest_flat, blk_exp, pend, n_used = _route(eidx, tm, n_blocks)
    xs = _dispatch(h2, dest_flat, pend, tm, n_blocks * EXPERT_BLOCK)
    y = _experts(xs, blk_exp, n_used, w_gate, w_up, w_down)
    return _combine(y, dest_flat, xo, gates, mod, final_g,
                    per_token=per_token, tm=tm, tiles_per_seq=tiles_per_seq)


def kernel(x_prompt, x_sample, c_prompt, c_sample, cache_k, cache_v, state_conv, page_table, norm1_g, norm2_g,
           w_ada, b_ada, w_in, w_attn_out, w_dw, b_dw, ln_g, ln_b, w_pw2, w_out, w_router, b_router, w_gate,
           w_up, w_down, final_g):
    depth = w_in.shape[0]
    assert depth == 1, "single-layer trunk"
    batch, seq, _ = x_prompt.shape
    n_req = x_sample.shape[0]
    assert x_sample.shape[1] == 1 and seq % MOBA_BLOCK == 0 and n_req % 8 == 0
    layer = 0
    row2 = lambda a: a.reshape(1, -1)

    pad = (-batch) % 8
    c_all = jnp.concatenate([c_prompt, jnp.zeros((pad, D_MODEL), F32), c_sample], axis=0)
    ada = _ada(c_all, w_ada[layer], b_ada[layer])
    mod_p = ada[:batch].reshape(batch, 1, 6 * D_MODEL)
    mod_s = ada[batch + pad:]

    w_in_bf = w_in[layer].astype(BF16)
    merge_w = (row2(b_dw[layer]), row2(ln_g[layer]), row2(ln_b[layer]), w_attn_out[layer].astype(BF16),
               w_pw2[layer].astype(BF16), w_out[layer].astype(BF16), row2(norm2_g[layer]),
               jnp.pad(w_router[layer], ((0, 0), (0, LANES - N_EXPERTS))),
               jnp.pad(row2(b_router[layer]), ((0, 0), (0, LANES - N_EXPERTS))))
    w_dw_pad = jnp.pad(w_dw[layer], ((0, HALO - CONV_WIDTH), (0, 0)))
    slopes = jnp.exp2(-8.0 * jnp.arange(1, N_HEADS + 1, dtype=F32) / N_HEADS)
    slopes_rows = jnp.broadcast_to(slopes[:, None], (N_HEADS, LANES))
    fg = row2(final_g)
    experts = (w_gate[layer], w_up[layer], w_down[layer])

    tm = MOBA_BLOCK
    tps = seq // tm
    xp = x_prompt.reshape(batch * seq, D_MODEL)
    w_kvt_bf = w_in_bf[:, D_ATTN:3 * D_ATTN].T
    q, k_bf, kt, vt, vt_bf, u, sg, kmean = _inproj(xp, mod_p, row2(norm1_g[layer]), w_in_bf, w_kvt_bf,
                                                   per_token=False, tm=tm, tiles_per_seq=tps)
    attn = _attend_prompt(q, k_bf, vt_bf, kmean, slopes, batch, seq)
    xo, h2, eidx, gates = _merge(xp, attn, u, u, sg, mod_p, (w_dw_pad,) + merge_w,
                                 per_token=False, tm=tm, tiles_per_seq=tps)
    y_prompt = _moe(h2, eidx, gates, xo, mod_p, fg, *experts, per_token=False, tm=tm, tiles_per_seq=tps)
    to_cache = lambda t: t.reshape(batch, N_HEADS, HEAD_DIM, seq).transpose(0, 3, 1, 2)[None]
    k_prompt, v_prompt = to_cache(kt), to_cache(vt)
    conv_prompt = u.reshape(batch, seq, D_CONV)[None, :, seq - (CONV_WIDTH - 1):]

    xs = x_sample.reshape(n_req, D_MODEL)
    qs, ks, vs, us, sgs = _inproj(xs, mod_s, row2(norm1_g[layer]), w_in_bf,
                                  per_token=True, tm=n_req, tiles_per_seq=1)
    attn_s = _attend_sample(qs, ks, vs, cache_k[layer], cache_v[layer], page_table, slopes_rows)
    state = state_conv[layer]
    xo_s, h2_s, eidx_s, gates_s = _merge(xs, attn_s, us, state.transpose(1, 0, 2), sgs, mod_s,
                                         (w_dw_pad,) + merge_w, per_token=True, tm=n_req, tiles_per_seq=1)
    y_sample = _moe(h2_s, eidx_s, gates_s, xo_s, mod_s, fg, *experts, per_token=True, tm=n_req, tiles_per_seq=1)
    k_sample = ks.reshape(1, n_req, 1, N_HEADS, HEAD_DIM)
    v_sample = vs.reshape(1, n_req, 1, N_HEADS, HEAD_DIM)
    conv_sample = jnp.concatenate([state[:, 1:], us[:, None, :]], axis=1)[None]

    return (y_prompt.reshape(batch, seq, D_MODEL), y_sample.reshape(n_req, 1, D_MODEL),
            k_prompt, v_prompt, conv_prompt, k_sample, v_sample, conv_sample)
```

```python
import functools

import jax
import jax.numpy as jnp
from jax import lax
from jax.experimental import pallas as pl
from jax.experimental.pallas import tpu as pltpu

F32 = jnp.float32
BF16 = jnp.bfloat16
I32 = jnp.int32

D_MODEL = 1024
N_HEADS = 8
HEAD_DIM = 64
D_ATTN = N_HEADS * HEAD_DIM
D_CONV = 512
CONV_WIDTH = 31
N_EXPERTS = 32
TOP_K = 4
MOBA_BLOCK = 256
MOBA_TOPK = 3
PAGE_SIZE = 128
SWIGLU_ALPHA = 1.702
SWIGLU_LIMIT = 7.0
EPS = 1e-6
D_IN = 3 * D_ATTN + 2 * D_CONV + 2 * D_MODEL

LANES = 128
SUBLANES = 8
HALO = 32
EXPERT_BLOCK = 256
ROW_LOOP_UNROLL = 4
HEADS_PER_TILE = LANES // HEAD_DIM
KV_GROUP = 2
VMEM_LIMIT = 56 * 1024 * 1024

NT_DIMS = (((1,), (1,)), ((), ()))
NEG_INF = float("-inf")
LOG2E = 1.4426950408889634


def _params(n_axes, **kw):
    return pltpu.CompilerParams(dimension_semantics=("arbitrary",) * n_axes,
                                vmem_limit_bytes=VMEM_LIMIT, **kw)


def _const_spec(shape):
    nd = len(shape)
    return pl.BlockSpec(shape, lambda *_: (0,) * nd, pipeline_mode=pl.Buffered(1))


def _rms(x, g):
    return (x * lax.rsqrt(jnp.mean(x * x, axis=-1, keepdims=True) + EPS)) * g


def _sigmoid(x):
    return 1.0 / (1.0 + jnp.exp(-x))


def _split_bf16(w):
    hi = w.astype(BF16)
    return jnp.stack([hi, (w - hi.astype(F32)).astype(BF16)])


def _ada_kernel(c_ref, w_ref, b_ref, o_ref):
    o_ref[...] = jnp.dot(c_ref[...].astype(BF16), w_ref[...].astype(BF16),
                         preferred_element_type=F32) + b_ref[...]


def _ada(c, w_ada, b_ada):
    rows = c.shape[0]
    return pl.pallas_call(
        _ada_kernel,
        grid=(6,),
        in_specs=[pl.BlockSpec((rows, D_MODEL), lambda j: (0, 0)),
                  pl.BlockSpec((D_MODEL, D_MODEL), lambda j: (0, j)),
                  pl.BlockSpec((1, D_MODEL), lambda j: (0, j))],
        out_specs=pl.BlockSpec((rows, D_MODEL), lambda j: (0, j)),
        out_shape=jax.ShapeDtypeStruct((rows, 6 * D_MODEL), F32),
        compiler_params=_params(1),
        name="ada",
    )(c, w_ada, b_ada.reshape(1, -1))


def _mod_spec(per_token, tm, tiles_per_seq):
    if per_token:
        return pl.BlockSpec((tm, 6 * D_MODEL), lambda i: (i, 0))
    return pl.BlockSpec((1, 1, 6 * D_MODEL), lambda i: (i // tiles_per_seq, 0, 0))


def _load_mod(mod_ref, per_token):
    return mod_ref[...] if per_token else mod_ref[0]


def _inproj_common(x_ref, mod_ref, g_ref, w_ref, u_ref, sg_ref, per_token):
    mod = _load_mod(mod_ref, per_token)
    sh1, sc1 = mod[:, 0:D_MODEL], mod[:, D_MODEL:2 * D_MODEL]
    h = (_rms(x_ref[...], g_ref[...]) * (1.0 + sc1) + sh1).astype(BF16)

    def proj(lo, hi):
        return jnp.dot(h, w_ref[:, lo:hi], preferred_element_type=F32)

    o = 3 * D_ATTN
    u_ref[...] = proj(o, o + D_CONV) * _sigmoid(proj(o + D_CONV, o + 2 * D_CONV))
    o += 2 * D_CONV
    sg_ref[...] = _sigmoid(proj(o, o + 2 * D_MODEL))
    return h, proj


def _inproj_sample_kernel(x_ref, mod_ref, g_ref, w_ref, q_ref, k_ref, v_ref, u_ref, sg_ref):
    _, proj = _inproj_common(x_ref, mod_ref, g_ref, w_ref, u_ref, sg_ref, True)
    q_ref[...] = proj(0, D_ATTN)
    k_ref[...] = proj(D_ATTN, 2 * D_ATTN)
    v_ref[...] = proj(2 * D_ATTN, 3 * D_ATTN)


def _inproj_prompt_kernel(x_ref, mod_ref, g_ref, w_ref, wt_ref, q_ref, kb_ref, kt_ref, vt_ref, vtb_ref,
                          u_ref, sg_ref, km_ref):
    h, proj = _inproj_common(x_ref, mod_ref, g_ref, w_ref, u_ref, sg_ref, False)
    q_ref[...] = proj(0, D_ATTN)
    k = proj(D_ATTN, 2 * D_ATTN)
    kb_ref[...] = k.astype(BF16)
    km_ref[0] = jnp.mean(k, axis=0, keepdims=True)
    kt_ref[0] = lax.dot_general(wt_ref[0:D_ATTN, :], h, NT_DIMS, preferred_element_type=F32)
    vt = lax.dot_general(wt_ref[D_ATTN:2 * D_ATTN, :], h, NT_DIMS, preferred_element_type=F32)
    vt_ref[0] = vt
    vtb_ref[0] = vt.astype(BF16)


def _inproj(x, mod, norm_g, w_in_bf, w_kvt_bf=None, *, per_token, tm, tiles_per_seq):
    n = x.shape[0]
    row = lambda i: (i, 0)
    in_specs = [pl.BlockSpec((tm, D_MODEL), row), _mod_spec(per_token, tm, tiles_per_seq),
                _const_spec((1, D_MODEL)), _const_spec((D_MODEL, D_IN))]
    tail = [(D_CONV, F32), (2 * D_MODEL, F32)]
    if per_token:
        kernel, args = _inproj_sample_kernel, (x, mod, norm_g, w_in_bf)
        outs = [(D_ATTN, F32)] * 3 + tail
        out_shape = [jax.ShapeDtypeStruct((n, w), dt) for w, dt in outs]
        out_specs = [pl.BlockSpec((tm, w), row) for w, _ in outs]
    else:
        assert tm == MOBA_BLOCK
        kernel, args = _inproj_prompt_kernel, (x, mod, norm_g, w_in_bf, w_kvt_bf)
        in_specs.append(_const_spec((2 * D_ATTN, D_MODEL)))
        batch = n // (tm * tiles_per_seq)
        seq = tm * tiles_per_seq
        tok = lambda w, dt: (jax.ShapeDtypeStruct((n, w), dt), pl.BlockSpec((tm, w), row))
        tr = lambda dt: (jax.ShapeDtypeStruct((batch, D_ATTN, seq), dt),
                         pl.BlockSpec((1, D_ATTN, tm), lambda i: (i // tiles_per_seq, 0, i % tiles_per_seq)))
        km = (jax.ShapeDtypeStruct((n // tm, 1, D_ATTN), F32), pl.BlockSpec((1, 1, D_ATTN), lambda i: (i, 0, 0)))
        pairs = [tok(D_ATTN, F32), tok(D_ATTN, BF16), tr(F32), tr(F32), tr(BF16)] + [tok(*t) for t in tail] + [km]
        out_shape, out_specs = [p[0] for p in pairs], [p[1] for p in pairs]
    return pl.pallas_call(
        kernel,
        grid=(n // tm,),
        in_specs=in_specs,
        out_specs=out_specs,
        out_shape=out_shape,
        compiler_params=_params(1),
        name="inproj",
    )(*args)


def _select_top(g, idx, n_pick, n_idx, axis):
    sel = jnp.zeros(g.shape, F32)
    for _ in range(n_pick):
        mx = jnp.max(g, axis=axis, keepdims=True)
        first = jnp.min(jnp.where(g == mx, idx, float(n_idx)), axis=axis, keepdims=True)
        hit = idx == first
        sel = jnp.where(hit & (mx > NEG_INF), 1.0, sel)
        g = jnp.where(hit, NEG_INF, g)
    return sel


def _attn_kernel(slopes_ref, q_ref, k_ref, vt_ref, km_ref, o_ref, bias_ref, sel_ref, sa_ref, sb_ref):
    pair, own = pl.program_id(1), pl.program_id(2)
    blk = MOBA_BLOCK
    n_blk = km_ref.shape[1]
    key = lax.broadcasted_iota(I32, (blk, blk), 0)
    qry = lax.broadcasted_iota(I32, (blk, blk), 1)

    @pl.when(own == 0)
    def _():
        rel = (qry - key).astype(F32)
        for hh in range(HEADS_PER_TILE):
            bias_ref[hh] = -(slopes_ref[pair * HEADS_PER_TILE + hh] * LOG2E) * rel

    q = q_ref[...]
    lane = lax.broadcasted_iota(I32, (1, LANES), 1)
    km = km_ref[0]
    bidx = lax.broadcasted_iota(I32, (n_blk, blk), 0)
    bidx_f = bidx.astype(F32)
    own_off = pl.multiple_of(own * blk, blk)
    k_own = k_ref[pl.ds(own_off, blk), :]
    vt_own = vt_ref[0, :, pl.ds(own_off, blk)]

    qhs = [jnp.where((lane // HEAD_DIM) == hh, q, 0.0) for hh in range(HEADS_PER_TILE)]
    qs = [(qh * (HEAD_DIM ** -0.5 * LOG2E)).astype(BF16) for qh in qhs]
    span = KV_GROUP * blk

    def score_stage(dst_ref, n0):
        n0 = jnp.minimum(n0, n_blk - KV_GROUP)
        kb = k_ref[pl.ds(pl.multiple_of(n0 * blk, span), span), :]
        for hh in range(HEADS_PER_TILE):
            dst_ref[hh] = lax.dot_general(kb, qs[hh], NT_DIMS, preferred_element_type=F32)

    ms, ls, pvs = [], [], []
    for hh in range(HEADS_PER_TILE):
        gate = lax.dot_general(km, qhs[hh], NT_DIMS, precision=lax.Precision.HIGHEST,
                               preferred_element_type=F32)
        gate = jnp.where(bidx < own, gate, NEG_INF)
        sel_ref[hh] = _select_top(gate, bidx_f, MOBA_TOPK, n_blk, 0)
        s = lax.dot_general(k_own, qs[hh], NT_DIMS, preferred_element_type=F32) + bias_ref[hh]
        s = jnp.where(key <= qry, s, NEG_INF)
        m = jnp.max(s, axis=0, keepdims=True)
        p = jnp.exp2(s - m)
        ms.append(m)
        ls.append(jnp.sum(p, axis=0, keepdims=True))
        pvs.append(jnp.dot(vt_own[hh * HEAD_DIM:(hh + 1) * HEAD_DIM, :], p.astype(BF16),
                           preferred_element_type=F32))

    def softmax_stage(src_ref, n0, carry):
        vtb = vt_ref[0, :, pl.ds(pl.multiple_of(n0 * blk, span), span)]
        new = []
        for hh in range(HEADS_PER_TILE):
            m, l, acc = carry[3 * hh:3 * hh + 3]
            slope = slopes_ref[pair * HEADS_PER_TILE + hh] * LOG2E
            bias = bias_ref[hh]
            subs, m_new = [], m
            for j in range(KV_GROUP):
                s = src_ref[hh, j * blk:(j + 1) * blk, :] + bias
                c = -slope * ((own - n0 - j) * blk).astype(F32)
                picked = sel_ref[hh, pl.ds(n0 + j, 1), :] > 0.0
                m_new = jnp.maximum(m_new, jnp.where(picked, jnp.max(s, axis=0, keepdims=True) + c, NEG_INF))
                subs.append((s, c, picked))
            alpha = jnp.exp2(m - m_new)
            l = alpha * l
            probs = []
            for s, c, picked in subs:
                p = jnp.exp2(s - jnp.where(picked, m_new - c, float("inf")))
                l = l + jnp.sum(p, axis=0, keepdims=True)
                probs.append(p.astype(BF16))
            pv = jnp.dot(vtb[hh * HEAD_DIM:(hh + 1) * HEAD_DIM, :], jnp.concatenate(probs, axis=0),
                         preferred_element_type=F32)
            new += [m_new, l, acc * alpha + pv]
        return tuple(new)

    score_stage(sa_ref, 0)

    def body(it, carry):
        n0 = it * (2 * KV_GROUP)
        score_stage(sb_ref, n0 + KV_GROUP)
        carry = softmax_stage(sa_ref, n0, carry)
        score_stage(sa_ref, n0 + 2 * KV_GROUP)
        return softmax_stage(sb_ref, n0 + KV_GROUP, carry)

    init = (ms[0], ls[0], pvs[0], ms[1], ls[1], pvs[1])
    n_trips = (own + 2 * KV_GROUP - 1) // (2 * KV_GROUP)
    _, l0, acc0, _, l1, acc1 = lax.fori_loop(0, n_trips, body, init)
    out_t = jnp.concatenate([acc0 / l0, acc1 / l1], axis=0)
    o_ref[...] = out_t.T.astype(o_ref.dtype)


def _attend_prompt(q, k_bf, vt_bf, kmean, slopes, batch, seq):
    n_qb = seq // MOBA_BLOCK
    assert n_qb % (2 * KV_GROUP) == 0
    score_buf = pltpu.VMEM((HEADS_PER_TILE, KV_GROUP * MOBA_BLOCK, MOBA_BLOCK), F32)
    tile = lambda b, p, i: (b * n_qb + i, p)
    return pl.pallas_call(
        _attn_kernel,
        grid=(batch, N_HEADS // HEADS_PER_TILE, n_qb),
        in_specs=[pl.BlockSpec(memory_space=pltpu.SMEM),
                  pl.BlockSpec((MOBA_BLOCK, LANES), tile),
                  pl.BlockSpec((seq, LANES), lambda b, p, i: (b, p)),
                  pl.BlockSpec((1, LANES, seq), lambda b, p, i: (b, p, 0)),
                  pl.BlockSpec((1, n_qb, LANES), lambda b, p, i: (b, 0, p))],
        out_specs=pl.BlockSpec((MOBA_BLOCK, LANES), tile),
        out_shape=jax.ShapeDtypeStruct((batch * seq, D_ATTN), BF16),
        scratch_shapes=[pltpu.VMEM((HEADS_PER_TILE, MOBA_BLOCK, MOBA_BLOCK), F32),
                        pltpu.VMEM((HEADS_PER_TILE, n_qb, MOBA_BLOCK), F32), score_buf, score_buf],
        compiler_params=_params(3),
        name="attn_prompt",
    )(slopes, q, k_bf, vt_bf, kmean.reshape(batch, n_qb, D_ATTN))


def _decode_attn_kernel(pt_ref, qt_ref, knt_ref, vnt_ref, slope_ref, *refs, n_pages):
    k_pages, v_pages, o_ref = refs[:n_pages], refs[n_pages:2 * n_pages], refs[2 * n_pages]
    pages_per_blk = MOBA_BLOCK // PAGE_SIZE
    n_blk = n_pages // pages_per_blk
    past_len = n_pages * PAGE_SIZE
    req = pl.program_id(0)
    mine = lax.broadcasted_iota(I32, (D_ATTN, LANES), 1) == req

    @pl.when(req == 0)
    def _():
        o_ref[...] = jnp.zeros_like(o_ref)

    def column(ref):
        return jnp.sum(jnp.where(mine, ref[...], 0.0), axis=1, keepdims=True)

    def head_sums(x):
        head = lax.broadcasted_iota(I32, (N_HEADS, x.shape[1]), 0)
        out = jnp.zeros((N_HEADS, x.shape[1]), F32)
        for h in range(N_HEADS):
            part = jnp.sum(x[h * HEAD_DIM:(h + 1) * HEAD_DIM], axis=0, keepdims=True)
            out = jnp.where(head == h, part, out)
        return out

    def head_rows(x):
        return jnp.concatenate([jnp.broadcast_to(x[h:h + 1], (HEAD_DIM, x.shape[1]))
                                for h in range(N_HEADS)], axis=0)

    q, k_new, v_new = column(qt_ref), column(knt_ref), column(vnt_ref)
    raw = [head_sums(k_pages[p][0] * q) for p in range(n_pages)]

    lane = lax.broadcasted_iota(I32, (N_HEADS, LANES), 1)
    gate = jnp.full((N_HEADS, LANES), NEG_INF, F32)
    for n in range(n_blk):
        tot = raw[n * pages_per_blk]
        for j in range(1, pages_per_blk):
            tot = tot + raw[n * pages_per_blk + j]
        gate = jnp.where(lane == n, jnp.sum(tot, axis=1, keepdims=True) * (1.0 / MOBA_BLOCK), gate)
    sel = _select_top(gate, lane.astype(F32), MOBA_TOPK, LANES, 1)

    slope = slope_ref[...]
    scale = HEAD_DIM ** -0.5
    scores = []
    for p in range(n_pages):
        dist = (past_len - p * PAGE_SIZE - lane).astype(F32)
        n = p // pages_per_blk
        scores.append(jnp.where(sel[:, n:n + 1] > 0.0, raw[p] * scale - slope * dist, NEG_INF))
    s_new = head_sums(q * k_new) * scale

    m = s_new
    for s in scores:
        m = jnp.maximum(m, jnp.max(s, axis=1, keepdims=True))
    p_new = jnp.exp(s_new - m)
    l = p_new
    acc = jnp.zeros((D_ATTN, PAGE_SIZE), F32)
    for p in range(n_pages):
        prob = jnp.exp(scores[p] - m)
        l = l + jnp.sum(prob, axis=1, keepdims=True)
        acc = acc + v_pages[p][0] * head_rows(prob)
    out = (jnp.sum(acc, axis=1, keepdims=True) + head_rows(p_new) * v_new) / head_rows(l)
    o_ref[...] = jnp.where(mine, out, o_ref[...])


def _attend_sample(q, k_new, v_new, cache_k, cache_v, page_table, slopes_rows):
    n_req, n_pages = page_table.shape
    assert n_req == LANES
    n_phys = cache_k.shape[0]
    as_tiles = lambda c: c.transpose(0, 2, 3, 1).reshape(n_phys, D_ATTN, PAGE_SIZE)
    ck, cv = as_tiles(cache_k), as_tiles(cache_v)
    cols = pl.BlockSpec((D_ATTN, n_req), lambda r, pt: (0, 0))

    def page_spec(p):
        return pl.BlockSpec((1, D_ATTN, PAGE_SIZE), lambda r, pt: (pt[r * n_pages + p], 0, 0))

    pages = [page_spec(p) for p in range(n_pages)]
    grid_spec = pltpu.PrefetchScalarGridSpec(
        num_scalar_prefetch=1,
        grid=(n_req,),
        in_specs=[cols, cols, cols, pl.BlockSpec((N_HEADS, LANES), lambda r, pt: (0, 0))] + pages + pages,
        out_specs=cols,
    )
    out_t = pl.pallas_call(
        functools.partial(_decode_attn_kernel, n_pages=n_pages),
        grid_spec=grid_spec,
        out_shape=jax.ShapeDtypeStruct((D_ATTN, n_req), F32),
        compiler_params=_params(1),
        name="attn_sample",
    )(page_table.reshape(-1), q.T, k_new.T, v_new.T, slopes_rows,
      *([ck] * n_pages), *([cv] * n_pages))
    return out_t.T


def _branch_merge(y_conv, x, attn_bf, sg, mod, w, xo_ref, h2_ref, eidx_ref, gate_ref):
    (b_dw, ln_g, ln_b, w_ao, w_pw2, w_out, n2g, w_router, b_router) = w
    y = y_conv + b_dw[...]
    mu = jnp.mean(y, axis=-1, keepdims=True)
    yc = y - mu
    var = jnp.mean(yc * yc, axis=-1, keepdims=True)
    y = (yc * lax.rsqrt(var + EPS)) * ln_g[...] + ln_b[...]
    y = y * _sigmoid(y)
    c_out = jnp.dot(y.astype(BF16), w_pw2[...], preferred_element_type=F32)
    a_out = jnp.dot(attn_bf, w_ao[...], preferred_element_type=F32)
    merged = sg[:, 0:D_MODEL] * a_out + sg[:, D_MODEL:2 * D_MODEL] * c_out
    g1 = mod[:, 2 * D_MODEL:3 * D_MODEL]
    xo = x + g1 * jnp.dot(merged.astype(BF16), w_out[...], preferred_element_type=F32)
    xo_ref[...] = xo
    sh2, sc2 = mod[:, 3 * D_MODEL:4 * D_MODEL], mod[:, 4 * D_MODEL:5 * D_MODEL]
    h2 = _rms(xo, n2g[...]) * (1.0 + sc2) + sh2
    h2_ref[...] = h2

    h2_hi = h2.astype(BF16)
    h2_lo = (h2 - h2_hi.astype(F32)).astype(BF16)
    logits = (jnp.dot(h2_hi, w_router[0], preferred_element_type=F32)
              + jnp.dot(h2_lo, w_router[0], preferred_element_type=F32)
              + jnp.dot(h2_hi, w_router[1], preferred_element_type=F32)) + b_router[...]
    tm = logits.shape[0]
    lane = lax.broadcasted_iota(I32, (tm, LANES), 1)
    logits = jnp.where(lane < N_EXPERTS, logits, NEG_INF)
    eidx = lane.astype(F32)
    vals, e_out = [], jnp.zeros((tm, LANES), F32)
    for kk in range(TOP_K):
        mx = jnp.max(logits, axis=1, keepdims=True)
        first = jnp.min(jnp.where(logits == mx, eidx, float(N_EXPERTS)), axis=1, keepdims=True)
        logits = jnp.where(eidx == first, NEG_INF, logits)
        vals.append(mx)
        e_out = jnp.where(lane == kk, first, e_out)
    ex = [jnp.exp(v - vals[0]) for v in vals]
    den = ex[0] + ex[1] + ex[2] + ex[3]
    g_out = jnp.zeros((tm, LANES), F32)
    for kk in range(TOP_K):
        g_out = jnp.where(lane == kk, ex[kk] / den, g_out)
    eidx_ref[...] = e_out.astype(I32)
    gate_ref[...] = g_out


def _merge_prompt_kernel(x_ref, attn_ref, u_ref, halo_ref, sg_ref, mod_ref, wdw_ref, *refs, tiles_per_seq):
    w, (xo_ref, h2_ref, eidx_ref, gate_ref, full_ref, shifted_ref) = refs[:9], refs[9:]
    tm = u_ref.shape[0]
    first = (pl.program_id(0) % tiles_per_seq) == 0
    full_ref[0:HALO, :] = jnp.where(first, 0.0, halo_ref[...])
    full_ref[HALO:HALO + tm, :] = u_ref[...]
    base = HALO - (CONV_WIDTH - 1)
    y = jnp.zeros((tm, D_CONV), F32)
    for phase in range(SUBLANES):
        taps = [j for j in range(CONV_WIDTH) if (base + j) % SUBLANES == phase]
        rows = max(base + j for j in taps) - phase + tm
        shifted_ref[phase, 0:rows, :] = full_ref[phase:phase + rows, :]
        for j in taps:
            lo = base + j - phase
            y = y + wdw_ref[j:j + 1, :] * shifted_ref[phase, lo:lo + tm, :]
    _branch_merge(y, x_ref[...], attn_ref[...], sg_ref[...], mod_ref[0], w,
                  xo_ref, h2_ref, eidx_ref, gate_ref)


def _merge_sample_kernel(x_ref, attn_ref, u_ref, state_ref, sg_ref, mod_ref, wdw_ref, *refs):
    w, (xo_ref, h2_ref, eidx_ref, gate_ref) = refs[:9], refs[9:]
    y = wdw_ref[CONV_WIDTH - 1:CONV_WIDTH, :] * u_ref[...]
    for j in range(CONV_WIDTH - 1):
        y = y + wdw_ref[j:j + 1, :] * state_ref[j]
    _branch_merge(y, x_ref[...], attn_ref[...].astype(BF16), sg_ref[...], mod_ref[...], w,
                  xo_ref, h2_ref, eidx_ref, gate_ref)


def _merge(x, attn, u, hist, sg, mod, weights, *, per_token, tm, tiles_per_seq):
    n = x.shape[0]
    row = lambda i: (i, 0)
    if per_token:
        kernel = _merge_sample_kernel
        hist_spec = _const_spec(hist.shape)
        scratch = []
    else:
        kernel = functools.partial(_merge_prompt_kernel, tiles_per_seq=tiles_per_seq)
        hist_spec = pl.BlockSpec((HALO, D_CONV), lambda i: (jnp.maximum(i * (tm // HALO) - 1, 0), 0))
        scratch = [pltpu.VMEM((HALO + tm, D_CONV), F32), pltpu.VMEM((SUBLANES, HALO + tm, D_CONV), F32)]
    outs = [(D_MODEL, F32), (D_MODEL, F32), (LANES, I32), (LANES, F32)]
    return pl.pallas_call(
        kernel,
        grid=(n // tm,),
        in_specs=[pl.BlockSpec((tm, D_MODEL), row), pl.BlockSpec((tm, D_ATTN), row),
                  pl.BlockSpec((tm, D_CONV), row), hist_spec, pl.BlockSpec((tm, 2 * D_MODEL), row),
                  _mod_spec(per_token, tm, tiles_per_seq)] + [_const_spec(a.shape) for a in weights],
        out_specs=[pl.BlockSpec((tm, w), row) for w, _ in outs],
        out_shape=[jax.ShapeDtypeStruct((n, w), dt) for w, dt in outs],
        scratch_shapes=scratch,
        compiler_params=_params(1),
        name="merge",
    )(x, attn, u, hist, sg, mod, *weights)


def _onehots(e, lane):
    return [(e[:, kk:kk + 1] == lane) for kk in range(TOP_K)]


def _rank_kernel(e_ref, rank_ref, count_ref, carry_ref):
    tm = e_ref.shape[0]

    @pl.when(pl.program_id(0) == 0)
    def _():
        carry_ref[...] = jnp.zeros_like(carry_ref)

    lane = lax.broadcasted_iota(I32, (tm, LANES), 1)
    hots = _onehots(e_ref[...], lane)
    hot = jnp.zeros((tm, LANES), F32)
    for h in hots:
        hot = hot + h.astype(F32)
    r = lax.broadcasted_iota(I32, (tm, tm), 0)
    c = lax.broadcasted_iota(I32, (tm, tm), 1)
    earlier = (c < r).astype(BF16)
    before = carry_ref[...] + jnp.dot(earlier, hot.astype(BF16), preferred_element_type=F32)
    rank = jnp.zeros((tm, LANES), F32)
    for kk, h in enumerate(hots):
        rank = jnp.where(lane == kk, jnp.sum(jnp.where(h, before, 0.0), axis=1, keepdims=True), rank)
    rank_ref[...] = rank.astype(I32)
    carry_ref[...] = carry_ref[...] + jnp.sum(hot, axis=0, keepdims=True)
    count_ref[...] = jnp.broadcast_to(carry_ref[...], count_ref.shape)


def _slot_kernel(e_ref, rank_ref, count_ref, dest_ref, blk_ref, info_ref, *, n_blocks_pad):
    tm = e_ref.shape[0]
    counts = count_ref[0:1, :].astype(I32)
    padded = ((counts + (EXPERT_BLOCK - 1)) // EXPERT_BLOCK) * EXPERT_BLOCK
    r = lax.broadcasted_iota(I32, (LANES, LANES), 0)
    c = lax.broadcasted_iota(I32, (LANES, LANES), 1)
    pstart = jnp.dot(jnp.broadcast_to(padded.astype(F32), (8, LANES)), (r < c).astype(F32),
                     precision=lax.Precision.HIGHEST, preferred_element_type=F32)[0:1]
    pend = pstart + padded.astype(F32)
    lane = lax.broadcasted_iota(I32, (tm, LANES), 1)
    e = e_ref[...]
    dest = rank_ref[...]
    for kk, h in enumerate(_onehots(e, lane)):
        off = jnp.sum(jnp.where(h, pstart, 0.0), axis=1, keepdims=True).astype(I32)
        dest = dest + jnp.where(lane == kk, off, 0)
    dest_ref[...] = dest

    lane1 = lax.broadcasted_iota(I32, (1, LANES), 1)
    used = jnp.max(pend, axis=1, keepdims=True)
    blk_start = (lax.broadcasted_iota(I32, (n_blocks_pad, 1), 0) * EXPERT_BLOCK).astype(F32)
    blk_start = jnp.minimum(blk_start, used - EXPERT_BLOCK)
    done = jnp.where((lane1 < N_EXPERTS) & (pend <= blk_start), 1.0, 0.0)
    blk_exp = jnp.minimum(jnp.sum(done, axis=1, keepdims=True), N_EXPERTS - 1.0)
    blk_ref[...] = jnp.broadcast_to(blk_exp, blk_ref.shape).astype(I32)
    row8 = lax.broadcasted_iota(I32, (8, LANES), 0)
    info = jnp.where(row8 == 0, pstart, jnp.where(row8 == 1, pend, jnp.where(row8 == 2, used, 0.0)))
    info_ref[...] = info.astype(I32)


def _route(eidx, tm, n_blocks):
    n = eidx.shape[0]
    row = lambda i: (i, 0)
    fixed = lambda i: (0, 0)
    rank, counts = pl.pallas_call(
        _rank_kernel,
        grid=(n // tm,),
        in_specs=[pl.BlockSpec((tm, LANES), row)],
        out_specs=[pl.BlockSpec((tm, LANES), row), pl.BlockSpec((8, LANES), fixed)],
        out_shape=[jax.ShapeDtypeStruct((n, LANES), I32), jax.ShapeDtypeStruct((8, LANES), F32)],
        scratch_shapes=[pltpu.VMEM((1, LANES), F32)],
        compiler_params=_params(1),
        name="moe_rank",
    )(eidx)
    n_blocks_pad = -(-n_blocks // 8) * 8
    dest, blk, info = pl.pallas_call(
        functools.partial(_slot_kernel, n_blocks_pad=n_blocks_pad),
        grid=(n // tm,),
        in_specs=[pl.BlockSpec((tm, LANES), row), pl.BlockSpec((tm, LANES), row),
                  pl.BlockSpec((8, LANES), fixed)],
        out_specs=[pl.BlockSpec((tm, LANES), row), pl.BlockSpec((n_blocks_pad, LANES), fixed),
                   pl.BlockSpec((8, LANES), fixed)],
        out_shape=[jax.ShapeDtypeStruct((n, LANES), I32), jax.ShapeDtypeStruct((n_blocks_pad, LANES), I32),
                   jax.ShapeDtypeStruct((8, LANES), I32)],
        compiler_params=_params(1),
        name="moe_slot",
    )(eidx, rank, counts)
    dest_flat = dest[:, :TOP_K].reshape(-1)
    blk_exp = blk[:n_blocks, 0]
    pend = info[1, :N_EXPERTS]
    n_used = jnp.right_shift(info[2, 0:1], EXPERT_BLOCK.bit_length() - 1)
    return dest_flat, blk_exp, pend, n_used


def _row_copy(src_ref, src_row, dst_ref, dst_row, sem):
    return pltpu.make_async_copy(src_ref.at[pl.ds(src_row, 1)], dst_ref.at[pl.ds(dst_row, 1)], sem)


def _dispatch_kernel(dest_ref, pend_ref, h_ref, xs_ref, zero_ref, sem):
    tm = h_ref.shape[0]

    @pl.when(pl.program_id(0) == 0)
    def _():
        zero_ref[...] = jnp.zeros_like(zero_ref)
        n_blocks = xs_ref.shape[0] // EXPERT_BLOCK
        first_unused = pend_ref[N_EXPERTS - 1] // EXPERT_BLOCK

        def fill(start):
            return pltpu.make_async_copy(zero_ref, xs_ref.at[pl.ds(pl.multiple_of(start, EXPERT_BLOCK), EXPERT_BLOCK)], sem)

        def last_block(e):
            return jnp.maximum(pend_ref[e] - EXPERT_BLOCK, 0)

        @pl.loop(0, N_EXPERTS)
        def _(e):
            fill(last_block(e)).start()

        @pl.loop(first_unused, n_blocks)
        def _(j):
            fill(j * EXPERT_BLOCK).start()

        @pl.loop(0, N_EXPERTS)
        def _(e):
            fill(last_block(e)).wait()

        @pl.loop(first_unused, n_blocks)
        def _(j):
            fill(j * EXPERT_BLOCK).wait()

    @pl.loop(0, tm, unroll=ROW_LOOP_UNROLL)
    def _(t):
        for kk in range(TOP_K):
            _row_copy(h_ref, t, xs_ref, dest_ref[t * TOP_K + kk], sem).start()

    @pl.loop(0, tm, unroll=ROW_LOOP_UNROLL)
    def _(t):
        for kk in range(TOP_K):
            _row_copy(h_ref, t, xs_ref, dest_ref[t * TOP_K + kk], sem).wait()


def _dispatch(h2, dest_flat, pend, tm, cap):
    n = h2.shape[0]
    return pl.pallas_call(
        _dispatch_kernel,
        grid=(n // tm,),
        in_specs=[pl.BlockSpec((tm * TOP_K,), lambda i: (i,), memory_space=pltpu.SMEM),
                  pl.BlockSpec(memory_space=pltpu.SMEM),
                  pl.BlockSpec((tm, D_MODEL), lambda i: (i, 0))],
        out_specs=pl.BlockSpec(memory_space=pl.ANY),
        out_shape=jax.ShapeDtypeStruct((cap, D_MODEL), F32),
        scratch_shapes=[pltpu.VMEM((EXPERT_BLOCK, D_MODEL), F32), pltpu.SemaphoreType.DMA(())],
        compiler_params=_params(1, has_side_effects=True),
        name="moe_dispatch",
    )(dest_flat, pend, h2)


def _expert_kernel(blk_ref, used_ref, x_ref, wg_ref, wu_ref, wd_ref, y_ref, wg_bf, wu_bf, wd_bf):
    j = pl.program_id(0)
    active = j < used_ref[0]
    changed = (j == 0) | (blk_ref[j] != blk_ref[jnp.maximum(j - 1, 0)])

    @pl.when(active & changed)
    def _():
        wg_bf[...] = wg_ref[0].astype(BF16)
        wu_bf[...] = wu_ref[0].astype(BF16)
        wd_bf[...] = wd_ref[0].astype(BF16)

    @pl.when(active)
    def _():
        x = x_ref[...].astype(BF16)
        gt = jnp.minimum(jnp.dot(x, wg_bf[...], preferred_element_type=F32), SWIGLU_LIMIT)
        up = jnp.clip(jnp.dot(x, wu_bf[...], preferred_element_type=F32), -SWIGLU_LIMIT, SWIGLU_LIMIT)
        act = gt * _sigmoid(SWIGLU_ALPHA * gt) * (up + 1.0)
        y_ref[...] = jnp.dot(act.astype(BF16), wd_bf[...], preferred_element_type=F32)

    @pl.when(jnp.logical_not(active))
    def _():
        y_ref[...] = jnp.zeros_like(y_ref)


def _experts(xs, blk_exp, n_used, w_gate, w_up, w_down):
    n_blocks = blk_exp.shape[0]
    d_ff = w_gate.shape[2]
    xrow = lambda j, blk, used: (jnp.minimum(j, used[0] - 1), 0)
    wsel = lambda j, blk, used: (blk[j], 0, 0)
    grid_spec = pltpu.PrefetchScalarGridSpec(
        num_scalar_prefetch=2,
        grid=(n_blocks,),
        in_specs=[pl.BlockSpec((EXPERT_BLOCK, D_MODEL), xrow),
                  pl.BlockSpec((1, D_MODEL, d_ff), wsel),
                  pl.BlockSpec((1, D_MODEL, d_ff), wsel),
                  pl.BlockSpec((1, d_ff, D_MODEL), wsel)],
        out_specs=pl.BlockSpec((EXPERT_BLOCK, D_MODEL), lambda j, blk, used: (j, 0)),
        scratch_shapes=[pltpu.VMEM((D_MODEL, d_ff), BF16), pltpu.VMEM((D_MODEL, d_ff), BF16),
                        pltpu.VMEM((d_ff, D_MODEL), BF16)],
    )
    return pl.pallas_call(
        _expert_kernel,
        grid_spec=grid_spec,
        out_shape=jax.ShapeDtypeStruct(xs.shape, F32),
        compiler_params=_params(1),
        name="moe_experts",
    )(blk_exp, n_used, xs, w_gate, w_up, w_down)


def _combine_kernel(dest_ref, y_ref, xo_ref, gate_ref, mod_ref, fg_ref, o_ref, buf_ref, sem, *, per_token):
    tm = xo_ref.shape[0]

    @pl.loop(0, tm, unroll=ROW_LOOP_UNROLL)
    def _(t):
        for kk in range(TOP_K):
            _row_copy(y_ref, dest_ref[t * TOP_K + kk], buf_ref.at[kk], t, sem).start()

    @pl.loop(0, tm, unroll=ROW_LOOP_UNROLL)
    def _(t):
        for kk in range(TOP_K):
            _row_copy(y_ref, dest_ref[t * TOP_K + kk], buf_ref.at[kk], t, sem).wait()

    gates = gate_ref[...]
    moe = gates[:, 0:1] * buf_ref[0]
    for kk in range(1, TOP_K):
        moe = moe + gates[:, kk:kk + 1] * buf_ref[kk]
    g2 = _load_mod(mod_ref, per_token)[:, 5 * D_MODEL:6 * D_MODEL]
    o_ref[...] = _rms(xo_ref[...] + g2 * moe, fg_ref[...])


def _combine(y, dest_flat, xo, gates, mod, final_g, *, per_token, tm, tiles_per_seq):
    n = xo.shape[0]
    row = lambda i: (i, 0)
    return pl.pallas_call(
        functools.partial(_combine_kernel, per_token=per_token),
        grid=(n // tm,),
        in_specs=[pl.BlockSpec((tm * TOP_K,), lambda i: (i,), memory_space=pltpu.SMEM),
                  pl.BlockSpec(memory_space=pl.ANY),
                  pl.BlockSpec((tm, D_MODEL), row), pl.BlockSpec((tm, LANES), row),
                  _mod_spec(per_token, tm, tiles_per_seq), _const_spec((1, D_MODEL))],
        out_specs=pl.BlockSpec((tm, D_MODEL), row),
        out_shape=jax.ShapeDtypeStruct((n, D_MODEL), F32),
        scratch_shapes=[pltpu.VMEM((TOP_K, tm, D_MODEL), F32), pltpu.SemaphoreType.DMA(())],
        compiler_params=_params(1),
        name="moe_combine",
    )(dest_flat, y, xo, gates, mod, final_g)


def _moe(h2, eidx, gates, xo, mod, final_g, w_gate, w_up, w_down, *, per_token, tm, tiles_per_seq):
    n = h2.shape[0]
    n_blocks = -(-n * TOP_K // EXPERT_BLOCK) + N_EXPERTS
    dest_flat, blk_exp, pend, n_used = _route(eidx, tm, n_blocks)
    xs = _dispatch(h2, dest_flat, pend, tm, n_blocks * EXPERT_BLOCK)
    y = _experts(xs, blk_exp, n_used, w_gate, w_up, w_down)
    return _combine(y, dest_flat, xo, gates, mod, final_g,
                    per_token=per_token, tm=tm, tiles_per_seq=tiles_per_seq)


def kernel(x_prompt, x_sample, c_prompt, c_sample, cache_k, cache_v, state_conv, page_table, norm1_g, norm2_g,
           w_ada, b_ada, w_in, w_attn_out, w_dw, b_dw, ln_g, ln_b, w_pw2, w_out, w_router, b_router, w_gate,
           w_up, w_down, final_g):
    depth = w_in.shape[0]
    assert depth == 1, "single-layer trunk"
    batch, seq, _ = x_prompt.shape
    n_req = x_sample.shape[0]
    assert x_sample.shape[1] == 1 and seq % MOBA_BLOCK == 0 and n_req % 8 == 0
    layer = 0
    row2 = lambda a: a.reshape(1, -1)

    pad = (-batch) % 8
    c_all = jnp.concatenate([c_prompt, jnp.zeros((pad, D_MODEL), F32), c_sample], axis=0)
    ada = _ada(c_all, w_ada[layer], b_ada[layer])
    mod_p = ada[:batch].reshape(batch, 1, 6 * D_MODEL)
    mod_s = ada[batch + pad:]

    w_in_bf = w_in[layer].astype(BF16)
    merge_w = (row2(b_dw[layer]), row2(ln_g[layer]), row2(ln_b[layer]), w_attn_out[layer].astype(BF16),
               w_pw2[layer].astype(BF16), w_out[layer].astype(BF16), row2(norm2_g[layer]),
               _split_bf16(jnp.pad(w_router[layer], ((0, 0), (0, LANES - N_EXPERTS)))),
               jnp.pad(row2(b_router[layer]), ((0, 0), (0, LANES - N_EXPERTS))))
    w_dw_pad = jnp.pad(w_dw[layer], ((0, HALO - CONV_WIDTH), (0, 0)))
    slopes = jnp.exp2(-8.0 * jnp.arange(1, N_HEADS + 1, dtype=F32) / N_HEADS)
    slopes_rows = jnp.broadcast_to(slopes[:, None], (N_HEADS, LANES))
    fg = row2(final_g)
    experts = (w_gate[layer], w_up[layer], w_down[layer])

    tm = MOBA_BLOCK
    tps = seq // tm
    xp = x_prompt.reshape(batch * seq, D_MODEL)
    w_kvt_bf = w_in_bf[:, D_ATTN:3 * D_ATTN].T
    q, k_bf, kt, vt, vt_bf, u, sg, kmean = _inproj(xp, mod_p, row2(norm1_g[layer]), w_in_bf, w_kvt_bf,
                                                   per_token=False, tm=tm, tiles_per_seq=tps)
    attn = _attend_prompt(q, k_bf, vt_bf, kmean, slopes, batch, seq)
    xo, h2, eidx, gates = _merge(xp, attn, u, u, sg, mod_p, (w_dw_pad,) + merge_w,
                                 per_token=False, tm=tm, tiles_per_seq=tps)
    y_prompt = _moe(h2, eidx, gates, xo, mod_p, fg, *experts, per_token=False, tm=tm, tiles_per_seq=tps)
    to_cache = lambda t: t.reshape(batch, N_HEADS, HEAD_DIM, seq).transpose(0, 3, 1, 2)[None]
    k_prompt, v_prompt = to_cache(kt), to_cache(vt)
    conv_prompt = u.reshape(batch, seq, D_CONV)[None, :, seq - (CONV_WIDTH - 1):]

    xs = x_sample.reshape(n_req, D_MODEL)
    qs, ks, vs, us, sgs = _inproj(xs, mod_s, row2(norm1_g[layer]), w_in_bf,
                                  per_token=True, tm=n_req, tiles_per_seq=1)
    attn_s = _attend_sample(qs, ks, vs, cache_k[layer], cache_v[layer], page_table, slopes_rows)
    state = state_conv[layer]
    xo_s, h2_s, eidx_s, gates_s = _merge(xs, attn_s, us, state.transpose(1, 0, 2), sgs, mod_s,
                                         (w_dw_pad,) + merge_w, per_token=True, tm=n_req, tiles_per_seq=1)
    y_sample = _moe(h2_s, eidx_s, gates_s, xo_s, mod_s, fg, *experts, per_token=True, tm=n_req, tiles_per_seq=1)
    k_sample = ks.reshape(1, n_req, 1, N_HEADS, HEAD_DIM)
    v_sample = vs.reshape(1, n_req, 1, N_HEADS, HEAD_DIM)
    conv_sample = jnp.concatenate([state[:, 1:], us[:, None, :]], axis=1)[None]

    return (y_prompt.reshape(batch, seq, D_MODEL), y_sample.reshape(n_req, 1, D_MODEL),
            k_prompt, v_prompt, conv_prompt, k_sample, v_sample, conv_sample)
```

```python
import functools

import jax
import jax.numpy as jnp
from jax import lax
from jax.experimental import pallas as pl
from jax.experimental.pallas import tpu as pltpu

F32 = jnp.float32
BF16 = jnp.bfloat16
I32 = jnp.int32

D_MODEL = 1024
N_HEADS = 8
HEAD_DIM = 64
D_ATTN = N_HEADS * HEAD_DIM
D_CONV = 512
CONV_WIDTH = 31
N_EXPERTS = 32
TOP_K = 4
MOBA_BLOCK = 256
MOBA_TOPK = 3
PAGE_SIZE = 128
SWIGLU_ALPHA = 1.702
SWIGLU_LIMIT = 7.0
EPS = 1e-6
D_IN = 3 * D_ATTN + 2 * D_CONV + 2 * D_MODEL

LANES = 128
SUBLANES = 8
HALO = 32
EXPERT_BLOCK = 256
ROW_LOOP_UNROLL = 4
HEADS_PER_TILE = LANES // HEAD_DIM
KV_GROUP = 2
VMEM_LIMIT = 56 * 1024 * 1024

NT_DIMS = (((1,), (1,)), ((), ()))
NEG_INF = float("-inf")
LOG2E = 1.4426950408889634


def _params(n_axes, **kw):
    return pltpu.CompilerParams(dimension_semantics=("arbitrary",) * n_axes,
                                vmem_limit_bytes=VMEM_LIMIT, **kw)


def _const_spec(shape):
    nd = len(shape)
    return pl.BlockSpec(shape, lambda *_: (0,) * nd, pipeline_mode=pl.Buffered(1))


def _rms(x, g):
    return (x * lax.rsqrt(jnp.mean(x * x, axis=-1, keepdims=True) + EPS)) * g


def _sigmoid(x):
    return 1.0 / (1.0 + jnp.exp(-x))


def _split_bf16(w):
    hi = w.astype(BF16)
    return jnp.stack([hi, (w - hi.astype(F32)).astype(BF16)])


def _ada_kernel(c_ref, w_ref, b_ref, o_ref):
    o_ref[...] = jnp.dot(c_ref[...].astype(BF16), w_ref[...].astype(BF16),
                         preferred_element_type=F32) + b_ref[...]


def _ada(c, w_ada, b_ada):
    rows = c.shape[0]
    return pl.pallas_call(
        _ada_kernel,
        grid=(6,),
        in_specs=[pl.BlockSpec((rows, D_MODEL), lambda j: (0, 0)),
                  pl.BlockSpec((D_MODEL, D_MODEL), lambda j: (0, j)),
                  pl.BlockSpec((1, D_MODEL), lambda j: (0, j))],
        out_specs=pl.BlockSpec((rows, D_MODEL), lambda j: (0, j)),
        out_shape=jax.ShapeDtypeStruct((rows, 6 * D_MODEL), F32),
        compiler_params=_params(1),
        name="ada",
    )(c, w_ada, b_ada.reshape(1, -1))


def _mod_spec(per_token, tm, tiles_per_seq):
    if per_token:
        return pl.BlockSpec((tm, 6 * D_MODEL), lambda i: (i, 0))
    return pl.BlockSpec((1, 1, 6 * D_MODEL), lambda i: (i // tiles_per_seq, 0, 0))


def _load_mod(mod_ref, per_token):
    return mod_ref[...] if per_token else mod_ref[0]


def _inproj_common(x_ref, mod_ref, g_ref, w_ref, u_ref, sg_ref, per_token):
    mod = _load_mod(mod_ref, per_token)
    sh1, sc1 = mod[:, 0:D_MODEL], mod[:, D_MODEL:2 * D_MODEL]
    h = (_rms(x_ref[...], g_ref[...]) * (1.0 + sc1) + sh1).astype(BF16)

    def proj(lo, hi):
        return jnp.dot(h, w_ref[:, lo:hi], preferred_element_type=F32)

    o = 3 * D_ATTN
    u_ref[...] = proj(o, o + D_CONV) * _sigmoid(proj(o + D_CONV, o + 2 * D_CONV))
    o += 2 * D_CONV
    sg_ref[...] = _sigmoid(proj(o, o + 2 * D_MODEL))
    return h, proj


def _inproj_sample_kernel(x_ref, mod_ref, g_ref, w_ref, q_ref, k_ref, v_ref, u_ref, sg_ref):
    _, proj = _inproj_common(x_ref, mod_ref, g_ref, w_ref, u_ref, sg_ref, True)
    q_ref[...] = proj(0, D_ATTN)
    k_ref[...] = proj(D_ATTN, 2 * D_ATTN)
    v_ref[...] = proj(2 * D_ATTN, 3 * D_ATTN)


def _inproj_prompt_kernel(x_ref, mod_ref, g_ref, w_ref, wt_ref, q_ref, kb_ref, kt_ref, vt_ref, vtb_ref,
                          u_ref, sg_ref, km_ref):
    h, proj = _inproj_common(x_ref, mod_ref, g_ref, w_ref, u_ref, sg_ref, False)
    q_ref[...] = proj(0, D_ATTN)
    k = proj(D_ATTN, 2 * D_ATTN)
    kb_ref[...] = k.astype(BF16)
    km_ref[0] = jnp.mean(k, axis=0, keepdims=True)
    kt_ref[0] = lax.dot_general(wt_ref[0:D_ATTN, :], h, NT_DIMS, preferred_element_type=F32)
    vt = lax.dot_general(wt_ref[D_ATTN:2 * D_ATTN, :], h, NT_DIMS, preferred_element_type=F32)
    vt_ref[0] = vt
    vtb_ref[0] = vt.astype(BF16)


def _inproj(x, mod, norm_g, w_in_bf, w_kvt_bf=None, *, per_token, tm, tiles_per_seq):
    n = x.shape[0]
    row = lambda i: (i, 0)
    in_specs = [pl.BlockSpec((tm, D_MODEL), row), _mod_spec(per_token, tm, tiles_per_seq),
                _const_spec((1, D_MODEL)), _const_spec((D_MODEL, D_IN))]
    tail = [(D_CONV, F32), (2 * D_MODEL, F32)]
    if per_token:
        kernel, args = _inproj_sample_kernel, (x, mod, norm_g, w_in_bf)
        outs = [(D_ATTN, F32)] * 3 + tail
        out_shape = [jax.ShapeDtypeStruct((n, w), dt) for w, dt in outs]
        out_specs = [pl.BlockSpec((tm, w), row) for w, _ in outs]
    else:
        assert tm == MOBA_BLOCK
        kernel, args = _inproj_prompt_kernel, (x, mod, norm_g, w_in_bf, w_kvt_bf)
        in_specs.append(_const_spec((2 * D_ATTN, D_MODEL)))
        batch = n // (tm * tiles_per_seq)
        seq = tm * tiles_per_seq
        tok = lambda w, dt: (jax.ShapeDtypeStruct((n, w), dt), pl.BlockSpec((tm, w), row))
        tr = lambda dt: (jax.ShapeDtypeStruct((batch, D_ATTN, seq), dt),
                         pl.BlockSpec((1, D_ATTN, tm), lambda i: (i // tiles_per_seq, 0, i % tiles_per_seq)))
        km = (jax.ShapeDtypeStruct((n // tm, 1, D_ATTN), F32), pl.BlockSpec((1, 1, D_ATTN), lambda i: (i, 0, 0)))
        pairs = [tok(D_ATTN, F32), tok(D_ATTN, BF16), tr(F32), tr(F32), tr(BF16)] + [tok(*t) for t in tail] + [km]
        out_shape, out_specs = [p[0] for p in pairs], [p[1] for p in pairs]
    return pl.pallas_call(
        kernel,
        grid=(n // tm,),
        in_specs=in_specs,
        out_specs=out_specs,
        out_shape=out_shape,
        compiler_params=_params(1),
        name="inproj",
    )(*args)


def _select_top(g, idx, n_pick, n_idx, axis):
    sel = jnp.zeros(g.shape, F32)
    for _ in range(n_pick):
        mx = jnp.max(g, axis=axis, keepdims=True)
        first = jnp.min(jnp.where(g == mx, idx, float(n_idx)), axis=axis, keepdims=True)
        hit = idx == first
        sel = jnp.where(hit & (mx > NEG_INF), 1.0, sel)
        g = jnp.where(hit, NEG_INF, g)
    return sel


def _attn_kernel(slopes_ref, q_ref, k_ref, vt_ref, km_ref, o_ref, bias_ref, sel_ref, sa_ref, sb_ref):
    pair, own = pl.program_id(1), pl.program_id(2)
    blk = MOBA_BLOCK
    n_blk = km_ref.shape[1]
    key = lax.broadcasted_iota(I32, (blk, blk), 0)
    qry = lax.broadcasted_iota(I32, (blk, blk), 1)

    @pl.when(own == 0)
    def _():
        rel = (qry - key).astype(F32)
        for hh in range(HEADS_PER_TILE):
            bias_ref[hh] = -(slopes_ref[pair * HEADS_PER_TILE + hh] * LOG2E) * rel

    q = q_ref[...]
    lane = lax.broadcasted_iota(I32, (1, LANES), 1)
    km = km_ref[0]
    bidx = lax.broadcasted_iota(I32, (n_blk, blk), 0)
    bidx_f = bidx.astype(F32)
    own_off = pl.multiple_of(own * blk, blk)
    k_own = k_ref[pl.ds(own_off, blk), :]
    vt_own = vt_ref[0, :, pl.ds(own_off, blk)]

    qhs = [jnp.where((lane // HEAD_DIM) == hh, q, 0.0) for hh in range(HEADS_PER_TILE)]
    qs = [(qh * (HEAD_DIM ** -0.5 * LOG2E)).astype(BF16) for qh in qhs]
    span = KV_GROUP * blk

    def score_stage(dst_ref, n0):
        n0 = jnp.minimum(n0, n_blk - KV_GROUP)
        kb = k_ref[pl.ds(pl.multiple_of(n0 * blk, span), span), :]
        for hh in range(HEADS_PER_TILE):
            dst_ref[hh] = lax.dot_general(kb, qs[hh], NT_DIMS, preferred_element_type=F32)

    ms, ls, pvs = [], [], []
    for hh in range(HEADS_PER_TILE):
        gate = lax.dot_general(km, qhs[hh], NT_DIMS, precision=lax.Precision.HIGHEST,
                               preferred_element_type=F32)
        gate = jnp.where(bidx < own, gate, NEG_INF)
        sel_ref[hh] = _select_top(gate, bidx_f, MOBA_TOPK, n_blk, 0)
        s = lax.dot_general(k_own, qs[hh], NT_DIMS, preferred_element_type=F32) + bias_ref[hh]
        s = jnp.where(key <= qry, s, NEG_INF)
        m = jnp.max(s, axis=0, keepdims=True)
        p = jnp.exp2(s - m)
        ms.append(m)
        ls.append(jnp.sum(p, axis=0, keepdims=True))
        pvs.append(jnp.dot(vt_own[hh * HEAD_DIM:(hh + 1) * HEAD_DIM, :], p.astype(BF16),
                           preferred_element_type=F32))

    def softmax_stage(src_ref, n0, carry):
        vtb = vt_ref[0, :, pl.ds(pl.multiple_of(n0 * blk, span), span)]
        new = []
        for hh in range(HEADS_PER_TILE):
            m, l, acc = carry[3 * hh:3 * hh + 3]
            slope = slopes_ref[pair * HEADS_PER_TILE + hh] * LOG2E
            bias = bias_ref[hh]
            subs, m_new = [], m
            for j in range(KV_GROUP):
                s = src_ref[hh, j * blk:(j + 1) * blk, :] + bias
                c = -slope * ((own - n0 - j) * blk).astype(F32)
                picked = sel_ref[hh, pl.ds(n0 + j, 1), :] > 0.0
                m_new = jnp.maximum(m_new, jnp.where(picked, jnp.max(s, axis=0, keepdims=True) + c, NEG_INF))
                subs.append((s, c, picked))
            alpha = jnp.exp2(m - m_new)
            l = alpha * l
            probs = []
            for s, c, picked in subs:
                p = jnp.exp2(s - jnp.where(picked, m_new - c, float("inf")))
                l = l + jnp.sum(p, axis=0, keepdims=True)
                probs.append(p.astype(BF16))
            pv = jnp.dot(vtb[hh * HEAD_DIM:(hh + 1) * HEAD_DIM, :], jnp.concatenate(probs, axis=0),
                         preferred_element_type=F32)
            new += [m_new, l, acc * alpha + pv]
        return tuple(new)

    score_stage(sa_ref, 0)

    def body(it, carry):
        n0 = it * (2 * KV_GROUP)
        score_stage(sb_ref, n0 + KV_GROUP)
        carry = softmax_stage(sa_ref, n0, carry)
        score_stage(sa_ref, n0 + 2 * KV_GROUP)
        return softmax_stage(sb_ref, n0 + KV_GROUP, carry)

    init = (ms[0], ls[0], pvs[0], ms[1], ls[1], pvs[1])
    n_trips = (own + 2 * KV_GROUP - 1) // (2 * KV_GROUP)
    _, l0, acc0, _, l1, acc1 = lax.fori_loop(0, n_trips, body, init)
    out_t = jnp.concatenate([acc0 / l0, acc1 / l1], axis=0)
    o_ref[...] = out_t.T.astype(o_ref.dtype)


def _attend_prompt(q, k_bf, vt_bf, kmean, slopes, batch, seq):
    n_qb = seq // MOBA_BLOCK
    assert n_qb % (2 * KV_GROUP) == 0
    score_buf = pltpu.VMEM((HEADS_PER_TILE, KV_GROUP * MOBA_BLOCK, MOBA_BLOCK), F32)
    tile = lambda b, p, i: (b * n_qb + i, p)
    return pl.pallas_call(
        _attn_kernel,
        grid=(batch, N_HEADS // HEADS_PER_TILE, n_qb),
        in_specs=[pl.BlockSpec(memory_space=pltpu.SMEM),
                  pl.BlockSpec((MOBA_BLOCK, LANES), tile),
                  pl.BlockSpec((seq, LANES), lambda b, p, i: (b, p)),
                  pl.BlockSpec((1, LANES, seq), lambda b, p, i: (b, p, 0)),
                  pl.BlockSpec((1, n_qb, LANES), lambda b, p, i: (b, 0, p))],
        out_specs=pl.BlockSpec((MOBA_BLOCK, LANES), tile),
        out_shape=jax.ShapeDtypeStruct((batch * seq, D_ATTN), BF16),
        scratch_shapes=[pltpu.VMEM((HEADS_PER_TILE, MOBA_BLOCK, MOBA_BLOCK), F32),
                        pltpu.VMEM((HEADS_PER_TILE, n_qb, MOBA_BLOCK), F32), score_buf, score_buf],
        compiler_params=_params(3),
        name="attn_prompt",
    )(slopes, q, k_bf, vt_bf, kmean.reshape(batch, n_qb, D_ATTN))


def _decode_attn_kernel(pt_ref, qt_ref, knt_ref, vnt_ref, slope_ref, *refs, n_pages):
    k_pages, v_pages, o_ref = refs[:n_pages], refs[n_pages:2 * n_pages], refs[2 * n_pages]
    pages_per_blk = MOBA_BLOCK // PAGE_SIZE
    n_blk = n_pages // pages_per_blk
    past_len = n_pages * PAGE_SIZE
    req = pl.program_id(0)
    mine = lax.broadcasted_iota(I32, (D_ATTN, LANES), 1) == req

    @pl.when(req == 0)
    def _():
        o_ref[...] = jnp.zeros_like(o_ref)

    def column(ref):
        return jnp.sum(jnp.where(mine, ref[...], 0.0), axis=1, keepdims=True)

    def head_sums(x):
        head = lax.broadcasted_iota(I32, (N_HEADS, x.shape[1]), 0)
        out = jnp.zeros((N_HEADS, x.shape[1]), F32)
        for h in range(N_HEADS):
            part = jnp.sum(x[h * HEAD_DIM:(h + 1) * HEAD_DIM], axis=0, keepdims=True)
            out = jnp.where(head == h, part, out)
        return out

    def head_rows(x):
        return jnp.concatenate([jnp.broadcast_to(x[h:h + 1], (HEAD_DIM, x.shape[1]))
                                for h in range(N_HEADS)], axis=0)

    q, k_new, v_new = column(qt_ref), column(knt_ref), column(vnt_ref)
    raw = [head_sums(k_pages[p][0] * q) for p in range(n_pages)]

    lane = lax.broadcasted_iota(I32, (N_HEADS, LANES), 1)
    gate = jnp.full((N_HEADS, LANES), NEG_INF, F32)
    for n in range(n_blk):
        tot = raw[n * pages_per_blk]
        for j in range(1, pages_per_blk):
            tot = tot + raw[n * pages_per_blk + j]
        gate = jnp.where(lane == n, jnp.sum(tot, axis=1, keepdims=True) * (1.0 / MOBA_BLOCK), gate)
    sel = _select_top(gate, lane.astype(F32), MOBA_TOPK, LANES, 1)

    slope = slope_ref[...]
    scale = HEAD_DIM ** -0.5
    scores = []
    for p in range(n_pages):
        dist = (past_len - p * PAGE_SIZE - lane).astype(F32)
        n = p // pages_per_blk
        scores.append(jnp.where(sel[:, n:n + 1] > 0.0, raw[p] * scale - slope * dist, NEG_INF))
    s_new = head_sums(q * k_new) * scale

    m = s_new
    for s in scores:
        m = jnp.maximum(m, jnp.max(s, axis=1, keepdims=True))
    p_new = jnp.exp(s_new - m)
    l = p_new
    acc = jnp.zeros((D_ATTN, PAGE_SIZE), F32)
    for p in range(n_pages):
        prob = jnp.exp(scores[p] - m)
        l = l + jnp.sum(prob, axis=1, keepdims=True)
        acc = acc + v_pages[p][0] * head_rows(prob)
    out = (jnp.sum(acc, axis=1, keepdims=True) + head_rows(p_new) * v_new) / head_rows(l)
    o_ref[...] = jnp.where(mine, out, o_ref[...])


def _attend_sample(q, k_new, v_new, cache_k, cache_v, page_table, slopes_rows):
    n_req, n_pages = page_table.shape
    assert n_req == LANES
    n_phys = cache_k.shape[0]
    as_tiles = lambda c: c.transpose(0, 2, 3, 1).reshape(n_phys, D_ATTN, PAGE_SIZE)
    ck, cv = as_tiles(cache_k), as_tiles(cache_v)
    cols = pl.BlockSpec((D_ATTN, n_req), lambda r, pt: (0, 0))

    def page_spec(p):
        return pl.BlockSpec((1, D_ATTN, PAGE_SIZE), lambda r, pt: (pt[r * n_pages + p], 0, 0))

    pages = [page_spec(p) for p in range(n_pages)]
    grid_spec = pltpu.PrefetchScalarGridSpec(
        num_scalar_prefetch=1,
        grid=(n_req,),
        in_specs=[cols, cols, cols, pl.BlockSpec((N_HEADS, LANES), lambda r, pt: (0, 0))] + pages + pages,
        out_specs=cols,
    )
    out_t = pl.pallas_call(
        functools.partial(_decode_attn_kernel, n_pages=n_pages),
        grid_spec=grid_spec,
        out_shape=jax.ShapeDtypeStruct((D_ATTN, n_req), F32),
        compiler_params=_params(1),
        name="attn_sample",
    )(page_table.reshape(-1), q.T, k_new.T, v_new.T, slopes_rows,
      *([ck] * n_pages), *([cv] * n_pages))
    return out_t.T


def _onehots(e, lane):
    return [(e[:, kk:kk + 1] == lane) for kk in range(TOP_K)]


def _expert_ranks(e, carry_ref):
    tm = e.shape[0]
    lane = lax.broadcasted_iota(I32, (tm, LANES), 1)
    hots = _onehots(e, lane)
    hot = jnp.zeros((tm, LANES), F32)
    for h in hots:
        hot = hot + h.astype(F32)
    r = lax.broadcasted_iota(I32, (tm, tm), 0)
    c = lax.broadcasted_iota(I32, (tm, tm), 1)
    earlier = (c < r).astype(BF16)
    before = carry_ref[...] + jnp.dot(earlier, hot.astype(BF16), preferred_element_type=F32)
    rank = jnp.zeros((tm, LANES), F32)
    for kk, h in enumerate(hots):
        rank = jnp.where(lane == kk, jnp.sum(jnp.where(h, before, 0.0), axis=1, keepdims=True), rank)
    carry_ref[...] = carry_ref[...] + jnp.sum(hot, axis=0, keepdims=True)
    return rank.astype(I32)


def _branch_merge(y_conv, x, attn_bf, sg, mod, w, outs, carry_ref):
    (b_dw, ln_g, ln_b, w_ao, w_pw2, w_out, n2g, w_router, b_router, count_in) = w
    xo_ref, h2_ref, eidx_ref, gate_ref, rank_ref, count_ref = outs

    @pl.when(pl.program_id(0) == 0)
    def _():
        carry_ref[...] = count_in[0:1, :]
    y = y_conv + b_dw[...]
    mu = jnp.mean(y, axis=-1, keepdims=True)
    yc = y - mu
    var = jnp.mean(yc * yc, axis=-1, keepdims=True)
    y = (yc * lax.rsqrt(var + EPS)) * ln_g[...] + ln_b[...]
    y = y * _sigmoid(y)
    c_out = jnp.dot(y.astype(BF16), w_pw2[...], preferred_element_type=F32)
    a_out = jnp.dot(attn_bf, w_ao[...], preferred_element_type=F32)
    merged = sg[:, 0:D_MODEL] * a_out + sg[:, D_MODEL:2 * D_MODEL] * c_out
    g1 = mod[:, 2 * D_MODEL:3 * D_MODEL]
    xo = x + g1 * jnp.dot(merged.astype(BF16), w_out[...], preferred_element_type=F32)
    xo_ref[...] = xo
    sh2, sc2 = mod[:, 3 * D_MODEL:4 * D_MODEL], mod[:, 4 * D_MODEL:5 * D_MODEL]
    h2 = _rms(xo, n2g[...]) * (1.0 + sc2) + sh2
    h2_ref[...] = h2

    h2_hi = h2.astype(BF16)
    h2_lo = (h2 - h2_hi.astype(F32)).astype(BF16)
    logits = (jnp.dot(h2_hi, w_router[0], preferred_element_type=F32)
              + jnp.dot(h2_lo, w_router[0], preferred_element_type=F32)
              + jnp.dot(h2_hi, w_router[1], preferred_element_type=F32)) + b_router[...]
    tm = logits.shape[0]
    lane = lax.broadcasted_iota(I32, (tm, LANES), 1)
    logits = jnp.where(lane < N_EXPERTS, logits, NEG_INF)
    eidx = lane.astype(F32)
    vals, e_out = [], jnp.zeros((tm, LANES), F32)
    for kk in range(TOP_K):
        mx = jnp.max(logits, axis=1, keepdims=True)
        first = jnp.min(jnp.where(logits == mx, eidx, float(N_EXPERTS)), axis=1, keepdims=True)
        logits = jnp.where(eidx == first, NEG_INF, logits)
        vals.append(mx)
        e_out = jnp.where(lane == kk, first, e_out)
    ex = [jnp.exp(v - vals[0]) for v in vals]
    den = ex[0] + ex[1] + ex[2] + ex[3]
    g_out = jnp.zeros((tm, LANES), F32)
    for kk in range(TOP_K):
        g_out = jnp.where(lane == kk, ex[kk] / den, g_out)
    experts = e_out.astype(I32)
    eidx_ref[...] = experts
    gate_ref[...] = g_out
    rank_ref[...] = _expert_ranks(experts, carry_ref)
    count_ref[...] = jnp.broadcast_to(carry_ref[...], count_ref.shape)


def _merge_prompt_kernel(x_ref, attn_ref, u_ref, halo_ref, sg_ref, mod_ref, wdw_ref, *refs, tiles_per_seq):
    w, outs, (full_ref, shifted_ref, carry_ref) = refs[:10], refs[10:16], refs[16:]
    tm = u_ref.shape[0]
    first = (pl.program_id(0) % tiles_per_seq) == 0
    full_ref[0:HALO, :] = jnp.where(first, 0.0, halo_ref[...])
    full_ref[HALO:HALO + tm, :] = u_ref[...]
    base = HALO - (CONV_WIDTH - 1)
    y = jnp.zeros((tm, D_CONV), F32)
    for phase in range(SUBLANES):
        taps = [j for j in range(CONV_WIDTH) if (base + j) % SUBLANES == phase]
        rows = max(base + j for j in taps) - phase + tm
        shifted_ref[phase, 0:rows, :] = full_ref[phase:phase + rows, :]
        for j in taps:
            lo = base + j - phase
            y = y + wdw_ref[j:j + 1, :] * shifted_ref[phase, lo:lo + tm, :]
    _branch_merge(y, x_ref[...], attn_ref[...], sg_ref[...], mod_ref[0], w, outs, carry_ref)


def _merge_sample_kernel(x_ref, attn_ref, u_ref, state_ref, sg_ref, mod_ref, wdw_ref, *refs):
    w, outs, (carry_ref,) = refs[:10], refs[10:16], refs[16:]
    y = wdw_ref[CONV_WIDTH - 1:CONV_WIDTH, :] * u_ref[...]
    for j in range(CONV_WIDTH - 1):
        y = y + wdw_ref[j:j + 1, :] * state_ref[j]
    _branch_merge(y, x_ref[...], attn_ref[...].astype(BF16), sg_ref[...], mod_ref[...], w, outs, carry_ref)


def _merge(x, attn, u, hist, sg, mod, weights, *, per_token, tm, tiles_per_seq):
    n = x.shape[0]
    row = lambda i: (i, 0)
    if per_token:
        kernel = _merge_sample_kernel
        hist_spec = _const_spec(hist.shape)
        scratch = []
    else:
        kernel = functools.partial(_merge_prompt_kernel, tiles_per_seq=tiles_per_seq)
        hist_spec = pl.BlockSpec((HALO, D_CONV), lambda i: (jnp.maximum(i * (tm // HALO) - 1, 0), 0))
        scratch = [pltpu.VMEM((HALO + tm, D_CONV), F32), pltpu.VMEM((SUBLANES, HALO + tm, D_CONV), F32)]
    scratch.append(pltpu.VMEM((1, LANES), F32))
    outs = [(D_MODEL, F32), (D_MODEL, F32), (LANES, I32), (LANES, F32), (LANES, I32)]
    return pl.pallas_call(
        kernel,
        grid=(n // tm,),
        in_specs=[pl.BlockSpec((tm, D_MODEL), row), pl.BlockSpec((tm, D_ATTN), row),
                  pl.BlockSpec((tm, D_CONV), row), hist_spec, pl.BlockSpec((tm, 2 * D_MODEL), row),
                  _mod_spec(per_token, tm, tiles_per_seq)] + [_const_spec(a.shape) for a in weights],
        out_specs=[pl.BlockSpec((tm, w), row) for w, _ in outs] + [pl.BlockSpec((8, LANES), lambda i: (0, 0))],
        out_shape=[jax.ShapeDtypeStruct((n, w), dt) for w, dt in outs] + [jax.ShapeDtypeStruct((8, LANES), F32)],
        scratch_shapes=scratch,
        compiler_params=_params(1),
        name="merge",
    )(x, attn, u, hist, sg, mod, *weights)


def _slot_kernel(e_ref, rank_ref, count_ref, dest_ref, blk_ref, info_ref, *, n_blocks_pad):
    tm = e_ref.shape[0]
    counts = count_ref[0:1, :].astype(I32)
    padded = ((counts + (EXPERT_BLOCK - 1)) // EXPERT_BLOCK) * EXPERT_BLOCK
    r = lax.broadcasted_iota(I32, (LANES, LANES), 0)
    c = lax.broadcasted_iota(I32, (LANES, LANES), 1)
    pstart = jnp.dot(jnp.broadcast_to(padded.astype(F32), (8, LANES)), (r < c).astype(F32),
                     precision=lax.Precision.HIGHEST, preferred_element_type=F32)[0:1]
    pend = pstart + padded.astype(F32)
    lane = lax.broadcasted_iota(I32, (tm, LANES), 1)
    e = e_ref[...]
    dest = rank_ref[...]
    for kk, h in enumerate(_onehots(e, lane)):
        off = jnp.sum(jnp.where(h, pstart, 0.0), axis=1, keepdims=True).astype(I32)
        dest = dest + jnp.where(lane == kk, off, 0)
    dest_ref[...] = dest

    lane1 = lax.broadcasted_iota(I32, (1, LANES), 1)
    used = jnp.max(pend, axis=1, keepdims=True)
    blk_start = (lax.broadcasted_iota(I32, (n_blocks_pad, 1), 0) * EXPERT_BLOCK).astype(F32)
    blk_start = jnp.minimum(blk_start, used - EXPERT_BLOCK)
    done = jnp.where((lane1 < N_EXPERTS) & (pend <= blk_start), 1.0, 0.0)
    blk_exp = jnp.minimum(jnp.sum(done, axis=1, keepdims=True), N_EXPERTS - 1.0)
    blk_ref[...] = jnp.broadcast_to(blk_exp, blk_ref.shape).astype(I32)
    row8 = lax.broadcasted_iota(I32, (8, LANES), 0)
    info = jnp.where(row8 == 0, pstart, jnp.where(row8 == 1, pend, jnp.where(row8 == 2, used, 0.0)))
    info_ref[...] = info.astype(I32)


def _slots(eidx, rank, counts, tm, n_blocks):
    n = eidx.shape[0]
    row = lambda i: (i, 0)
    fixed = lambda i: (0, 0)
    n_blocks_pad = -(-n_blocks // 8) * 8
    dest, blk, info = pl.pallas_call(
        functools.partial(_slot_kernel, n_blocks_pad=n_blocks_pad),
        grid=(n // tm,),
        in_specs=[pl.BlockSpec((tm, LANES), row), pl.BlockSpec((tm, LANES), row),
                  pl.BlockSpec((8, LANES), fixed)],
        out_specs=[pl.BlockSpec((tm, LANES), row), pl.BlockSpec((n_blocks_pad, LANES), fixed),
                   pl.BlockSpec((8, LANES), fixed)],
        out_shape=[jax.ShapeDtypeStruct((n, LANES), I32), jax.ShapeDtypeStruct((n_blocks_pad, LANES), I32),
                   jax.ShapeDtypeStruct((8, LANES), I32)],
        compiler_params=_params(1),
        name="moe_slot",
    )(eidx, rank, counts)
    dest_flat = dest[:, :TOP_K].reshape(-1)
    blk_exp = blk[:n_blocks, 0]
    pend = info[1, :N_EXPERTS]
    n_used = jnp.right_shift(info[2, 0:1], EXPERT_BLOCK.bit_length() - 1)
    return dest_flat, blk_exp, pend, n_used


def _row_copy(src_ref, src_row, dst_ref, dst_row, sem):
    return pltpu.make_async_copy(src_ref.at[pl.ds(src_row, 1)], dst_ref.at[pl.ds(dst_row, 1)], sem)


def _dispatch_kernel(dest_ref, pend_ref, h_ref, tail_ref, xs_ref, zero_ref, sem):
    @pl.when(pl.program_id(0) == 0)
    def _():
        zero_ref[...] = jnp.zeros_like(zero_ref)
        n_blocks = xs_ref.shape[0] // EXPERT_BLOCK
        first_unused = pend_ref[N_EXPERTS - 1] // EXPERT_BLOCK

        def fill(start):
            return pltpu.make_async_copy(zero_ref, xs_ref.at[pl.ds(pl.multiple_of(start, EXPERT_BLOCK), EXPERT_BLOCK)], sem)

        def last_block(e):
            return jnp.maximum(pend_ref[e] - EXPERT_BLOCK, 0)

        @pl.loop(0, N_EXPERTS)
        def _(e):
            fill(last_block(e)).start()

        @pl.loop(first_unused, n_blocks)
        def _(j):
            fill(j * EXPERT_BLOCK).start()

        @pl.loop(0, N_EXPERTS)
        def _(e):
            fill(last_block(e)).wait()

        @pl.loop(first_unused, n_blocks)
        def _(j):
            fill(j * EXPERT_BLOCK).wait()

    def scatter(h_ref):
        @pl.loop(0, h_ref.shape[0], unroll=ROW_LOOP_UNROLL)
        def _(t):
            for kk in range(TOP_K):
                _row_copy(h_ref, t, xs_ref, dest_ref[t * TOP_K + kk], sem).start()

        @pl.loop(0, h_ref.shape[0], unroll=ROW_LOOP_UNROLL)
        def _(t):
            for kk in range(TOP_K):
                _row_copy(h_ref, t, xs_ref, dest_ref[t * TOP_K + kk], sem).wait()

    is_tail = pl.program_id(0) == pl.num_programs(0) - 1

    @pl.when(jnp.logical_not(is_tail))
    def _():
        scatter(h_ref)

    @pl.when(is_tail)
    def _():
        scatter(tail_ref)


def _dispatch(h2, h2_tail, dest_flat, pend, tm, cap):
    n_tiles = h2.shape[0] // tm
    assert h2_tail.shape[0] <= tm and dest_flat.shape[0] == (n_tiles + 1) * tm * TOP_K
    return pl.pallas_call(
        _dispatch_kernel,
        grid=(n_tiles + 1,),
        in_specs=[pl.BlockSpec((tm * TOP_K,), lambda i: (i,), memory_space=pltpu.SMEM),
                  pl.BlockSpec(memory_space=pltpu.SMEM),
                  pl.BlockSpec((tm, D_MODEL), lambda i: (jnp.minimum(i, n_tiles - 1), 0)),
                  _const_spec(h2_tail.shape)],
        out_specs=pl.BlockSpec(memory_space=pl.ANY),
        out_shape=jax.ShapeDtypeStruct((cap, D_MODEL), F32),
        scratch_shapes=[pltpu.VMEM((EXPERT_BLOCK, D_MODEL), F32), pltpu.SemaphoreType.DMA(())],
        compiler_params=_params(1, has_side_effects=True),
        name="moe_dispatch",
    )(dest_flat, pend, h2, h2_tail)


def _expert_kernel(blk_ref, used_ref, x_ref, wg_ref, wu_ref, wd_ref, y_ref, wg_bf, wu_bf, wd_bf):
    j = pl.program_id(0)
    active = j < used_ref[0]
    changed = (j == 0) | (blk_ref[j] != blk_ref[jnp.maximum(j - 1, 0)])

    @pl.when(active & changed)
    def _():
        wg_bf[...] = wg_ref[0].astype(BF16)
        wu_bf[...] = wu_ref[0].astype(BF16)
        wd_bf[...] = wd_ref[0].astype(BF16)

    @pl.when(active)
    def _():
        x = x_ref[...].astype(BF16)
        gt = jnp.minimum(jnp.dot(x, wg_bf[...], preferred_element_type=F32), SWIGLU_LIMIT)
        up = jnp.clip(jnp.dot(x, wu_bf[...], preferred_element_type=F32), -SWIGLU_LIMIT, SWIGLU_LIMIT)
        act = gt * _sigmoid(SWIGLU_ALPHA * gt) * (up + 1.0)
        y_ref[...] = jnp.dot(act.astype(BF16), wd_bf[...], preferred_element_type=F32)

    @pl.when(jnp.logical_not(active))
    def _():
        y_ref[...] = jnp.zeros_like(y_ref)


def _experts(xs, blk_exp, n_used, w_gate, w_up, w_down):
    n_blocks = blk_exp.shape[0]
    d_ff = w_gate.shape[2]
    xrow = lambda j, blk, used: (jnp.minimum(j, used[0] - 1), 0)
    wsel = lambda j, blk, used: (blk[j], 0, 0)
    grid_spec = pltpu.PrefetchScalarGridSpec(
        num_scalar_prefetch=2,
        grid=(n_blocks,),
        in_specs=[pl.BlockSpec((EXPERT_BLOCK, D_MODEL), xrow),
                  pl.BlockSpec((1, D_MODEL, d_ff), wsel),
                  pl.BlockSpec((1, D_MODEL, d_ff), wsel),
                  pl.BlockSpec((1, d_ff, D_MODEL), wsel)],
        out_specs=pl.BlockSpec((EXPERT_BLOCK, D_MODEL), lambda j, blk, used: (j, 0)),
        scratch_shapes=[pltpu.VMEM((D_MODEL, d_ff), BF16), pltpu.VMEM((D_MODEL, d_ff), BF16),
                        pltpu.VMEM((d_ff, D_MODEL), BF16)],
    )
    return pl.pallas_call(
        _expert_kernel,
        grid_spec=grid_spec,
        out_shape=jax.ShapeDtypeStruct(xs.shape, F32),
        compiler_params=_params(1),
        name="moe_experts",
    )(blk_exp, n_used, xs, w_gate, w_up, w_down)


def _combine_kernel(dest_ref, y_ref, xo_ref, gate_ref, mod_ref, fg_ref, o_ref, buf_ref, sem, *, per_token):
    tm = xo_ref.shape[0]

    @pl.loop(0, tm, unroll=ROW_LOOP_UNROLL)
    def _(t):
        for kk in range(TOP_K):
            _row_copy(y_ref, dest_ref[t * TOP_K + kk], buf_ref.at[kk], t, sem).start()

    @pl.loop(0, tm, unroll=ROW_LOOP_UNROLL)
    def _(t):
        for kk in range(TOP_K):
            _row_copy(y_ref, dest_ref[t * TOP_K + kk], buf_ref.at[kk], t, sem).wait()

    gates = gate_ref[...]
    moe = gates[:, 0:1] * buf_ref[0]
    for kk in range(1, TOP_K):
        moe = moe + gates[:, kk:kk + 1] * buf_ref[kk]
    g2 = _load_mod(mod_ref, per_token)[:, 5 * D_MODEL:6 * D_MODEL]
    o_ref[...] = _rms(xo_ref[...] + g2 * moe, fg_ref[...])


def _combine(y, dest_flat, xo, gates, mod, final_g, *, per_token, tm, tiles_per_seq):
    n = xo.shape[0]
    row = lambda i: (i, 0)
    return pl.pallas_call(
        functools.partial(_combine_kernel, per_token=per_token),
        grid=(n // tm,),
        in_specs=[pl.BlockSpec((tm * TOP_K,), lambda i: (i,), memory_space=pltpu.SMEM),
                  pl.BlockSpec(memory_space=pl.ANY),
                  pl.BlockSpec((tm, D_MODEL), row), pl.BlockSpec((tm, LANES), row),
                  _mod_spec(per_token, tm, tiles_per_seq), _const_spec((1, D_MODEL))],
        out_specs=pl.BlockSpec((tm, D_MODEL), row),
        out_shape=jax.ShapeDtypeStruct((n, D_MODEL), F32),
        scratch_shapes=[pltpu.VMEM((TOP_K, tm, D_MODEL), F32), pltpu.SemaphoreType.DMA(())],
        compiler_params=_params(1),
        name="moe_combine",
    )(dest_flat, y, xo, gates, mod, final_g)


def _moe(prompt, sample, counts, final_g, w_gate, w_up, w_down, *, tm, tiles_per_seq):
    n_p, n_s = prompt[0].shape[0], sample[0].shape[0]
    n_blocks = -(-(n_p + n_s) * TOP_K // EXPERT_BLOCK) + N_EXPERTS
    dest_p, blk_exp, pend, n_used = _slots(prompt[1], prompt[3], counts, tm, n_blocks)
    dest_s, _, _, _ = _slots(sample[1], sample[3], counts, n_s, n_blocks)
    dest_all = jnp.concatenate([dest_p, dest_s, jnp.zeros(((tm - n_s) * TOP_K,), I32)])
    xs = _dispatch(prompt[0], sample[0], dest_all, pend, tm, n_blocks * EXPERT_BLOCK)
    y = _experts(xs, blk_exp, n_used, w_gate, w_up, w_down)
    y_p = _combine(y, dest_p, prompt[4], prompt[2], prompt[5], final_g,
                   per_token=False, tm=tm, tiles_per_seq=tiles_per_seq)
    y_s = _combine(y, dest_s, sample[4], sample[2], sample[5], final_g,
                   per_token=True, tm=n_s, tiles_per_seq=1)
    return y_p, y_s


def kernel(x_prompt, x_sample, c_prompt, c_sample, cache_k, cache_v, state_conv, page_table, norm1_g, norm2_g,
           w_ada, b_ada, w_in, w_attn_out, w_dw, b_dw, ln_g, ln_b, w_pw2, w_out, w_router, b_router, w_gate,
           w_up, w_down, final_g):
    depth = w_in.shape[0]
    assert depth == 1, "single-layer trunk"
    batch, seq, _ = x_prompt.shape
    n_req = x_sample.shape[0]
    assert x_sample.shape[1] == 1 and seq % MOBA_BLOCK == 0 and n_req % 8 == 0
    layer = 0
    row2 = lambda a: a.reshape(1, -1)

    pad = (-batch) % 8
    c_all = jnp.concatenate([c_prompt, jnp.zeros((pad, D_MODEL), F32), c_sample], axis=0)
    ada = _ada(c_all, w_ada[layer], b_ada[layer])
    mod_p = ada[:batch].reshape(batch, 1, 6 * D_MODEL)
    mod_s = ada[batch + pad:]

    w_in_bf = w_in[layer].astype(BF16)
    merge_w = (row2(b_dw[layer]), row2(ln_g[layer]), row2(ln_b[layer]), w_attn_out[layer].astype(BF16),
               w_pw2[layer].astype(BF16), w_out[layer].astype(BF16), row2(norm2_g[layer]),
               _split_bf16(jnp.pad(w_router[layer], ((0, 0), (0, LANES - N_EXPERTS)))),
               jnp.pad(row2(b_router[layer]), ((0, 0), (0, LANES - N_EXPERTS))))
    w_dw_pad = jnp.pad(w_dw[layer], ((0, HALO - CONV_WIDTH), (0, 0)))
    slopes = jnp.exp2(-8.0 * jnp.arange(1, N_HEADS + 1, dtype=F32) / N_HEADS)
    slopes_rows = jnp.broadcast_to(slopes[:, None], (N_HEADS, LANES))
    fg = row2(final_g)
    experts = (w_gate[layer], w_up[layer], w_down[layer])

    tm = MOBA_BLOCK
    tps = seq // tm
    xp = x_prompt.reshape(batch * seq, D_MODEL)
    w_kvt_bf = w_in_bf[:, D_ATTN:3 * D_ATTN].T
    q, k_bf, kt, vt, vt_bf, u, sg, kmean = _inproj(xp, mod_p, row2(norm1_g[layer]), w_in_bf, w_kvt_bf,
                                                   per_token=False, tm=tm, tiles_per_seq=tps)
    attn = _attend_prompt(q, k_bf, vt_bf, kmean, slopes, batch, seq)
    no_counts = jnp.zeros((8, LANES), F32)
    xo, h2, eidx, gates, rank, counts_p = _merge(xp, attn, u, u, sg, mod_p, (w_dw_pad,) + merge_w + (no_counts,),
                                                 per_token=False, tm=tm, tiles_per_seq=tps)
    to_cache = lambda t: t.reshape(batch, N_HEADS, HEAD_DIM, seq).transpose(0, 3, 1, 2)[None]
    k_prompt, v_prompt = to_cache(kt), to_cache(vt)
    conv_prompt = u.reshape(batch, seq, D_CONV)[None, :, seq - (CONV_WIDTH - 1):]

    xs = x_sample.reshape(n_req, D_MODEL)
    qs, ks, vs, us, sgs = _inproj(xs, mod_s, row2(norm1_g[layer]), w_in_bf,
                                  per_token=True, tm=n_req, tiles_per_seq=1)
    attn_s = _attend_sample(qs, ks, vs, cache_k[layer], cache_v[layer], page_table, slopes_rows)
    state = state_conv[layer]
    xo_s, h2_s, eidx_s, gates_s, rank_s, counts = _merge(
        xs, attn_s, us, state.transpose(1, 0, 2), sgs, mod_s, (w_dw_pad,) + merge_w + (counts_p,),
        per_token=True, tm=n_req, tiles_per_seq=1)

    y_prompt, y_sample = _moe((h2, eidx, gates, rank, xo, mod_p), (h2_s, eidx_s, gates_s, rank_s, xo_s, mod_s),
                              counts, fg, *experts, tm=tm, tiles_per_seq=tps)
    k_sample = ks.reshape(1, n_req, 1, N_HEADS, HEAD_DIM)
    v_sample = vs.reshape(1, n_req, 1, N_HEADS, HEAD_DIM)
    conv_sample = jnp.concatenate([state[:, 1:], us[:, None, :]], axis=1)[None]

    return (y_prompt.reshape(batch, seq, D_MODEL), y_sample.reshape(n_req, 1, D_MODEL),
            k_prompt, v_prompt, conv_prompt, k_sample, v_sample, conv_sample)
```

```python
import functools

import jax
import jax.numpy as jnp
from jax import lax
from jax.experimental import pallas as pl
from jax.experimental.pallas import tpu as pltpu

F32 = jnp.float32
BF16 = jnp.bfloat16
I32 = jnp.int32

D_MODEL = 1024
N_HEADS = 8
HEAD_DIM = 64
D_ATTN = N_HEADS * HEAD_DIM
D_CONV = 512
CONV_WIDTH = 31
N_EXPERTS = 32
TOP_K = 4
MOBA_BLOCK = 256
MOBA_TOPK = 3
PAGE_SIZE = 128
SWIGLU_ALPHA = 1.702
SWIGLU_LIMIT = 7.0
EPS = 1e-6
D_IN = 3 * D_ATTN + 2 * D_CONV + 2 * D_MODEL

LANES = 128
SUBLANES = 8
HALO = 32
EXPERT_BLOCK = 256
ROW_LOOP_UNROLL = 4
HEADS_PER_TILE = LANES // HEAD_DIM
KV_GROUP = 2
VMEM_LIMIT = 56 * 1024 * 1024

NT_DIMS = (((1,), (1,)), ((), ()))
NEG_INF = float("-inf")
LOG2E = 1.4426950408889634


def _params(n_axes, **kw):
    return pltpu.CompilerParams(dimension_semantics=("arbitrary",) * n_axes,
                                vmem_limit_bytes=VMEM_LIMIT, **kw)


def _const_spec(shape):
    nd = len(shape)
    return pl.BlockSpec(shape, lambda *_: (0,) * nd, pipeline_mode=pl.Buffered(1))


def _rms(x, g):
    return (x * lax.rsqrt(jnp.mean(x * x, axis=-1, keepdims=True) + EPS)) * g


def _sigmoid(x):
    return 1.0 / (1.0 + jnp.exp(-x))


ROW_PITCH = D_MODEL // LANES


def _tiled(n_rows):
    return (n_rows * ROW_PITCH, LANES)


def _rows_to_tiles(ref, x):
    for c in range(ROW_PITCH):
        ref[pl.ds(c, x.shape[0], stride=ROW_PITCH), :] = x[:, c * LANES:(c + 1) * LANES]


def _tiles_to_rows(ref):
    r = ref.shape[0] // ROW_PITCH
    return jnp.concatenate([ref[pl.ds(c, r, stride=ROW_PITCH), :] for c in range(ROW_PITCH)], axis=1)


def _split_bf16(w):
    hi = w.astype(BF16)
    return jnp.stack([hi, (w - hi.astype(F32)).astype(BF16)])


def _ada_kernel(c_ref, w_ref, b_ref, o_ref):
    o_ref[...] = jnp.dot(c_ref[...].astype(BF16), w_ref[...].astype(BF16),
                         preferred_element_type=F32) + b_ref[...]


def _ada(c, w_ada, b_ada):
    rows = c.shape[0]
    return pl.pallas_call(
        _ada_kernel,
        grid=(6,),
        in_specs=[pl.BlockSpec((rows, D_MODEL), lambda j: (0, 0)),
                  pl.BlockSpec((D_MODEL, D_MODEL), lambda j: (0, j)),
                  pl.BlockSpec((1, D_MODEL), lambda j: (0, j))],
        out_specs=pl.BlockSpec((rows, D_MODEL), lambda j: (0, j)),
        out_shape=jax.ShapeDtypeStruct((rows, 6 * D_MODEL), F32),
        compiler_params=_params(1),
        name="ada",
    )(c, w_ada, b_ada.reshape(1, -1))


def _mod_spec(per_token, tm, tiles_per_seq):
    if per_token:
        return pl.BlockSpec((tm, 6 * D_MODEL), lambda i: (i, 0))
    return pl.BlockSpec((1, 1, 6 * D_MODEL), lambda i: (i // tiles_per_seq, 0, 0))


def _load_mod(mod_ref, per_token):
    return mod_ref[...] if per_token else mod_ref[0]


def _inproj_common(x_ref, mod_ref, g_ref, w_ref, u_ref, sg_ref, per_token):
    mod = _load_mod(mod_ref, per_token)
    sh1, sc1 = mod[:, 0:D_MODEL], mod[:, D_MODEL:2 * D_MODEL]
    h = (_rms(x_ref[...], g_ref[...]) * (1.0 + sc1) + sh1).astype(BF16)

    def proj(lo, hi):
        return jnp.dot(h, w_ref[:, lo:hi], preferred_element_type=F32)

    o = 3 * D_ATTN
    u_ref[...] = proj(o, o + D_CONV) * _sigmoid(proj(o + D_CONV, o + 2 * D_CONV))
    o += 2 * D_CONV
    sg_ref[...] = _sigmoid(proj(o, o + 2 * D_MODEL))
    return h, proj


def _inproj_sample_kernel(x_ref, mod_ref, g_ref, w_ref, q_ref, k_ref, v_ref, u_ref, sg_ref):
    _, proj = _inproj_common(x_ref, mod_ref, g_ref, w_ref, u_ref, sg_ref, True)
    q_ref[...] = proj(0, D_ATTN)
    k_ref[...] = proj(D_ATTN, 2 * D_ATTN)
    v_ref[...] = proj(2 * D_ATTN, 3 * D_ATTN)


def _inproj_prompt_kernel(x_ref, mod_ref, g_ref, w_ref, wt_ref, q_ref, kb_ref, kt_ref, vt_ref, vtb_ref,
                          u_ref, sg_ref, km_ref):
    h, proj = _inproj_common(x_ref, mod_ref, g_ref, w_ref, u_ref, sg_ref, False)
    q_ref[...] = proj(0, D_ATTN)
    k = proj(D_ATTN, 2 * D_ATTN)
    kb_ref[...] = k.astype(BF16)
    km_ref[0] = jnp.mean(k, axis=0, keepdims=True)
    kt_ref[0] = lax.dot_general(wt_ref[0:D_ATTN, :], h, NT_DIMS, preferred_element_type=F32)
    vt = lax.dot_general(wt_ref[D_ATTN:2 * D_ATTN, :], h, NT_DIMS, preferred_element_type=F32)
    vt_ref[0] = vt
    vtb_ref[0] = vt.astype(BF16)


def _inproj(x, mod, norm_g, w_in_bf, w_kvt_bf=None, *, per_token, tm, tiles_per_seq):
    n = x.shape[0]
    row = lambda i: (i, 0)
    in_specs = [pl.BlockSpec((tm, D_MODEL), row), _mod_spec(per_token, tm, tiles_per_seq),
                _const_spec((1, D_MODEL)), _const_spec((D_MODEL, D_IN))]
    tail = [(D_CONV, F32), (2 * D_MODEL, F32)]
    if per_token:
        kernel, args = _inproj_sample_kernel, (x, mod, norm_g, w_in_bf)
        outs = [(D_ATTN, F32)] * 3 + tail
        out_shape = [jax.ShapeDtypeStruct((n, w), dt) for w, dt in outs]
        out_specs = [pl.BlockSpec((tm, w), row) for w, _ in outs]
    else:
        assert tm == MOBA_BLOCK
        kernel, args = _inproj_prompt_kernel, (x, mod, norm_g, w_in_bf, w_kvt_bf)
        in_specs.append(_const_spec((2 * D_ATTN, D_MODEL)))
        batch = n // (tm * tiles_per_seq)
        seq = tm * tiles_per_seq
        tok = lambda w, dt: (jax.ShapeDtypeStruct((n, w), dt), pl.BlockSpec((tm, w), row))
        tr = lambda dt: (jax.ShapeDtypeStruct((batch, D_ATTN, seq), dt),
                         pl.BlockSpec((1, D_ATTN, tm), lambda i: (i // tiles_per_seq, 0, i % tiles_per_seq)))
        km = (jax.ShapeDtypeStruct((n // tm, 1, D_ATTN), F32), pl.BlockSpec((1, 1, D_ATTN), lambda i: (i, 0, 0)))
        pairs = [tok(D_ATTN, F32), tok(D_ATTN, BF16), tr(F32), tr(F32), tr(BF16)] + [tok(*t) for t in tail] + [km]
        out_shape, out_specs = [p[0] for p in pairs], [p[1] for p in pairs]
    return pl.pallas_call(
        kernel,
        grid=(n // tm,),
        in_specs=in_specs,
        out_specs=out_specs,
        out_shape=out_shape,
        compiler_params=_params(1),
        name="inproj",
    )(*args)


def _select_top(g, idx, n_pick, n_idx, axis):
    sel = jnp.zeros(g.shape, F32)
    for _ in range(n_pick):
        mx = jnp.max(g, axis=axis, keepdims=True)
        first = jnp.min(jnp.where(g == mx, idx, float(n_idx)), axis=axis, keepdims=True)
        hit = idx == first
        sel = jnp.where(hit & (mx > NEG_INF), 1.0, sel)
        g = jnp.where(hit, NEG_INF, g)
    return sel


def _attn_kernel(slopes_ref, q_ref, k_ref, vt_ref, km_ref, o_ref, bias_ref, sel_ref, sa_ref, sb_ref):
    pair, own = pl.program_id(1), pl.program_id(2)
    blk = MOBA_BLOCK
    n_blk = km_ref.shape[1]
    key = lax.broadcasted_iota(I32, (blk, blk), 0)
    qry = lax.broadcasted_iota(I32, (blk, blk), 1)

    @pl.when(own == 0)
    def _():
        rel = (qry - key).astype(F32)
        for hh in range(HEADS_PER_TILE):
            bias_ref[hh] = -(slopes_ref[pair * HEADS_PER_TILE + hh] * LOG2E) * rel

    q = q_ref[...]
    lane = lax.broadcasted_iota(I32, (1, LANES), 1)
    km = km_ref[0]
    bidx = lax.broadcasted_iota(I32, (n_blk, blk), 0)
    bidx_f = bidx.astype(F32)
    own_off = pl.multiple_of(own * blk, blk)
    k_own = k_ref[pl.ds(own_off, blk), :]
    vt_own = vt_ref[0, :, pl.ds(own_off, blk)]

    qhs = [jnp.where((lane // HEAD_DIM) == hh, q, 0.0) for hh in range(HEADS_PER_TILE)]
    qs = [(qh * (HEAD_DIM ** -0.5 * LOG2E)).astype(BF16) for qh in qhs]
    span = KV_GROUP * blk

    def score_stage(dst_ref, n0):
        n0 = jnp.minimum(n0, n_blk - KV_GROUP)
        kb = k_ref[pl.ds(pl.multiple_of(n0 * blk, span), span), :]
        for hh in range(HEADS_PER_TILE):
            dst_ref[hh] = lax.dot_general(kb, qs[hh], NT_DIMS, preferred_element_type=F32)

    ms, ls, pvs = [], [], []
    for hh in range(HEADS_PER_TILE):
        gate = lax.dot_general(km, qhs[hh], NT_DIMS, precision=lax.Precision.HIGHEST,
                               preferred_element_type=F32)
        gate = jnp.where(bidx < own, gate, NEG_INF)
        sel_ref[hh] = _select_top(gate, bidx_f, MOBA_TOPK, n_blk, 0)
        s = lax.dot_general(k_own, qs[hh], NT_DIMS, preferred_element_type=F32) + bias_ref[hh]
        s = jnp.where(key <= qry, s, NEG_INF)
        m = jnp.max(s, axis=0, keepdims=True)
        p = jnp.exp2(s - m)
        ms.append(m)
        ls.append(jnp.sum(p, axis=0, keepdims=True))
        pvs.append(jnp.dot(vt_own[hh * HEAD_DIM:(hh + 1) * HEAD_DIM, :], p.astype(BF16),
                           preferred_element_type=F32))

    def softmax_stage(src_ref, n0, carry):
        vtb = vt_ref[0, :, pl.ds(pl.multiple_of(n0 * blk, span), span)]
        new = []
        for hh in range(HEADS_PER_TILE):
            m, l, acc = carry[3 * hh:3 * hh + 3]
            slope = slopes_ref[pair * HEADS_PER_TILE + hh] * LOG2E
            bias = bias_ref[hh]
            subs, m_new = [], m
            for j in range(KV_GROUP):
                s = src_ref[hh, j * blk:(j + 1) * blk, :] + bias
                c = -slope * ((own - n0 - j) * blk).astype(F32)
                picked = sel_ref[hh, pl.ds(n0 + j, 1), :] > 0.0
                m_new = jnp.maximum(m_new, jnp.where(picked, jnp.max(s, axis=0, keepdims=True) + c, NEG_INF))
                subs.append((s, c, picked))
            alpha = jnp.exp2(m - m_new)
            l = alpha * l
            probs = []
            for s, c, picked in subs:
                p = jnp.exp2(s - jnp.where(picked, m_new - c, float("inf")))
                l = l + jnp.sum(p, axis=0, keepdims=True)
                probs.append(p.astype(BF16))
            pv = jnp.dot(vtb[hh * HEAD_DIM:(hh + 1) * HEAD_DIM, :], jnp.concatenate(probs, axis=0),
                         preferred_element_type=F32)
            new += [m_new, l, acc * alpha + pv]
        return tuple(new)

    score_stage(sa_ref, 0)

    def body(it, carry):
        n0 = it * (2 * KV_GROUP)
        score_stage(sb_ref, n0 + KV_GROUP)
        carry = softmax_stage(sa_ref, n0, carry)
        score_stage(sa_ref, n0 + 2 * KV_GROUP)
        return softmax_stage(sb_ref, n0 + KV_GROUP, carry)

    init = (ms[0], ls[0], pvs[0], ms[1], ls[1], pvs[1])
    n_trips = (own + 2 * KV_GROUP - 1) // (2 * KV_GROUP)
    _, l0, acc0, _, l1, acc1 = lax.fori_loop(0, n_trips, body, init)
    out_t = jnp.concatenate([acc0 / l0, acc1 / l1], axis=0)
    o_ref[...] = out_t.T.astype(o_ref.dtype)


def _attend_prompt(q, k_bf, vt_bf, kmean, slopes, batch, seq):
    n_qb = seq // MOBA_BLOCK
    assert n_qb % (2 * KV_GROUP) == 0
    score_buf = pltpu.VMEM((HEADS_PER_TILE, KV_GROUP * MOBA_BLOCK, MOBA_BLOCK), F32)
    tile = lambda b, p, i: (b * n_qb + i, p)
    return pl.pallas_call(
        _attn_kernel,
        grid=(batch, N_HEADS // HEADS_PER_TILE, n_qb),
        in_specs=[pl.BlockSpec(memory_space=pltpu.SMEM),
                  pl.BlockSpec((MOBA_BLOCK, LANES), tile),
                  pl.BlockSpec((seq, LANES), lambda b, p, i: (b, p)),
                  pl.BlockSpec((1, LANES, seq), lambda b, p, i: (b, p, 0)),
                  pl.BlockSpec((1, n_qb, LANES), lambda b, p, i: (b, 0, p))],
        out_specs=pl.BlockSpec((MOBA_BLOCK, LANES), tile),
        out_shape=jax.ShapeDtypeStruct((batch * seq, D_ATTN), BF16),
        scratch_shapes=[pltpu.VMEM((HEADS_PER_TILE, MOBA_BLOCK, MOBA_BLOCK), F32),
                        pltpu.VMEM((HEADS_PER_TILE, n_qb, MOBA_BLOCK), F32), score_buf, score_buf],
        compiler_params=_params(3),
        name="attn_prompt",
    )(slopes, q, k_bf, vt_bf, kmean.reshape(batch, n_qb, D_ATTN))


def _decode_attn_kernel(pt_ref, qt_ref, knt_ref, vnt_ref, slope_ref, ck_ref, cv_ref, o_ref, kbuf, vbuf, sem,
                        *, n_pages):
    pages_per_blk = MOBA_BLOCK // PAGE_SIZE
    n_blk = n_pages // pages_per_blk
    past_len = n_pages * PAGE_SIZE
    req = pl.program_id(0)
    slot = req % 2
    mine = lax.broadcasted_iota(I32, (D_ATTN, LANES), 1) == req

    def page_copies(r, buf_slot):
        return [pltpu.make_async_copy(cache.at[pt_ref[r * n_pages + p]], buf.at[buf_slot, p], sem.at[buf_slot])
                for cache, buf in ((ck_ref, kbuf), (cv_ref, vbuf)) for p in range(n_pages)]

    @pl.when(req == 0)
    def _():
        o_ref[...] = jnp.zeros_like(o_ref)
        for c in page_copies(0, 0):
            c.start()

    @pl.when(req + 1 < pl.num_programs(0))
    def _():
        for c in page_copies(req + 1, 1 - slot):
            c.start()

    for c in page_copies(req, slot):
        c.wait()
    k_pages = [kbuf.at[slot, p] for p in range(n_pages)]
    v_pages = [vbuf.at[slot, p] for p in range(n_pages)]

    def column(ref):
        return jnp.sum(jnp.where(mine, ref[...], 0.0), axis=1, keepdims=True)

    def head_sums(x):
        head = lax.broadcasted_iota(I32, (N_HEADS, x.shape[1]), 0)
        out = jnp.zeros((N_HEADS, x.shape[1]), F32)
        for h in range(N_HEADS):
            part = jnp.sum(x[h * HEAD_DIM:(h + 1) * HEAD_DIM], axis=0, keepdims=True)
            out = jnp.where(head == h, part, out)
        return out

    def head_rows(x):
        return jnp.concatenate([jnp.broadcast_to(x[h:h + 1], (HEAD_DIM, x.shape[1]))
                                for h in range(N_HEADS)], axis=0)

    q, k_new, v_new = column(qt_ref), column(knt_ref), column(vnt_ref)
    raw = [head_sums(k_pages[p][...] * q) for p in range(n_pages)]

    gates = []
    for n in range(n_blk):
        tot = raw[n * pages_per_blk]
        for j in range(1, pages_per_blk):
            tot = tot + raw[n * pages_per_blk + j]
        gates.append(jnp.sum(tot, axis=1, keepdims=True) * (1.0 / MOBA_BLOCK))
    picked = []
    for n in range(n_blk):
        beaten = jnp.zeros_like(gates[n])
        for o in range(n_blk):
            if o != n:
                wins = (gates[o] >= gates[n]) if o < n else (gates[o] > gates[n])
                beaten = beaten + wins.astype(F32)
        picked.append(beaten < float(MOBA_TOPK))

    lane = lax.broadcasted_iota(I32, (N_HEADS, LANES), 1)
    slope = slope_ref[...]
    scale = HEAD_DIM ** -0.5
    scores = []
    for p in range(n_pages):
        dist = (past_len - p * PAGE_SIZE - lane).astype(F32)
        scores.append(jnp.where(picked[p // pages_per_blk], raw[p] * scale - slope * dist, NEG_INF))
    s_new = head_sums(q * k_new) * scale

    top = scores[0]
    for s in scores[1:]:
        top = jnp.maximum(top, s)
    m = jnp.maximum(s_new, jnp.max(top, axis=1, keepdims=True))
    p_new = jnp.exp(s_new - m)
    tot_prob = jnp.zeros((N_HEADS, PAGE_SIZE), F32)
    acc = jnp.zeros((D_ATTN, PAGE_SIZE), F32)
    for p in range(n_pages):
        prob = jnp.exp(scores[p] - m)
        tot_prob = tot_prob + prob
        acc = acc + v_pages[p][...] * head_rows(prob)
    l = p_new + jnp.sum(tot_prob, axis=1, keepdims=True)
    out = (jnp.sum(acc, axis=1, keepdims=True) + head_rows(p_new) * v_new) / head_rows(l)
    o_ref[...] = jnp.where(mine, out, o_ref[...])


def _attend_sample(q, k_new, v_new, cache_k, cache_v, page_table, slopes_rows):
    n_req, n_pages = page_table.shape
    assert n_req == LANES
    n_phys = cache_k.shape[0]
    as_tiles = lambda c: c.transpose(0, 2, 3, 1).reshape(n_phys, D_ATTN, PAGE_SIZE)
    ck, cv = as_tiles(cache_k), as_tiles(cache_v)
    cols = pl.BlockSpec((D_ATTN, n_req), lambda r, pt: (0, 0))
    anywhere = pl.BlockSpec(memory_space=pl.ANY)
    page_buf = pltpu.VMEM((2, n_pages, D_ATTN, PAGE_SIZE), F32)
    grid_spec = pltpu.PrefetchScalarGridSpec(
        num_scalar_prefetch=1,
        grid=(n_req,),
        in_specs=[cols, cols, cols, pl.BlockSpec((N_HEADS, LANES), lambda r, pt: (0, 0)), anywhere, anywhere],
        out_specs=cols,
        scratch_shapes=[page_buf, page_buf, pltpu.SemaphoreType.DMA((2,))],
    )
    out_t = pl.pallas_call(
        functools.partial(_decode_attn_kernel, n_pages=n_pages),
        grid_spec=grid_spec,
        out_shape=jax.ShapeDtypeStruct((D_ATTN, n_req), F32),
        compiler_params=_params(1),
        name="attn_sample",
    )(page_table.reshape(-1), q.T, k_new.T, v_new.T, slopes_rows, ck, cv)
    return out_t.T


def _onehots(e, lane):
    return [(e[:, kk:kk + 1] == lane) for kk in range(TOP_K)]


def _expert_ranks(e, carry_ref):
    tm = e.shape[0]
    lane = lax.broadcasted_iota(I32, (tm, LANES), 1)
    hots = _onehots(e, lane)
    hot = jnp.zeros((tm, LANES), F32)
    for h in hots:
        hot = hot + h.astype(F32)
    r = lax.broadcasted_iota(I32, (tm, tm), 0)
    c = lax.broadcasted_iota(I32, (tm, tm), 1)
    earlier = (c < r).astype(BF16)
    before = carry_ref[...] + jnp.dot(earlier, hot.astype(BF16), preferred_element_type=F32)
    rank = jnp.zeros((tm, LANES), F32)
    for kk, h in enumerate(hots):
        rank = jnp.where(lane == kk, jnp.sum(jnp.where(h, before, 0.0), axis=1, keepdims=True), rank)
    carry_ref[...] = carry_ref[...] + jnp.sum(hot, axis=0, keepdims=True)
    return rank.astype(I32)


def _branch_merge(y_conv, x, attn_bf, sg, mod, w, outs, carry_ref):
    (b_dw, ln_g, ln_b, w_ao, w_pw2, w_out, n2g, w_router, b_router, count_in) = w
    xo_ref, h2_ref, eidx_ref, gate_ref, rank_ref, count_ref = outs

    @pl.when(pl.program_id(0) == 0)
    def _():
        carry_ref[...] = count_in[0:1, :]
    y = y_conv + b_dw[...]
    mu = jnp.mean(y, axis=-1, keepdims=True)
    yc = y - mu
    var = jnp.mean(yc * yc, axis=-1, keepdims=True)
    y = (yc * lax.rsqrt(var + EPS)) * ln_g[...] + ln_b[...]
    y = y * _sigmoid(y)
    c_out = jnp.dot(y.astype(BF16), w_pw2[...], preferred_element_type=F32)
    a_out = jnp.dot(attn_bf, w_ao[...], preferred_element_type=F32)
    merged = sg[:, 0:D_MODEL] * a_out + sg[:, D_MODEL:2 * D_MODEL] * c_out
    g1 = mod[:, 2 * D_MODEL:3 * D_MODEL]
    xo = x + g1 * jnp.dot(merged.astype(BF16), w_out[...], preferred_element_type=F32)
    xo_ref[...] = xo
    sh2, sc2 = mod[:, 3 * D_MODEL:4 * D_MODEL], mod[:, 4 * D_MODEL:5 * D_MODEL]
    h2 = _rms(xo, n2g[...]) * (1.0 + sc2) + sh2
    _rows_to_tiles(h2_ref, h2)

    h2_hi = h2.astype(BF16)
    h2_lo = (h2 - h2_hi.astype(F32)).astype(BF16)
    logits = (jnp.dot(h2_hi, w_router[0], preferred_element_type=F32)
              + jnp.dot(h2_lo, w_router[0], preferred_element_type=F32)
              + jnp.dot(h2_hi, w_router[1], preferred_element_type=F32)) + b_router[...]
    tm = logits.shape[0]
    lane = lax.broadcasted_iota(I32, (tm, LANES), 1)
    logits = jnp.where(lane < N_EXPERTS, logits, NEG_INF)
    eidx = lane.astype(F32)
    vals, e_out = [], jnp.zeros((tm, LANES), F32)
    for kk in range(TOP_K):
        mx = jnp.max(logits, axis=1, keepdims=True)
        first = jnp.min(jnp.where(logits == mx, eidx, float(N_EXPERTS)), axis=1, keepdims=True)
        logits = jnp.where(eidx == first, NEG_INF, logits)
        vals.append(mx)
        e_out = jnp.where(lane == kk, first, e_out)
    ex = [jnp.exp(v - vals[0]) for v in vals]
    den = ex[0] + ex[1] + ex[2] + ex[3]
    g_out = jnp.zeros((tm, LANES), F32)
    for kk in range(TOP_K):
        g_out = jnp.where(lane == kk, ex[kk] / den, g_out)
    experts = e_out.astype(I32)
    eidx_ref[...] = experts
    gate_ref[...] = g_out
    rank_ref[...] = _expert_ranks(experts, carry_ref)
    count_ref[...] = jnp.broadcast_to(carry_ref[...], count_ref.shape)


def _merge_prompt_kernel(x_ref, attn_ref, u_ref, halo_ref, sg_ref, mod_ref, wdw_ref, *refs, tiles_per_seq):
    w, outs, (full_ref, shifted_ref, carry_ref) = refs[:10], refs[10:16], refs[16:]
    tm = u_ref.shape[0]
    first = (pl.program_id(0) % tiles_per_seq) == 0
    full_ref[0:HALO, :] = jnp.where(first, 0.0, halo_ref[...])
    full_ref[HALO:HALO + tm, :] = u_ref[...]
    base = HALO - (CONV_WIDTH - 1)
    y = jnp.zeros((tm, D_CONV), F32)
    for phase in range(SUBLANES):
        taps = [j for j in range(CONV_WIDTH) if (base + j) % SUBLANES == phase]
        rows = max(base + j for j in taps) - phase + tm
        shifted_ref[phase, 0:rows, :] = full_ref[phase:phase + rows, :]
        for j in taps:
            lo = base + j - phase
            y = y + wdw_ref[j:j + 1, :] * shifted_ref[phase, lo:lo + tm, :]
    _branch_merge(y, x_ref[...], attn_ref[...], sg_ref[...], mod_ref[0], w, outs, carry_ref)


def _merge_sample_kernel(x_ref, attn_ref, u_ref, state_ref, sg_ref, mod_ref, wdw_ref, *refs):
    w, outs, (carry_ref,) = refs[:10], refs[10:16], refs[16:]
    y = wdw_ref[CONV_WIDTH - 1:CONV_WIDTH, :] * u_ref[...]
    for j in range(CONV_WIDTH - 1):
        y = y + wdw_ref[j:j + 1, :] * state_ref[j]
    _branch_merge(y, x_ref[...], attn_ref[...].astype(BF16), sg_ref[...], mod_ref[...], w, outs, carry_ref)


def _merge(x, attn, u, hist, sg, mod, weights, *, per_token, tm, tiles_per_seq):
    n = x.shape[0]
    row = lambda i: (i, 0)
    if per_token:
        kernel = _merge_sample_kernel
        hist_spec = _const_spec(hist.shape)
        scratch = []
    else:
        kernel = functools.partial(_merge_prompt_kernel, tiles_per_seq=tiles_per_seq)
        hist_spec = pl.BlockSpec((HALO, D_CONV), lambda i: (jnp.maximum(i * (tm // HALO) - 1, 0), 0))
        scratch = [pltpu.VMEM((HALO + tm, D_CONV), F32), pltpu.VMEM((SUBLANES, HALO + tm, D_CONV), F32)]
    scratch.append(pltpu.VMEM((1, LANES), F32))
    flat = lambda w, dt: (jax.ShapeDtypeStruct((n, w), dt), pl.BlockSpec((tm, w), row))
    pairs = [flat(D_MODEL, F32),
             (jax.ShapeDtypeStruct(_tiled(n), F32), pl.BlockSpec(_tiled(tm), row)),
             flat(LANES, I32), flat(LANES, F32), flat(LANES, I32),
             (jax.ShapeDtypeStruct((8, LANES), F32), pl.BlockSpec((8, LANES), lambda i: (0, 0)))]
    return pl.pallas_call(
        kernel,
        grid=(n // tm,),
        in_specs=[pl.BlockSpec((tm, D_MODEL), row), pl.BlockSpec((tm, D_ATTN), row),
                  pl.BlockSpec((tm, D_CONV), row), hist_spec, pl.BlockSpec((tm, 2 * D_MODEL), row),
                  _mod_spec(per_token, tm, tiles_per_seq)] + [_const_spec(a.shape) for a in weights],
        out_specs=[p[1] for p in pairs],
        out_shape=[p[0] for p in pairs],
        scratch_shapes=scratch,
        compiler_params=_params(1),
        name="merge",
    )(x, attn, u, hist, sg, mod, *weights)


def _slot_kernel(e_ref, rank_ref, count_ref, dest_ref, blk_ref, info_ref, *, n_blocks_pad):
    tm = e_ref.shape[0]
    counts = count_ref[0:1, :].astype(I32)
    padded = ((counts + (EXPERT_BLOCK - 1)) // EXPERT_BLOCK) * EXPERT_BLOCK
    r = lax.broadcasted_iota(I32, (LANES, LANES), 0)
    c = lax.broadcasted_iota(I32, (LANES, LANES), 1)
    pstart = jnp.dot(jnp.broadcast_to(padded.astype(F32), (8, LANES)), (r < c).astype(F32),
                     precision=lax.Precision.HIGHEST, preferred_element_type=F32)[0:1]
    pend = pstart + padded.astype(F32)
    lane = lax.broadcasted_iota(I32, (tm, LANES), 1)
    e = e_ref[...]
    dest = rank_ref[...]
    for kk, h in enumerate(_onehots(e, lane)):
        off = jnp.sum(jnp.where(h, pstart, 0.0), axis=1, keepdims=True).astype(I32)
        dest = dest + jnp.where(lane == kk, off, 0)
    dest_ref[...] = dest

    lane1 = lax.broadcasted_iota(I32, (1, LANES), 1)
    used = jnp.max(pend, axis=1, keepdims=True)
    blk_start = (lax.broadcasted_iota(I32, (n_blocks_pad, 1), 0) * EXPERT_BLOCK).astype(F32)
    blk_start = jnp.minimum(blk_start, used - EXPERT_BLOCK)
    done = jnp.where((lane1 < N_EXPERTS) & (pend <= blk_start), 1.0, 0.0)
    blk_exp = jnp.minimum(jnp.sum(done, axis=1, keepdims=True), N_EXPERTS - 1.0)
    blk_ref[...] = jnp.broadcast_to(blk_exp, blk_ref.shape).astype(I32)
    row8 = lax.broadcasted_iota(I32, (8, LANES), 0)
    info = jnp.where(row8 == 0, pstart, jnp.where(row8 == 1, pend, jnp.where(row8 == 2, used, 0.0)))
    info_ref[...] = info.astype(I32)


def _slots(eidx, rank, counts, tm, n_blocks):
    n = eidx.shape[0]
    row = lambda i: (i, 0)
    fixed = lambda i: (0, 0)
    n_blocks_pad = -(-n_blocks // 8) * 8
    dest, blk, info = pl.pallas_call(
        functools.partial(_slot_kernel, n_blocks_pad=n_blocks_pad),
        grid=(n // tm,),
        in_specs=[pl.BlockSpec((tm, LANES), row), pl.BlockSpec((tm, LANES), row),
                  pl.BlockSpec((8, LANES), fixed)],
        out_specs=[pl.BlockSpec((tm, LANES), row), pl.BlockSpec((n_blocks_pad, LANES), fixed),
                   pl.BlockSpec((8, LANES), fixed)],
        out_shape=[jax.ShapeDtypeStruct((n, LANES), I32), jax.ShapeDtypeStruct((n_blocks_pad, LANES), I32),
                   jax.ShapeDtypeStruct((8, LANES), I32)],
        compiler_params=_params(1),
        name="moe_slot",
    )(eidx, rank, counts)
    dest_flat = dest[:, :TOP_K].reshape(-1)
    blk_exp = blk[:n_blocks, 0]
    pend = info[1, :N_EXPERTS]
    n_used = jnp.right_shift(info[2, 0:1], EXPERT_BLOCK.bit_length() - 1)
    return dest_flat, blk_exp, pend, n_used


def _for_each_row(n_rows, fn):
    @pl.loop(0, n_rows // SUBLANES)
    def _(g):
        first = pl.multiple_of(g * SUBLANES, SUBLANES)
        for s in range(SUBLANES):
            fn(first + s)


def _row_copy(src_ref, src_row, dst_ref, dst_row, sem):
    tile = lambda row: pl.ds(pl.multiple_of(row * ROW_PITCH, ROW_PITCH), ROW_PITCH)
    return pltpu.make_async_copy(src_ref.at[tile(src_row)], dst_ref.at[tile(dst_row)], sem)


def _dispatch_kernel(dest_ref, pend_ref, h_ref, tail_ref, xs_ref, zero_ref, sem):
    @pl.when(pl.program_id(0) == 0)
    def _():
        zero_ref[...] = jnp.zeros_like(zero_ref)
        block_rows = zero_ref.shape[0]
        n_blocks = xs_ref.shape[0] // block_rows
        first_unused = pend_ref[N_EXPERTS - 1] // EXPERT_BLOCK

        def fill(start):
            rows = pl.ds(pl.multiple_of(start * ROW_PITCH, block_rows), block_rows)
            return pltpu.make_async_copy(zero_ref, xs_ref.at[rows], sem)

        def last_block(e):
            return jnp.maximum(pend_ref[e] - EXPERT_BLOCK, 0)

        @pl.loop(0, N_EXPERTS)
        def _(e):
            fill(last_block(e)).start()

        @pl.loop(first_unused, n_blocks)
        def _(j):
            fill(j * EXPERT_BLOCK).start()

        @pl.loop(0, N_EXPERTS)
        def _(e):
            fill(last_block(e)).wait()

        @pl.loop(first_unused, n_blocks)
        def _(j):
            fill(j * EXPERT_BLOCK).wait()

    def scatter(h_ref):
        def copies(t):
            return [_row_copy(h_ref, t, xs_ref, dest_ref[t * TOP_K + kk], sem) for kk in range(TOP_K)]

        n_rows = h_ref.shape[0] // ROW_PITCH
        _for_each_row(n_rows, lambda t: [c.start() for c in copies(t)])
        _for_each_row(n_rows, lambda t: [c.wait() for c in copies(t)])

    is_tail = pl.program_id(0) == pl.num_programs(0) - 1

    @pl.when(jnp.logical_not(is_tail))
    def _():
        scatter(h_ref)

    @pl.when(is_tail)
    def _():
        scatter(tail_ref)


def _dispatch(h2, h2_tail, dest_flat, pend, tm, cap):
    n_tiles = h2.shape[0] // _tiled(tm)[0]
    assert h2_tail.shape[0] <= _tiled(tm)[0] and dest_flat.shape[0] == (n_tiles + 1) * tm * TOP_K
    return pl.pallas_call(
        _dispatch_kernel,
        grid=(n_tiles + 1,),
        in_specs=[pl.BlockSpec((tm * TOP_K,), lambda i: (i,), memory_space=pltpu.SMEM),
                  pl.BlockSpec(memory_space=pltpu.SMEM),
                  pl.BlockSpec(_tiled(tm), lambda i: (jnp.minimum(i, n_tiles - 1), 0)),
                  _const_spec(h2_tail.shape)],
        out_specs=pl.BlockSpec(memory_space=pl.ANY),
        out_shape=jax.ShapeDtypeStruct(_tiled(cap), F32),
        scratch_shapes=[pltpu.VMEM(_tiled(EXPERT_BLOCK), F32), pltpu.SemaphoreType.DMA(())],
        compiler_params=_params(1, has_side_effects=True),
        name="moe_dispatch",
    )(dest_flat, pend, h2, h2_tail)


def _expert_kernel(blk_ref, used_ref, x_ref, wg_ref, wu_ref, wd_ref, y_ref, wg_bf, wu_bf, wd_bf):
    j = pl.program_id(0)
    active = j < used_ref[0]
    changed = (j == 0) | (blk_ref[j] != blk_ref[jnp.maximum(j - 1, 0)])

    @pl.when(active & changed)
    def _():
        wg_bf[...] = wg_ref[0].astype(BF16)
        wu_bf[...] = wu_ref[0].astype(BF16)
        wd_bf[...] = wd_ref[0].astype(BF16)

    @pl.when(active)
    def _():
        x = _tiles_to_rows(x_ref).astype(BF16)
        gt = jnp.minimum(jnp.dot(x, wg_bf[...], preferred_element_type=F32), SWIGLU_LIMIT)
        up = jnp.clip(jnp.dot(x, wu_bf[...], preferred_element_type=F32), -SWIGLU_LIMIT, SWIGLU_LIMIT)
        act = gt * _sigmoid(SWIGLU_ALPHA * gt) * (up + 1.0)
        _rows_to_tiles(y_ref, jnp.dot(act.astype(BF16), wd_bf[...], preferred_element_type=F32))

    @pl.when(jnp.logical_not(active))
    def _():
        y_ref[...] = jnp.zeros_like(y_ref)


def _experts(xs, blk_exp, n_used, w_gate, w_up, w_down):
    n_blocks = blk_exp.shape[0]
    d_ff = w_gate.shape[2]
    xrow = lambda j, blk, used: (jnp.minimum(j, used[0] - 1), 0)
    wsel = lambda j, blk, used: (blk[j], 0, 0)
    grid_spec = pltpu.PrefetchScalarGridSpec(
        num_scalar_prefetch=2,
        grid=(n_blocks,),
        in_specs=[pl.BlockSpec(_tiled(EXPERT_BLOCK), xrow),
                  pl.BlockSpec((1, D_MODEL, d_ff), wsel),
                  pl.BlockSpec((1, D_MODEL, d_ff), wsel),
                  pl.BlockSpec((1, d_ff, D_MODEL), wsel)],
        out_specs=pl.BlockSpec(_tiled(EXPERT_BLOCK), lambda j, blk, used: (j, 0)),
        scratch_shapes=[pltpu.VMEM((D_MODEL, d_ff), BF16), pltpu.VMEM((D_MODEL, d_ff), BF16),
                        pltpu.VMEM((d_ff, D_MODEL), BF16)],
    )
    return pl.pallas_call(
        _expert_kernel,
        grid_spec=grid_spec,
        out_shape=jax.ShapeDtypeStruct(xs.shape, F32),
        compiler_params=_params(1),
        name="moe_experts",
    )(blk_exp, n_used, xs, w_gate, w_up, w_down)


def _combine_kernel(dest_ref, y_ref, xo_ref, gate_ref, mod_ref, fg_ref, o_ref, buf_ref, sem, *, per_token):
    tm = xo_ref.shape[0]

    def copies(t):
        return [_row_copy(y_ref, dest_ref[t * TOP_K + kk], buf_ref.at[kk], t, sem) for kk in range(TOP_K)]

    _for_each_row(tm, lambda t: [c.start() for c in copies(t)])
    _for_each_row(tm, lambda t: [c.wait() for c in copies(t)])

    gates = gate_ref[...]
    moe = gates[:, 0:1] * _tiles_to_rows(buf_ref.at[0])
    for kk in range(1, TOP_K):
        moe = moe + gates[:, kk:kk + 1] * _tiles_to_rows(buf_ref.at[kk])
    g2 = _load_mod(mod_ref, per_token)[:, 5 * D_MODEL:6 * D_MODEL]
    o_ref[...] = _rms(xo_ref[...] + g2 * moe, fg_ref[...])


def _combine(y, dest_flat, xo, gates, mod, final_g, *, per_token, tm, tiles_per_seq):
    n = xo.shape[0]
    row = lambda i: (i, 0)
    return pl.pallas_call(
        functools.partial(_combine_kernel, per_token=per_token),
        grid=(n // tm,),
        in_specs=[pl.BlockSpec((tm * TOP_K,), lambda i: (i,), memory_space=pltpu.SMEM),
                  pl.BlockSpec(memory_space=pl.ANY),
                  pl.BlockSpec((tm, D_MODEL), row), pl.BlockSpec((tm, LANES), row),
                  _mod_spec(per_token, tm, tiles_per_seq), _const_spec((1, D_MODEL))],
        out_specs=pl.BlockSpec((tm, D_MODEL), row),
        out_shape=jax.ShapeDtypeStruct((n, D_MODEL), F32),
        scratch_shapes=[pltpu.VMEM((TOP_K,) + _tiled(tm), F32), pltpu.SemaphoreType.DMA(())],
        compiler_params=_params(1),
        name="moe_combine",
    )(dest_flat, y, xo, gates, mod, final_g)


def _moe(prompt, sample, counts, final_g, w_gate, w_up, w_down, *, tm, tiles_per_seq):
    n_p, n_s = prompt[1].shape[0], sample[1].shape[0]
    n_blocks = -(-(n_p + n_s) * TOP_K // EXPERT_BLOCK) + N_EXPERTS
    dest_p, blk_exp, pend, n_used = _slots(prompt[1], prompt[3], counts, tm, n_blocks)
    dest_s, _, _, _ = _slots(sample[1], sample[3], counts, n_s, n_blocks)
    dest_all = jnp.concatenate([dest_p, dest_s, jnp.zeros(((tm - n_s) * TOP_K,), I32)])
    xs = _dispatch(prompt[0], sample[0], dest_all, pend, tm, n_blocks * EXPERT_BLOCK)
    y = _experts(xs, blk_exp, n_used, w_gate, w_up, w_down)
    y_p = _combine(y, dest_p, prompt[4], prompt[2], prompt[5], final_g,
                   per_token=False, tm=tm, tiles_per_seq=tiles_per_seq)
    y_s = _combine(y, dest_s, sample[4], sample[2], sample[5], final_g,
                   per_token=True, tm=n_s, tiles_per_seq=1)
    return y_p, y_s


def kernel(x_prompt, x_sample, c_prompt, c_sample, cache_k, cache_v, state_conv, page_table, norm1_g, norm2_g,
           w_ada, b_ada, w_in, w_attn_out, w_dw, b_dw, ln_g, ln_b, w_pw2, w_out, w_router, b_router, w_gate,
           w_up, w_down, final_g):
    depth = w_in.shape[0]
    assert depth == 1, "single-layer trunk"
    batch, seq, _ = x_prompt.shape
    n_req = x_sample.shape[0]
    assert x_sample.shape[1] == 1 and seq % MOBA_BLOCK == 0 and n_req % 8 == 0
    layer = 0
    row2 = lambda a: a.reshape(1, -1)

    pad = (-batch) % 8
    c_all = jnp.concatenate([c_prompt, jnp.zeros((pad, D_MODEL), F32), c_sample], axis=0)
    ada = _ada(c_all, w_ada[layer], b_ada[layer])
    mod_p = ada[:batch].reshape(batch, 1, 6 * D_MODEL)
    mod_s = ada[batch + pad:]

    w_in_bf = w_in[layer].astype(BF16)
    merge_w = (row2(b_dw[layer]), row2(ln_g[layer]), row2(ln_b[layer]), w_attn_out[layer].astype(BF16),
               w_pw2[layer].astype(BF16), w_out[layer].astype(BF16), row2(norm2_g[layer]),
               _split_bf16(jnp.pad(w_router[layer], ((0, 0), (0, LANES - N_EXPERTS)))),
               jnp.pad(row2(b_router[layer]), ((0, 0), (0, LANES - N_EXPERTS))))
    w_dw_pad = jnp.pad(w_dw[layer], ((0, HALO - CONV_WIDTH), (0, 0)))
    slopes = jnp.exp2(-8.0 * jnp.arange(1, N_HEADS + 1, dtype=F32) / N_HEADS)
    slopes_rows = jnp.broadcast_to(slopes[:, None], (N_HEADS, LANES))
    fg = row2(final_g)
    experts = (w_gate[layer], w_up[layer], w_down[layer])

    tm = MOBA_BLOCK
    tps = seq // tm
    xp = x_prompt.reshape(batch * seq, D_MODEL)
    w_kvt_bf = w_in_bf[:, D_ATTN:3 * D_ATTN].T
    q, k_bf, kt, vt, vt_bf, u, sg, kmean = _inproj(xp, mod_p, row2(norm1_g[layer]), w_in_bf, w_kvt_bf,
                                                   per_token=False, tm=tm, tiles_per_seq=tps)
    attn = _attend_prompt(q, k_bf, vt_bf, kmean, slopes, batch, seq)
    no_counts = jnp.zeros((8, LANES), F32)
    xo, h2, eidx, gates, rank, counts_p = _merge(xp, attn, u, u, sg, mod_p, (w_dw_pad,) + merge_w + (no_counts,),
                                                 per_token=False, tm=tm, tiles_per_seq=tps)
    to_cache = lambda t: t.reshape(batch, N_HEADS, HEAD_DIM, seq).transpose(0, 3, 1, 2)[None]
    k_prompt, v_prompt = to_cache(kt), to_cache(vt)
    conv_prompt = u.reshape(batch, seq, D_CONV)[None, :, seq - (CONV_WIDTH - 1):]

    xs = x_sample.reshape(n_req, D_MODEL)
    qs, ks, vs, us, sgs = _inproj(xs, mod_s, row2(norm1_g[layer]), w_in_bf,
                                  per_token=True, tm=n_req, tiles_per_seq=1)
    attn_s = _attend_sample(qs, ks, vs, cache_k[layer], cache_v[layer], page_table, slopes_rows)
    state = state_conv[layer]
    xo_s, h2_s, eidx_s, gates_s, rank_s, counts = _merge(
        xs, attn_s, us, state.transpose(1, 0, 2), sgs, mod_s, (w_dw_pad,) + merge_w + (counts_p,),
        per_token=True, tm=n_req, tiles_per_seq=1)

    y_prompt, y_sample = _moe((h2, eidx, gates, rank, xo, mod_p), (h2_s, eidx_s, gates_s, rank_s, xo_s, mod_s),
                              counts, fg, *experts, tm=tm, tiles_per_seq=tps)
    k_sample = ks.reshape(1, n_req, 1, N_HEADS, HEAD_DIM)
    v_sample = vs.reshape(1, n_req, 1, N_HEADS, HEAD_DIM)
    conv_sample = jnp.concatenate([state[:, 1:], us[:, None, :]], axis=1)[None]

    return (y_prompt.reshape(batch, seq, D_MODEL), y_sample.reshape(n_req, 1, D_MODEL),
            k_prompt, v_prompt, conv_prompt, k_sample, v_sample, conv_sample)
```

```python
import functools

import jax
import jax.numpy as jnp
from jax import lax
from jax.experimental import pallas as pl
from jax.experimental.pallas import tpu as pltpu

F32 = jnp.float32
BF16 = jnp.bfloat16
I32 = jnp.int32

D_MODEL = 1024
N_HEADS = 8
HEAD_DIM = 64
D_ATTN = N_HEADS * HEAD_DIM
D_CONV = 512
CONV_WIDTH = 31
N_EXPERTS = 32
TOP_K = 4
MOBA_BLOCK = 256
MOBA_TOPK = 3
PAGE_SIZE = 128
SWIGLU_ALPHA = 1.702
SWIGLU_LIMIT = 7.0
EPS = 1e-6
D_IN = 3 * D_ATTN + 2 * D_CONV + 2 * D_MODEL

LANES = 128
SUBLANES = 8
HALO = 32
EXPERT_BLOCK = 256
SLOT_TILE = 1024
HEADS_PER_TILE = LANES // HEAD_DIM
KV_GROUP = 2
VMEM_LIMIT = 56 * 1024 * 1024

NT_DIMS = (((1,), (1,)), ((), ()))
NEG_INF = float("-inf")
LOG2E = 1.4426950408889634


def _params(n_axes, **kw):
    return pltpu.CompilerParams(dimension_semantics=("arbitrary",) * n_axes,
                                vmem_limit_bytes=VMEM_LIMIT, **kw)


def _const_spec(shape):
    nd = len(shape)
    return pl.BlockSpec(shape, lambda *_: (0,) * nd, pipeline_mode=pl.Buffered(1))


def _rms(x, g):
    return (x * lax.rsqrt(jnp.mean(x * x, axis=-1, keepdims=True) + EPS)) * g


def _sigmoid(x):
    return 1.0 / (1.0 + jnp.exp(-x))


ROW_PITCH = D_MODEL // LANES


def _tiled(n_rows):
    return (n_rows * ROW_PITCH, LANES)


def _rows_to_tiles(ref, x):
    for c in range(ROW_PITCH):
        ref[pl.ds(c, x.shape[0], stride=ROW_PITCH), :] = x[:, c * LANES:(c + 1) * LANES]


def _tiles_to_rows(ref):
    r = ref.shape[0] // ROW_PITCH
    return jnp.concatenate([ref[pl.ds(c, r, stride=ROW_PITCH), :] for c in range(ROW_PITCH)], axis=1)


def _split_bf16(w):
    hi = w.astype(BF16)
    return jnp.stack([hi, (w - hi.astype(F32)).astype(BF16)])


def _ada_kernel(c_ref, w_ref, b_ref, o_ref):
    o_ref[...] = jnp.dot(c_ref[...].astype(BF16), w_ref[...].astype(BF16),
                         preferred_element_type=F32) + b_ref[...]


def _ada(c, w_ada, b_ada):
    rows = c.shape[0]
    return pl.pallas_call(
        _ada_kernel,
        grid=(6,),
        in_specs=[pl.BlockSpec((rows, D_MODEL), lambda j: (0, 0)),
                  pl.BlockSpec((D_MODEL, D_MODEL), lambda j: (0, j)),
                  pl.BlockSpec((1, D_MODEL), lambda j: (0, j))],
        out_specs=pl.BlockSpec((rows, D_MODEL), lambda j: (0, j)),
        out_shape=jax.ShapeDtypeStruct((rows, 6 * D_MODEL), F32),
        compiler_params=_params(1),
        name="ada",
    )(c, w_ada, b_ada.reshape(1, -1))


def _mod_spec(per_token, tm, tiles_per_seq):
    if per_token:
        return pl.BlockSpec((tm, 6 * D_MODEL), lambda i: (i, 0))
    return pl.BlockSpec((1, 1, 6 * D_MODEL), lambda i: (i // tiles_per_seq, 0, 0))


def _load_mod(mod_ref, per_token):
    return mod_ref[...] if per_token else mod_ref[0]


def _inproj_common(x_ref, mod_ref, g_ref, w_ref, u_ref, sg_ref, per_token):
    mod = _load_mod(mod_ref, per_token)
    sh1, sc1 = mod[:, 0:D_MODEL], mod[:, D_MODEL:2 * D_MODEL]
    h = (_rms(x_ref[...], g_ref[...]) * (1.0 + sc1) + sh1).astype(BF16)

    def proj(lo, hi):
        return jnp.dot(h, w_ref[:, lo:hi], preferred_element_type=F32)

    o = 3 * D_ATTN
    u_ref[...] = proj(o, o + D_CONV) * _sigmoid(proj(o + D_CONV, o + 2 * D_CONV))
    o += 2 * D_CONV
    sg_ref[...] = _sigmoid(proj(o, o + 2 * D_MODEL))
    return h, proj


def _inproj_sample_kernel(x_ref, mod_ref, g_ref, w_ref, q_ref, k_ref, v_ref, u_ref, sg_ref):
    _, proj = _inproj_common(x_ref, mod_ref, g_ref, w_ref, u_ref, sg_ref, True)
    q_ref[...] = proj(0, D_ATTN)
    k_ref[...] = proj(D_ATTN, 2 * D_ATTN)
    v_ref[...] = proj(2 * D_ATTN, 3 * D_ATTN)


def _inproj_prompt_kernel(x_ref, mod_ref, g_ref, w_ref, wt_ref, q_ref, kb_ref, kt_ref, vt_ref, vtb_ref,
                          u_ref, sg_ref, km_ref):
    h, proj = _inproj_common(x_ref, mod_ref, g_ref, w_ref, u_ref, sg_ref, False)
    q_ref[...] = proj(0, D_ATTN)
    k = proj(D_ATTN, 2 * D_ATTN)
    kb_ref[...] = k.astype(BF16)
    km_ref[0] = jnp.mean(k, axis=0, keepdims=True)
    kt_ref[0] = lax.dot_general(wt_ref[0:D_ATTN, :], h, NT_DIMS, preferred_element_type=F32)
    vt = lax.dot_general(wt_ref[D_ATTN:2 * D_ATTN, :], h, NT_DIMS, preferred_element_type=F32)
    vt_ref[0] = vt
    vtb_ref[0] = vt.astype(BF16)


def _inproj(x, mod, norm_g, w_in_bf, w_kvt_bf=None, *, per_token, tm, tiles_per_seq):
    n = x.shape[0]
    row = lambda i: (i, 0)
    in_specs = [pl.BlockSpec((tm, D_MODEL), row), _mod_spec(per_token, tm, tiles_per_seq),
                _const_spec((1, D_MODEL)), _const_spec((D_MODEL, D_IN))]
    tail = [(D_CONV, F32), (2 * D_MODEL, F32)]
    if per_token:
        kernel, args = _inproj_sample_kernel, (x, mod, norm_g, w_in_bf)
        outs = [(D_ATTN, F32)] * 3 + tail
        out_shape = [jax.ShapeDtypeStruct((n, w), dt) for w, dt in outs]
        out_specs = [pl.BlockSpec((tm, w), row) for w, _ in outs]
    else:
        assert tm == MOBA_BLOCK
        kernel, args = _inproj_prompt_kernel, (x, mod, norm_g, w_in_bf, w_kvt_bf)
        in_specs.append(_const_spec((2 * D_ATTN, D_MODEL)))
        batch = n // (tm * tiles_per_seq)
        seq = tm * tiles_per_seq
        tok = lambda w, dt: (jax.ShapeDtypeStruct((n, w), dt), pl.BlockSpec((tm, w), row))
        tr = lambda dt: (jax.ShapeDtypeStruct((batch, D_ATTN, seq), dt),
                         pl.BlockSpec((1, D_ATTN, tm), lambda i: (i // tiles_per_seq, 0, i % tiles_per_seq)))
        km = (jax.ShapeDtypeStruct((n // tm, 1, D_ATTN), F32), pl.BlockSpec((1, 1, D_ATTN), lambda i: (i, 0, 0)))
        pairs = [tok(D_ATTN, F32), tok(D_ATTN, BF16), tr(F32), tr(F32), tr(BF16)] + [tok(*t) for t in tail] + [km]
        out_shape, out_specs = [p[0] for p in pairs], [p[1] for p in pairs]
    return pl.pallas_call(
        kernel,
        grid=(n // tm,),
        in_specs=in_specs,
        out_specs=out_specs,
        out_shape=out_shape,
        compiler_params=_params(1),
        name="inproj",
    )(*args)


def _select_top(g, idx, n_pick, n_idx, axis):
    sel = jnp.zeros(g.shape, F32)
    for _ in range(n_pick):
        mx = jnp.max(g, axis=axis, keepdims=True)
        first = jnp.min(jnp.where(g == mx, idx, float(n_idx)), axis=axis, keepdims=True)
        hit = idx == first
        sel = jnp.where(hit & (mx > NEG_INF), 1.0, sel)
        g = jnp.where(hit, NEG_INF, g)
    return sel


def _attn_kernel(slopes_ref, q_ref, k_ref, vt_ref, km_ref, o_ref, bias_ref, sel_ref, sa_ref, sb_ref):
    pair, own = pl.program_id(1), pl.program_id(2)
    blk = MOBA_BLOCK
    n_blk = km_ref.shape[1]
    key = lax.broadcasted_iota(I32, (blk, blk), 0)
    qry = lax.broadcasted_iota(I32, (blk, blk), 1)

    @pl.when(own == 0)
    def _():
        rel = (qry - key).astype(F32)
        for hh in range(HEADS_PER_TILE):
            bias_ref[hh] = -(slopes_ref[pair * HEADS_PER_TILE + hh] * LOG2E) * rel

    q = q_ref[...]
    lane = lax.broadcasted_iota(I32, (1, LANES), 1)
    km = km_ref[0]
    bidx = lax.broadcasted_iota(I32, (n_blk, blk), 0)
    bidx_f = bidx.astype(F32)
    own_off = pl.multiple_of(own * blk, blk)
    k_own = k_ref[pl.ds(own_off, blk), :]
    vt_own = vt_ref[0, :, pl.ds(own_off, blk)]

    qhs = [jnp.where((lane // HEAD_DIM) == hh, q, 0.0) for hh in range(HEADS_PER_TILE)]
    qs = [(qh * (HEAD_DIM ** -0.5 * LOG2E)).astype(BF16) for qh in qhs]
    span = KV_GROUP * blk

    def score_stage(dst_ref, n0):
        n0 = jnp.minimum(n0, n_blk - KV_GROUP)
        kb = k_ref[pl.ds(pl.multiple_of(n0 * blk, span), span), :]
        for hh in range(HEADS_PER_TILE):
            dst_ref[hh] = lax.dot_general(kb, qs[hh], NT_DIMS, preferred_element_type=F32)

    ms, ls, pvs = [], [], []
    for hh in range(HEADS_PER_TILE):
        gate = lax.dot_general(km, qhs[hh], NT_DIMS, precision=lax.Precision.HIGHEST,
                               preferred_element_type=F32)
        gate = jnp.where(bidx < own, gate, NEG_INF)
        sel_ref[hh] = _select_top(gate, bidx_f, MOBA_TOPK, n_blk, 0)
        s = lax.dot_general(k_own, qs[hh], NT_DIMS, preferred_element_type=F32) + bias_ref[hh]
        s = jnp.where(key <= qry, s, NEG_INF)
        m = jnp.max(s, axis=0, keepdims=True)
        p = jnp.exp2(s - m)
        ms.append(m)
        ls.append(jnp.sum(p, axis=0, keepdims=True))
        pvs.append(jnp.dot(vt_own[hh * HEAD_DIM:(hh + 1) * HEAD_DIM, :], p.astype(BF16),
                           preferred_element_type=F32))

    def softmax_stage(src_ref, n0, carry):
        vtb = vt_ref[0, :, pl.ds(pl.multiple_of(n0 * blk, span), span)]
        new = []
        for hh in range(HEADS_PER_TILE):
            m, l, acc = carry[3 * hh:3 * hh + 3]
            slope = slopes_ref[pair * HEADS_PER_TILE + hh] * LOG2E
            bias = bias_ref[hh]
            subs, m_new = [], m
            for j in range(KV_GROUP):
                s = src_ref[hh, j * blk:(j + 1) * blk, :] + bias
                c = -slope * ((own - n0 - j) * blk).astype(F32)
                picked = sel_ref[hh, pl.ds(n0 + j, 1), :] > 0.0
                m_new = jnp.maximum(m_new, jnp.where(picked, jnp.max(s, axis=0, keepdims=True) + c, NEG_INF))
                subs.append((s, c, picked))
            alpha = jnp.exp2(m - m_new)
            l = alpha * l
            probs = []
            for s, c, picked in subs:
                p = jnp.exp2(s - jnp.where(picked, m_new - c, float("inf")))
                l = l + jnp.sum(p, axis=0, keepdims=True)
                probs.append(p.astype(BF16))
            pv = jnp.dot(vtb[hh * HEAD_DIM:(hh + 1) * HEAD_DIM, :], jnp.concatenate(probs, axis=0),
                         preferred_element_type=F32)
            new += [m_new, l, acc * alpha + pv]
        return tuple(new)

    score_stage(sa_ref, 0)

    def body(it, carry):
        n0 = it * (2 * KV_GROUP)
        score_stage(sb_ref, n0 + KV_GROUP)
        carry = softmax_stage(sa_ref, n0, carry)
        score_stage(sa_ref, n0 + 2 * KV_GROUP)
        return softmax_stage(sb_ref, n0 + KV_GROUP, carry)

    init = (ms[0], ls[0], pvs[0], ms[1], ls[1], pvs[1])
    n_trips = (own + 2 * KV_GROUP - 1) // (2 * KV_GROUP)
    _, l0, acc0, _, l1, acc1 = lax.fori_loop(0, n_trips, body, init)
    out_t = jnp.concatenate([acc0 / l0, acc1 / l1], axis=0)
    o_ref[...] = out_t.T.astype(o_ref.dtype)


def _attend_prompt(q, k_bf, vt_bf, kmean, slopes, batch, seq):
    n_qb = seq // MOBA_BLOCK
    assert n_qb % (2 * KV_GROUP) == 0
    score_buf = pltpu.VMEM((HEADS_PER_TILE, KV_GROUP * MOBA_BLOCK, MOBA_BLOCK), F32)
    tile = lambda b, p, i: (b * n_qb + i, p)
    return pl.pallas_call(
        _attn_kernel,
        grid=(batch, N_HEADS // HEADS_PER_TILE, n_qb),
        in_specs=[pl.BlockSpec(memory_space=pltpu.SMEM),
                  pl.BlockSpec((MOBA_BLOCK, LANES), tile),
                  pl.BlockSpec((seq, LANES), lambda b, p, i: (b, p)),
                  pl.BlockSpec((1, LANES, seq), lambda b, p, i: (b, p, 0)),
                  pl.BlockSpec((1, n_qb, LANES), lambda b, p, i: (b, 0, p))],
        out_specs=pl.BlockSpec((MOBA_BLOCK, LANES), tile),
        out_shape=jax.ShapeDtypeStruct((batch * seq, D_ATTN), BF16),
        scratch_shapes=[pltpu.VMEM((HEADS_PER_TILE, MOBA_BLOCK, MOBA_BLOCK), F32),
                        pltpu.VMEM((HEADS_PER_TILE, n_qb, MOBA_BLOCK), F32), score_buf, score_buf],
        compiler_params=_params(3),
        name="attn_prompt",
    )(slopes, q, k_bf, vt_bf, kmean.reshape(batch, n_qb, D_ATTN))


def _decode_attn_kernel(pt_ref, qt_ref, knt_ref, vnt_ref, slope_ref, ck_ref, cv_ref, o_ref, kbuf, vbuf, sem,
                        *, n_pages):
    pages_per_blk = MOBA_BLOCK // PAGE_SIZE
    n_blk = n_pages // pages_per_blk
    past_len = n_pages * PAGE_SIZE
    req = pl.program_id(0)
    slot = req % 2
    mine = lax.broadcasted_iota(I32, (D_ATTN, LANES), 1) == req

    def page_copies(r, buf_slot):
        return [pltpu.make_async_copy(cache.at[pt_ref[r * n_pages + p]], buf.at[buf_slot, p], sem.at[buf_slot])
                for cache, buf in ((ck_ref, kbuf), (cv_ref, vbuf)) for p in range(n_pages)]

    @pl.when(req == 0)
    def _():
        o_ref[...] = jnp.zeros_like(o_ref)
        for c in page_copies(0, 0):
            c.start()

    @pl.when(req + 1 < pl.num_programs(0))
    def _():
        for c in page_copies(req + 1, 1 - slot):
            c.start()

    for c in page_copies(req, slot):
        c.wait()
    k_pages = [kbuf.at[slot, p] for p in range(n_pages)]
    v_pages = [vbuf.at[slot, p] for p in range(n_pages)]

    def column(ref):
        return jnp.sum(jnp.where(mine, ref[...], 0.0), axis=1, keepdims=True)

    def head_sums(x):
        head = lax.broadcasted_iota(I32, (N_HEADS, x.shape[1]), 0)
        out = jnp.zeros((N_HEADS, x.shape[1]), F32)
        for h in range(N_HEADS):
            part = jnp.sum(x[h * HEAD_DIM:(h + 1) * HEAD_DIM], axis=0, keepdims=True)
            out = jnp.where(head == h, part, out)
        return out

    def head_rows(x):
        return jnp.concatenate([jnp.broadcast_to(x[h:h + 1], (HEAD_DIM, x.shape[1]))
                                for h in range(N_HEADS)], axis=0)

    q, k_new, v_new = column(qt_ref), column(knt_ref), column(vnt_ref)
    raw = [head_sums(k_pages[p][...] * q) for p in range(n_pages)]

    gates = []
    for n in range(n_blk):
        tot = raw[n * pages_per_blk]
        for j in range(1, pages_per_blk):
            tot = tot + raw[n * pages_per_blk + j]
        gates.append(jnp.sum(tot, axis=1, keepdims=True) * (1.0 / MOBA_BLOCK))
    picked = []
    for n in range(n_blk):
        beaten = jnp.zeros_like(gates[n])
        for o in range(n_blk):
            if o != n:
                wins = (gates[o] >= gates[n]) if o < n else (gates[o] > gates[n])
                beaten = beaten + wins.astype(F32)
        picked.append(beaten < float(MOBA_TOPK))

    lane = lax.broadcasted_iota(I32, (N_HEADS, LANES), 1)
    slope = slope_ref[...]
    scale = HEAD_DIM ** -0.5
    scores = []
    for p in range(n_pages):
        dist = (past_len - p * PAGE_SIZE - lane).astype(F32)
        scores.append(jnp.where(picked[p // pages_per_blk], raw[p] * scale - slope * dist, NEG_INF))
    s_new = head_sums(q * k_new) * scale

    top = scores[0]
    for s in scores[1:]:
        top = jnp.maximum(top, s)
    m = jnp.maximum(s_new, jnp.max(top, axis=1, keepdims=True))
    p_new = jnp.exp(s_new - m)
    tot_prob = jnp.zeros((N_HEADS, PAGE_SIZE), F32)
    acc = jnp.zeros((D_ATTN, PAGE_SIZE), F32)
    for p in range(n_pages):
        prob = jnp.exp(scores[p] - m)
        tot_prob = tot_prob + prob
        acc = acc + v_pages[p][...] * head_rows(prob)
    l = p_new + jnp.sum(tot_prob, axis=1, keepdims=True)
    out = (jnp.sum(acc, axis=1, keepdims=True) + head_rows(p_new) * v_new) / head_rows(l)
    o_ref[...] = jnp.where(mine, out, o_ref[...])


def _attend_sample(q, k_new, v_new, cache_k, cache_v, page_table, slopes_rows):
    n_req, n_pages = page_table.shape
    assert n_req == LANES
    n_phys = cache_k.shape[0]
    as_tiles = lambda c: c.transpose(0, 2, 3, 1).reshape(n_phys, D_ATTN, PAGE_SIZE)
    ck, cv = as_tiles(cache_k), as_tiles(cache_v)
    cols = pl.BlockSpec((D_ATTN, n_req), lambda r, pt: (0, 0))
    anywhere = pl.BlockSpec(memory_space=pl.ANY)
    page_buf = pltpu.VMEM((2, n_pages, D_ATTN, PAGE_SIZE), F32)
    grid_spec = pltpu.PrefetchScalarGridSpec(
        num_scalar_prefetch=1,
        grid=(n_req,),
        in_specs=[cols, cols, cols, pl.BlockSpec((N_HEADS, LANES), lambda r, pt: (0, 0)), anywhere, anywhere],
        out_specs=cols,
        scratch_shapes=[page_buf, page_buf, pltpu.SemaphoreType.DMA((2,))],
    )
    out_t = pl.pallas_call(
        functools.partial(_decode_attn_kernel, n_pages=n_pages),
        grid_spec=grid_spec,
        out_shape=jax.ShapeDtypeStruct((D_ATTN, n_req), F32),
        compiler_params=_params(1),
        name="attn_sample",
    )(page_table.reshape(-1), q.T, k_new.T, v_new.T, slopes_rows, ck, cv)
    return out_t.T


def _onehots(e, lane):
    return [(e[:, kk:kk + 1] == lane) for kk in range(TOP_K)]


def _expert_ranks(e, carry_ref):
    tm = e.shape[0]
    lane = lax.broadcasted_iota(I32, (tm, LANES), 1)
    hots = _onehots(e, lane)
    hot = jnp.zeros((tm, LANES), F32)
    for h in hots:
        hot = hot + h.astype(F32)
    r = lax.broadcasted_iota(I32, (tm, tm), 0)
    c = lax.broadcasted_iota(I32, (tm, tm), 1)
    earlier = (c < r).astype(BF16)
    before = carry_ref[...] + jnp.dot(earlier, hot.astype(BF16), preferred_element_type=F32)
    rank = jnp.zeros((tm, LANES), F32)
    for kk, h in enumerate(hots):
        rank = jnp.where(lane == kk, jnp.sum(jnp.where(h, before, 0.0), axis=1, keepdims=True), rank)
    carry_ref[...] = carry_ref[...] + jnp.sum(hot, axis=0, keepdims=True)
    return rank.astype(I32)


def _branch_merge(y_conv, x, attn_bf, sg, mod, w, outs, carry_ref):
    (b_dw, ln_g, ln_b, w_ao, w_pw2, w_out, n2g, w_router, b_router, count_in) = w
    xo_ref, h2_ref, eidx_ref, gate_ref, rank_ref, count_ref = outs

    @pl.when(pl.program_id(0) == 0)
    def _():
        carry_ref[...] = count_in[0:1, :]
    y = y_conv + b_dw[...]
    mu = jnp.mean(y, axis=-1, keepdims=True)
    yc = y - mu
    var = jnp.mean(yc * yc, axis=-1, keepdims=True)
    y = (yc * lax.rsqrt(var + EPS)) * ln_g[...] + ln_b[...]
    y = y * _sigmoid(y)
    c_out = jnp.dot(y.astype(BF16), w_pw2[...], preferred_element_type=F32)
    a_out = jnp.dot(attn_bf, w_ao[...], preferred_element_type=F32)
    merged = sg[:, 0:D_MODEL] * a_out + sg[:, D_MODEL:2 * D_MODEL] * c_out
    g1 = mod[:, 2 * D_MODEL:3 * D_MODEL]
    xo = x + g1 * jnp.dot(merged.astype(BF16), w_out[...], preferred_element_type=F32)
    xo_ref[...] = xo
    sh2, sc2 = mod[:, 3 * D_MODEL:4 * D_MODEL], mod[:, 4 * D_MODEL:5 * D_MODEL]
    h2 = _rms(xo, n2g[...]) * (1.0 + sc2) + sh2
    _rows_to_tiles(h2_ref, h2)

    h2_hi = h2.astype(BF16)
    h2_lo = (h2 - h2_hi.astype(F32)).astype(BF16)
    logits = (jnp.dot(h2_hi, w_router[0], preferred_element_type=F32)
              + jnp.dot(h2_lo, w_router[0], preferred_element_type=F32)
              + jnp.dot(h2_hi, w_router[1], preferred_element_type=F32)) + b_router[...]
    tm = logits.shape[0]
    lane = lax.broadcasted_iota(I32, (tm, LANES), 1)
    logits = jnp.where(lane < N_EXPERTS, logits, NEG_INF)
    eidx = lane.astype(F32)
    vals, e_out = [], jnp.zeros((tm, LANES), F32)
    for kk in range(TOP_K):
        mx = jnp.max(logits, axis=1, keepdims=True)
        first = jnp.min(jnp.where(logits == mx, eidx, float(N_EXPERTS)), axis=1, keepdims=True)
        logits = jnp.where(eidx == first, NEG_INF, logits)
        vals.append(mx)
        e_out = jnp.where(lane == kk, first, e_out)
    ex = [jnp.exp(v - vals[0]) for v in vals]
    den = ex[0] + ex[1] + ex[2] + ex[3]
    g_out = jnp.zeros((tm, LANES), F32)
    for kk in range(TOP_K):
        g_out = jnp.where(lane == kk, ex[kk] / den, g_out)
    experts = e_out.astype(I32)
    eidx_ref[...] = experts
    gate_ref[...] = g_out
    rank_ref[...] = _expert_ranks(experts, carry_ref)
    count_ref[...] = jnp.broadcast_to(carry_ref[...], count_ref.shape)


def _merge_prompt_kernel(x_ref, attn_ref, u_ref, halo_ref, sg_ref, mod_ref, wdw_ref, *refs, tiles_per_seq):
    w, outs, (full_ref, shifted_ref, carry_ref) = refs[:10], refs[10:16], refs[16:]
    tm = u_ref.shape[0]
    first = (pl.program_id(0) % tiles_per_seq) == 0
    full_ref[0:HALO, :] = jnp.where(first, 0.0, halo_ref[...])
    full_ref[HALO:HALO + tm, :] = u_ref[...]
    base = HALO - (CONV_WIDTH - 1)
    y = jnp.zeros((tm, D_CONV), F32)
    for phase in range(SUBLANES):
        taps = [j for j in range(CONV_WIDTH) if (base + j) % SUBLANES == phase]
        rows = max(base + j for j in taps) - phase + tm
        shifted_ref[phase, 0:rows, :] = full_ref[phase:phase + rows, :]
        for j in taps:
            lo = base + j - phase
            y = y + wdw_ref[j:j + 1, :] * shifted_ref[phase, lo:lo + tm, :]
    _branch_merge(y, x_ref[...], attn_ref[...], sg_ref[...], mod_ref[0], w, outs, carry_ref)


def _merge_sample_kernel(x_ref, attn_ref, u_ref, state_ref, sg_ref, mod_ref, wdw_ref, *refs):
    w, outs, (carry_ref,) = refs[:10], refs[10:16], refs[16:]
    y = wdw_ref[CONV_WIDTH - 1:CONV_WIDTH, :] * u_ref[...]
    for j in range(CONV_WIDTH - 1):
        y = y + wdw_ref[j:j + 1, :] * state_ref[j]
    _branch_merge(y, x_ref[...], attn_ref[...].astype(BF16), sg_ref[...], mod_ref[...], w, outs, carry_ref)


def _merge(x, attn, u, hist, sg, mod, weights, *, per_token, tm, tiles_per_seq):
    n = x.shape[0]
    row = lambda i: (i, 0)
    if per_token:
        kernel = _merge_sample_kernel
        hist_spec = _const_spec(hist.shape)
        scratch = []
    else:
        kernel = functools.partial(_merge_prompt_kernel, tiles_per_seq=tiles_per_seq)
        hist_spec = pl.BlockSpec((HALO, D_CONV), lambda i: (jnp.maximum(i * (tm // HALO) - 1, 0), 0))
        scratch = [pltpu.VMEM((HALO + tm, D_CONV), F32), pltpu.VMEM((SUBLANES, HALO + tm, D_CONV), F32)]
    scratch.append(pltpu.VMEM((1, LANES), F32))
    flat = lambda w, dt: (jax.ShapeDtypeStruct((n, w), dt), pl.BlockSpec((tm, w), row))
    pairs = [flat(D_MODEL, F32),
             (jax.ShapeDtypeStruct(_tiled(n), F32), pl.BlockSpec(_tiled(tm), row)),
             flat(LANES, I32), flat(LANES, F32), flat(LANES, I32),
             (jax.ShapeDtypeStruct((8, LANES), F32), pl.BlockSpec((8, LANES), lambda i: (0, 0)))]
    return pl.pallas_call(
        kernel,
        grid=(n // tm,),
        in_specs=[pl.BlockSpec((tm, D_MODEL), row), pl.BlockSpec((tm, D_ATTN), row),
                  pl.BlockSpec((tm, D_CONV), row), hist_spec, pl.BlockSpec((tm, 2 * D_MODEL), row),
                  _mod_spec(per_token, tm, tiles_per_seq)] + [_const_spec(a.shape) for a in weights],
        out_specs=[p[1] for p in pairs],
        out_shape=[p[0] for p in pairs],
        scratch_shapes=scratch,
        compiler_params=_params(1),
        name="merge",
    )(x, attn, u, hist, sg, mod, *weights)


def _slot_kernel(e_ref, rank_ref, count_ref, dest_ref, blk_ref, info_ref, *, n_blocks_pad):
    tm = e_ref.shape[0]
    counts = count_ref[0:1, :].astype(I32)
    padded = ((counts + (EXPERT_BLOCK - 1)) // EXPERT_BLOCK) * EXPERT_BLOCK
    r = lax.broadcasted_iota(I32, (LANES, LANES), 0)
    c = lax.broadcasted_iota(I32, (LANES, LANES), 1)
    pstart = jnp.dot(jnp.broadcast_to(padded.astype(F32), (8, LANES)), (r < c).astype(F32),
                     precision=lax.Precision.HIGHEST, preferred_element_type=F32)[0:1]
    pend = pstart + padded.astype(F32)
    lane = lax.broadcasted_iota(I32, (tm, LANES), 1)
    e = e_ref[...]
    dest = rank_ref[...]
    for kk, h in enumerate(_onehots(e, lane)):
        off = jnp.sum(jnp.where(h, pstart, 0.0), axis=1, keepdims=True).astype(I32)
        dest = dest + jnp.where(lane == kk, off, 0)
    dest_ref[...] = dest

    lane1 = lax.broadcasted_iota(I32, (1, LANES), 1)
    used = jnp.max(pend, axis=1, keepdims=True)
    blk_start = (lax.broadcasted_iota(I32, (n_blocks_pad, 1), 0) * EXPERT_BLOCK).astype(F32)
    blk_start = jnp.minimum(blk_start, used - EXPERT_BLOCK)
    done = jnp.where((lane1 < N_EXPERTS) & (pend <= blk_start), 1.0, 0.0)
    blk_exp = jnp.minimum(jnp.sum(done, axis=1, keepdims=True), N_EXPERTS - 1.0)
    blk_ref[...] = jnp.broadcast_to(blk_exp, blk_ref.shape).astype(I32)
    row8 = lax.broadcasted_iota(I32, (8, LANES), 0)
    info = jnp.where(row8 == 0, pstart, jnp.where(row8 == 1, pend, jnp.where(row8 == 2, used, 0.0)))
    info_ref[...] = info.astype(I32)


def _slots(eidx, rank, counts, tm, n_blocks):
    n = eidx.shape[0]
    row = lambda i: (i, 0)
    fixed = lambda i: (0, 0)
    n_blocks_pad = -(-n_blocks // 8) * 8
    dest, blk, info = pl.pallas_call(
        functools.partial(_slot_kernel, n_blocks_pad=n_blocks_pad),
        grid=(n // tm,),
        in_specs=[pl.BlockSpec((tm, LANES), row), pl.BlockSpec((tm, LANES), row),
                  pl.BlockSpec((8, LANES), fixed)],
        out_specs=[pl.BlockSpec((tm, LANES), row), pl.BlockSpec((n_blocks_pad, LANES), fixed),
                   pl.BlockSpec((8, LANES), fixed)],
        out_shape=[jax.ShapeDtypeStruct((n, LANES), I32), jax.ShapeDtypeStruct((n_blocks_pad, LANES), I32),
                   jax.ShapeDtypeStruct((8, LANES), I32)],
        compiler_params=_params(1),
        name="moe_slot",
    )(eidx, rank, counts)
    dest_flat = dest[:, :TOP_K].reshape(-1)
    blk_exp = blk[:n_blocks, 0]
    pend = info[1, :N_EXPERTS]
    n_used = jnp.right_shift(info[2, 0:1], EXPERT_BLOCK.bit_length() - 1)
    return dest_flat, blk_exp, pend, n_used


def _for_each_row(n_rows, fn):
    @pl.loop(0, n_rows // SUBLANES)
    def _(g):
        first = pl.multiple_of(g * SUBLANES, SUBLANES)
        for s in range(SUBLANES):
            fn(first + s)


def _start_all(copies):
    for i, c in enumerate(copies):
        c.start(priority=i % 2)


def _row_copy(src_ref, src_row, dst_ref, dst_row, sem):
    tile = lambda row: pl.ds(pl.multiple_of(row * ROW_PITCH, ROW_PITCH), ROW_PITCH)
    return pltpu.make_async_copy(src_ref.at[tile(src_row)], dst_ref.at[tile(dst_row)], sem)


def _dispatch_kernel(dest_ref, pend_ref, h_ref, tail_ref, xs_ref, zero_ref, sem):
    @pl.when(pl.program_id(0) == 0)
    def _():
        zero_ref[...] = jnp.zeros_like(zero_ref)
        block_rows = zero_ref.shape[0]
        n_blocks = xs_ref.shape[0] // block_rows
        first_unused = pend_ref[N_EXPERTS - 1] // EXPERT_BLOCK

        def fill(start):
            rows = pl.ds(pl.multiple_of(start * ROW_PITCH, block_rows), block_rows)
            return pltpu.make_async_copy(zero_ref, xs_ref.at[rows], sem)

        def last_block(e):
            return jnp.maximum(pend_ref[e] - EXPERT_BLOCK, 0)

        @pl.loop(0, N_EXPERTS)
        def _(e):
            fill(last_block(e)).start()

        @pl.loop(first_unused, n_blocks)
        def _(j):
            fill(j * EXPERT_BLOCK).start()

        @pl.loop(0, N_EXPERTS)
        def _(e):
            fill(last_block(e)).wait()

        @pl.loop(first_unused, n_blocks)
        def _(j):
            fill(j * EXPERT_BLOCK).wait()

    def scatter(h_ref):
        def copies(t):
            return [_row_copy(h_ref, t, xs_ref, dest_ref[t * TOP_K + kk], sem) for kk in range(TOP_K)]

        n_rows = h_ref.shape[0] // ROW_PITCH
        _for_each_row(n_rows, lambda t: _start_all(copies(t)))
        _for_each_row(n_rows, lambda t: [c.wait() for c in copies(t)])

    is_tail = pl.program_id(0) == pl.num_programs(0) - 1

    @pl.when(jnp.logical_not(is_tail))
    def _():
        scatter(h_ref)

    @pl.when(is_tail)
    def _():
        scatter(tail_ref)


def _dispatch(h2, h2_tail, dest_flat, pend, tm, cap):
    n_tiles = h2.shape[0] // _tiled(tm)[0]
    assert h2_tail.shape[0] <= _tiled(tm)[0] and dest_flat.shape[0] == (n_tiles + 1) * tm * TOP_K
    return pl.pallas_call(
        _dispatch_kernel,
        grid=(n_tiles + 1,),
        in_specs=[pl.BlockSpec((tm * TOP_K,), lambda i: (i,), memory_space=pltpu.SMEM),
                  pl.BlockSpec(memory_space=pltpu.SMEM),
                  pl.BlockSpec(_tiled(tm), lambda i: (jnp.minimum(i, n_tiles - 1), 0)),
                  _const_spec(h2_tail.shape)],
        out_specs=pl.BlockSpec(memory_space=pl.ANY),
        out_shape=jax.ShapeDtypeStruct(_tiled(cap), F32),
        scratch_shapes=[pltpu.VMEM(_tiled(EXPERT_BLOCK), F32), pltpu.SemaphoreType.DMA(())],
        compiler_params=_params(1, has_side_effects=True),
        name="moe_dispatch",
    )(dest_flat, pend, h2, h2_tail)


def _expert_kernel(blk_ref, used_ref, x_ref, wg_ref, wu_ref, wd_ref, y_ref, wg_bf, wu_bf, wd_bf):
    j = pl.program_id(0)
    active = j < used_ref[0]
    changed = (j == 0) | (blk_ref[j] != blk_ref[jnp.maximum(j - 1, 0)])

    @pl.when(active & changed)
    def _():
        wg_bf[...] = wg_ref[0].astype(BF16)
        wu_bf[...] = wu_ref[0].astype(BF16)
        wd_bf[...] = wd_ref[0].astype(BF16)

    @pl.when(active)
    def _():
        x = _tiles_to_rows(x_ref).astype(BF16)
        gt = jnp.minimum(jnp.dot(x, wg_bf[...], preferred_element_type=F32), SWIGLU_LIMIT)
        up = jnp.clip(jnp.dot(x, wu_bf[...], preferred_element_type=F32), -SWIGLU_LIMIT, SWIGLU_LIMIT)
        act = gt * _sigmoid(SWIGLU_ALPHA * gt) * (up + 1.0)
        _rows_to_tiles(y_ref, jnp.dot(act.astype(BF16), wd_bf[...], preferred_element_type=F32))

    @pl.when(jnp.logical_not(active))
    def _():
        y_ref[...] = jnp.zeros_like(y_ref)


def _experts(xs, blk_exp, n_used, w_gate, w_up, w_down):
    n_blocks = blk_exp.shape[0]
    d_ff = w_gate.shape[2]
    xrow = lambda j, blk, used: (jnp.minimum(j, used[0] - 1), 0)
    wsel = lambda j, blk, used: (blk[j], 0, 0)
    grid_spec = pltpu.PrefetchScalarGridSpec(
        num_scalar_prefetch=2,
        grid=(n_blocks,),
        in_specs=[pl.BlockSpec(_tiled(EXPERT_BLOCK), xrow),
                  pl.BlockSpec((1, D_MODEL, d_ff), wsel),
                  pl.BlockSpec((1, D_MODEL, d_ff), wsel),
                  pl.BlockSpec((1, d_ff, D_MODEL), wsel)],
        out_specs=pl.BlockSpec(_tiled(EXPERT_BLOCK), lambda j, blk, used: (j, 0)),
        scratch_shapes=[pltpu.VMEM((D_MODEL, d_ff), BF16), pltpu.VMEM((D_MODEL, d_ff), BF16),
                        pltpu.VMEM((d_ff, D_MODEL), BF16)],
    )
    return pl.pallas_call(
        _expert_kernel,
        grid_spec=grid_spec,
        out_shape=jax.ShapeDtypeStruct(xs.shape, F32),
        compiler_params=_params(1),
        name="moe_experts",
    )(blk_exp, n_used, xs, w_gate, w_up, w_down)


def _combine_kernel(dest_ref, y_ref, xo_ref, gate_ref, mod_ref, fg_ref, o_ref, buf_ref, sem, *, per_token):
    tm = xo_ref.shape[0]

    def copies(t):
        return [_row_copy(y_ref, dest_ref[t * TOP_K + kk], buf_ref.at[kk], t, sem) for kk in range(TOP_K)]

    _for_each_row(tm, lambda t: _start_all(copies(t)))
    _for_each_row(tm, lambda t: [c.wait() for c in copies(t)])

    gates = gate_ref[...]
    moe = gates[:, 0:1] * _tiles_to_rows(buf_ref.at[0])
    for kk in range(1, TOP_K):
        moe = moe + gates[:, kk:kk + 1] * _tiles_to_rows(buf_ref.at[kk])
    g2 = _load_mod(mod_ref, per_token)[:, 5 * D_MODEL:6 * D_MODEL]
    o_ref[...] = _rms(xo_ref[...] + g2 * moe, fg_ref[...])


def _combine(y, dest_flat, xo, gates, mod, final_g, *, per_token, tm, tiles_per_seq):
    n = xo.shape[0]
    row = lambda i: (i, 0)
    return pl.pallas_call(
        functools.partial(_combine_kernel, per_token=per_token),
        grid=(n // tm,),
        in_specs=[pl.BlockSpec((tm * TOP_K,), lambda i: (i,), memory_space=pltpu.SMEM),
                  pl.BlockSpec(memory_space=pl.ANY),
                  pl.BlockSpec((tm, D_MODEL), row), pl.BlockSpec((tm, LANES), row),
                  _mod_spec(per_token, tm, tiles_per_seq), _const_spec((1, D_MODEL))],
        out_specs=pl.BlockSpec((tm, D_MODEL), row),
        out_shape=jax.ShapeDtypeStruct((n, D_MODEL), F32),
        scratch_shapes=[pltpu.VMEM((TOP_K,) + _tiled(tm), F32), pltpu.SemaphoreType.DMA(())],
        compiler_params=_params(1),
        name="moe_combine",
    )(dest_flat, y, xo, gates, mod, final_g)


def _moe(prompt, sample, counts, final_g, w_gate, w_up, w_down, *, tm, tiles_per_seq):
    n_p, n_s = prompt[1].shape[0], sample[1].shape[0]
    n_blocks = -(-(n_p + n_s) * TOP_K // EXPERT_BLOCK) + N_EXPERTS
    dest_p, blk_exp, pend, n_used = _slots(prompt[1], prompt[3], counts, SLOT_TILE, n_blocks)
    dest_s, _, _, _ = _slots(sample[1], sample[3], counts, n_s, n_blocks)
    dest_all = jnp.concatenate([dest_p, dest_s, jnp.zeros(((tm - n_s) * TOP_K,), I32)])
    xs = _dispatch(prompt[0], sample[0], dest_all, pend, tm, n_blocks * EXPERT_BLOCK)
    y = _experts(xs, blk_exp, n_used, w_gate, w_up, w_down)
    y_p = _combine(y, dest_p, prompt[4], prompt[2], prompt[5], final_g,
                   per_token=False, tm=tm, tiles_per_seq=tiles_per_seq)
    y_s = _combine(y, dest_s, sample[4], sample[2], sample[5], final_g,
                   per_token=True, tm=n_s, tiles_per_seq=1)
    return y_p, y_s


def kernel(x_prompt, x_sample, c_prompt, c_sample, cache_k, cache_v, state_conv, page_table, norm1_g, norm2_g,
           w_ada, b_ada, w_in, w_attn_out, w_dw, b_dw, ln_g, ln_b, w_pw2, w_out, w_router, b_router, w_gate,
           w_up, w_down, final_g):
    depth = w_in.shape[0]
    assert depth == 1, "single-layer trunk"
    batch, seq, _ = x_prompt.shape
    n_req = x_sample.shape[0]
    assert x_sample.shape[1] == 1 and seq % MOBA_BLOCK == 0 and n_req % 8 == 0
    layer = 0
    row2 = lambda a: a.reshape(1, -1)

    pad = (-batch) % 8
    c_all = jnp.concatenate([c_prompt, jnp.zeros((pad, D_MODEL), F32), c_sample], axis=0)
    ada = _ada(c_all, w_ada[layer], b_ada[layer])
    mod_p = ada[:batch].reshape(batch, 1, 6 * D_MODEL)
    mod_s = ada[batch + pad:]

    w_in_bf = w_in[layer].astype(BF16)
    merge_w = (row2(b_dw[layer]), row2(ln_g[layer]), row2(ln_b[layer]), w_attn_out[layer].astype(BF16),
               w_pw2[layer].astype(BF16), w_out[layer].astype(BF16), row2(norm2_g[layer]),
               _split_bf16(jnp.pad(w_router[layer], ((0, 0), (0, LANES - N_EXPERTS)))),
               jnp.pad(row2(b_router[layer]), ((0, 0), (0, LANES - N_EXPERTS))))
    w_dw_pad = jnp.pad(w_dw[layer], ((0, HALO - CONV_WIDTH), (0, 0)))
    slopes = jnp.exp2(-8.0 * jnp.arange(1, N_HEADS + 1, dtype=F32) / N_HEADS)
    slopes_rows = jnp.broadcast_to(slopes[:, None], (N_HEADS, LANES))
    fg = row2(final_g)
    experts = (w_gate[layer], w_up[layer], w_down[layer])

    tm = MOBA_BLOCK
    tps = seq // tm
    xp = x_prompt.reshape(batch * seq, D_MODEL)
    w_kvt_bf = w_in_bf[:, D_ATTN:3 * D_ATTN].T
    q, k_bf, kt, vt, vt_bf, u, sg, kmean = _inproj(xp, mod_p, row2(norm1_g[layer]), w_in_bf, w_kvt_bf,
                                                   per_token=False, tm=tm, tiles_per_seq=tps)
    attn = _attend_prompt(q, k_bf, vt_bf, kmean, slopes, batch, seq)
    no_counts = jnp.zeros((8, LANES), F32)
    xo, h2, eidx, gates, rank, counts_p = _merge(xp, attn, u, u, sg, mod_p, (w_dw_pad,) + merge_w + (no_counts,),
                                                 per_token=False, tm=tm, tiles_per_seq=tps)
    to_cache = lambda t: t.reshape(batch, N_HEADS, HEAD_DIM, seq).transpose(0, 3, 1, 2)[None]
    k_prompt, v_prompt = to_cache(kt), to_cache(vt)
    conv_prompt = u.reshape(batch, seq, D_CONV)[None, :, seq - (CONV_WIDTH - 1):]

    xs = x_sample.reshape(n_req, D_MODEL)
    qs, ks, vs, us, sgs = _inproj(xs, mod_s, row2(norm1_g[layer]), w_in_bf,
                                  per_token=True, tm=n_req, tiles_per_seq=1)
    attn_s = _attend_sample(qs, ks, vs, cache_k[layer], cache_v[layer], page_table, slopes_rows)
    state = state_conv[layer]
    xo_s, h2_s, eidx_s, gates_s, rank_s, counts = _merge(
        xs, attn_s, us, state.transpose(1, 0, 2), sgs, mod_s, (w_dw_pad,) + merge_w + (counts_p,),
        per_token=True, tm=n_req, tiles_per_seq=1)

    y_prompt, y_sample = _moe((h2, eidx, gates, rank, xo, mod_p), (h2_s, eidx_s, gates_s, rank_s, xo_s, mod_s),
                              counts, fg, *experts, tm=tm, tiles_per_seq=tps)
    k_sample = ks.reshape(1, n_req, 1, N_HEADS, HEAD_DIM)
    v_sample = vs.reshape(1, n_req, 1, N_HEADS, HEAD_DIM)
    conv_sample = jnp.concatenate([state[:, 1:], us[:, None, :]], axis=1)[None]

    return (y_prompt.reshape(batch, seq, D_MODEL), y_sample.reshape(n_req, 1, D_MODEL),
            k_prompt, v_prompt, conv_prompt, k_sample, v_sample, conv_sample)
```

```python
import functools

import jax
import jax.numpy as jnp
from jax import lax
from jax.experimental import pallas as pl
from jax.experimental.pallas import tpu as pltpu

F32 = jnp.float32
BF16 = jnp.bfloat16
I32 = jnp.int32

D_MODEL = 1024
N_HEADS = 8
HEAD_DIM = 64
D_ATTN = N_HEADS * HEAD_DIM
D_CONV = 512
CONV_WIDTH = 31
N_EXPERTS = 32
TOP_K = 4
MOBA_BLOCK = 256
MOBA_TOPK = 3
PAGE_SIZE = 128
SWIGLU_ALPHA = 1.702
SWIGLU_LIMIT = 7.0
EPS = 1e-6
D_IN = 3 * D_ATTN + 2 * D_CONV + 2 * D_MODEL

LANES = 128
SUBLANES = 8
HALO = 32
EXPERT_BLOCK = 256
SLOT_TILE = 1024
HEADS_PER_TILE = LANES // HEAD_DIM
KV_GROUP = 2
VMEM_LIMIT = 56 * 1024 * 1024

NT_DIMS = (((1,), (1,)), ((), ()))
NEG_INF = float("-inf")
LOG2E = 1.4426950408889634
MAX_FLOOR = -1e30


def _params(n_axes, **kw):
    return pltpu.CompilerParams(dimension_semantics=("arbitrary",) * n_axes,
                                vmem_limit_bytes=VMEM_LIMIT, **kw)


def _const_spec(shape):
    nd = len(shape)
    return pl.BlockSpec(shape, lambda *_: (0,) * nd, pipeline_mode=pl.Buffered(1))


def _rms(x, g):
    return (x * lax.rsqrt(jnp.mean(x * x, axis=-1, keepdims=True) + EPS)) * g


def _sigmoid(x):
    return 1.0 / (1.0 + jnp.exp(-x))


ROW_PITCH = D_MODEL // LANES


def _tiled(n_rows):
    return (n_rows * ROW_PITCH, LANES)


def _rows_to_tiles(ref, x):
    for c in range(ROW_PITCH):
        ref[pl.ds(c, x.shape[0], stride=ROW_PITCH), :] = x[:, c * LANES:(c + 1) * LANES]


def _tiles_to_rows(ref):
    r = ref.shape[0] // ROW_PITCH
    return jnp.concatenate([ref[pl.ds(c, r, stride=ROW_PITCH), :] for c in range(ROW_PITCH)], axis=1)


def _split_bf16(w):
    hi = w.astype(BF16)
    return jnp.stack([hi, (w - hi.astype(F32)).astype(BF16)])


def _ada_kernel(c_ref, w_ref, b_ref, o_ref):
    o_ref[...] = jnp.dot(c_ref[...].astype(BF16), w_ref[...].astype(BF16),
                         preferred_element_type=F32) + b_ref[...]


def _ada(c, w_ada, b_ada):
    rows = c.shape[0]
    return pl.pallas_call(
        _ada_kernel,
        grid=(6,),
        in_specs=[pl.BlockSpec((rows, D_MODEL), lambda j: (0, 0)),
                  pl.BlockSpec((D_MODEL, D_MODEL), lambda j: (0, j)),
                  pl.BlockSpec((1, D_MODEL), lambda j: (0, j))],
        out_specs=pl.BlockSpec((rows, D_MODEL), lambda j: (0, j)),
        out_shape=jax.ShapeDtypeStruct((rows, 6 * D_MODEL), F32),
        compiler_params=_params(1),
        name="ada",
    )(c, w_ada, b_ada.reshape(1, -1))


def _mod_spec(per_token, tm, tiles_per_seq):
    if per_token:
        return pl.BlockSpec((tm, 6 * D_MODEL), lambda i: (i, 0))
    return pl.BlockSpec((1, 1, 6 * D_MODEL), lambda i: (i // tiles_per_seq, 0, 0))


def _load_mod(mod_ref, per_token):
    return mod_ref[...] if per_token else mod_ref[0]


def _inproj_common(x_ref, mod_ref, g_ref, w_ref, u_ref, sg_ref, per_token):
    mod = _load_mod(mod_ref, per_token)
    sh1, sc1 = mod[:, 0:D_MODEL], mod[:, D_MODEL:2 * D_MODEL]
    h = (_rms(x_ref[...], g_ref[...]) * (1.0 + sc1) + sh1).astype(BF16)

    def proj(lo, hi):
        return jnp.dot(h, w_ref[:, lo:hi], preferred_element_type=F32)

    o = 3 * D_ATTN
    u_ref[...] = proj(o, o + D_CONV) * _sigmoid(proj(o + D_CONV, o + 2 * D_CONV))
    o += 2 * D_CONV
    sg_ref[...] = _sigmoid(proj(o, o + 2 * D_MODEL))
    return h, proj


def _inproj_sample_kernel(x_ref, mod_ref, g_ref, w_ref, q_ref, k_ref, v_ref, u_ref, sg_ref):
    _, proj = _inproj_common(x_ref, mod_ref, g_ref, w_ref, u_ref, sg_ref, True)
    q_ref[...] = proj(0, D_ATTN)
    k_ref[...] = proj(D_ATTN, 2 * D_ATTN)
    v_ref[...] = proj(2 * D_ATTN, 3 * D_ATTN)


def _inproj_prompt_kernel(x_ref, mod_ref, g_ref, w_ref, wt_ref, q_ref, kb_ref, kt_ref, vt_ref, vtb_ref,
                          u_ref, sg_ref, km_ref):
    h, proj = _inproj_common(x_ref, mod_ref, g_ref, w_ref, u_ref, sg_ref, False)
    q_ref[...] = proj(0, D_ATTN)
    k = proj(D_ATTN, 2 * D_ATTN)
    kb_ref[...] = k.astype(BF16)
    km_ref[0] = jnp.mean(k, axis=0, keepdims=True)
    kt_ref[0] = lax.dot_general(wt_ref[0:D_ATTN, :], h, NT_DIMS, preferred_element_type=F32)
    vt = lax.dot_general(wt_ref[D_ATTN:2 * D_ATTN, :], h, NT_DIMS, preferred_element_type=F32)
    vt_ref[0] = vt
    vtb_ref[0] = vt.astype(BF16)


def _inproj(x, mod, norm_g, w_in_bf, w_kvt_bf=None, *, per_token, tm, tiles_per_seq):
    n = x.shape[0]
    row = lambda i: (i, 0)
    in_specs = [pl.BlockSpec((tm, D_MODEL), row), _mod_spec(per_token, tm, tiles_per_seq),
                _const_spec((1, D_MODEL)), _const_spec((D_MODEL, D_IN))]
    tail = [(D_CONV, F32), (2 * D_MODEL, F32)]
    if per_token:
        kernel, args = _inproj_sample_kernel, (x, mod, norm_g, w_in_bf)
        outs = [(D_ATTN, F32)] * 3 + tail
        out_shape = [jax.ShapeDtypeStruct((n, w), dt) for w, dt in outs]
        out_specs = [pl.BlockSpec((tm, w), row) for w, _ in outs]
    else:
        assert tm == MOBA_BLOCK
        kernel, args = _inproj_prompt_kernel, (x, mod, norm_g, w_in_bf, w_kvt_bf)
        in_specs.append(_const_spec((2 * D_ATTN, D_MODEL)))
        batch = n // (tm * tiles_per_seq)
        seq = tm * tiles_per_seq
        tok = lambda w, dt: (jax.ShapeDtypeStruct((n, w), dt), pl.BlockSpec((tm, w), row))
        tr = lambda dt: (jax.ShapeDtypeStruct((batch, D_ATTN, seq), dt),
                         pl.BlockSpec((1, D_ATTN, tm), lambda i: (i // tiles_per_seq, 0, i % tiles_per_seq)))
        km = (jax.ShapeDtypeStruct((n // tm, 1, D_ATTN), F32), pl.BlockSpec((1, 1, D_ATTN), lambda i: (i, 0, 0)))
        pairs = [tok(D_ATTN, F32), tok(D_ATTN, BF16), tr(F32), tr(F32), tr(BF16)] + [tok(*t) for t in tail] + [km]
        out_shape, out_specs = [p[0] for p in pairs], [p[1] for p in pairs]
    return pl.pallas_call(
        kernel,
        grid=(n // tm,),
        in_specs=in_specs,
        out_specs=out_specs,
        out_shape=out_shape,
        compiler_params=_params(1),
        name="inproj",
    )(*args)


def _select_top(g, idx, n_pick, n_idx, axis):
    sel = jnp.zeros(g.shape, F32)
    for _ in range(n_pick):
        mx = jnp.max(g, axis=axis, keepdims=True)
        first = jnp.min(jnp.where(g == mx, idx, float(n_idx)), axis=axis, keepdims=True)
        hit = idx == first
        sel = jnp.where(hit & (mx > NEG_INF), 1.0, sel)
        g = jnp.where(hit, NEG_INF, g)
    return sel


def _attn_kernel(slopes_ref, q_ref, k_ref, vt_ref, km_ref, o_ref, bias_ref, sel_ref, sa_ref, sb_ref):
    pair, own = pl.program_id(1), pl.program_id(2)
    blk = MOBA_BLOCK
    n_blk = km_ref.shape[1]

    @pl.when(own == 0)
    def _():
        key = lax.broadcasted_iota(I32, (blk, blk), 0)
        qry = lax.broadcasted_iota(I32, (blk, blk), 1)
        rel = (qry - key).astype(F32)
        for hh in range(HEADS_PER_TILE):
            bias = -(slopes_ref[pair * HEADS_PER_TILE + hh] * LOG2E) * rel
            bias_ref[0, hh] = bias
            bias_ref[1, hh] = jnp.where(key <= qry, bias, NEG_INF)

    q = q_ref[...]
    lane = lax.broadcasted_iota(I32, (1, LANES), 1)
    km = km_ref[0]
    bidx = lax.broadcasted_iota(I32, (n_blk, blk), 0)
    bidx_f = bidx.astype(F32)

    qhs = [jnp.where((lane // HEAD_DIM) == hh, q, 0.0) for hh in range(HEADS_PER_TILE)]
    qs = [(qh * (HEAD_DIM ** -0.5 * LOG2E)).astype(BF16) for qh in qhs]
    span = KV_GROUP * blk

    def score_stage(dst_ref, n0):
        n0 = jnp.minimum(n0, n_blk - KV_GROUP)
        kb = k_ref[pl.ds(pl.multiple_of(n0 * blk, span), span), :]
        for hh in range(HEADS_PER_TILE):
            dst_ref[hh] = lax.dot_general(kb, qs[hh], NT_DIMS, preferred_element_type=F32)

    score_stage(sa_ref, 0)

    for hh in range(HEADS_PER_TILE):
        gate = lax.dot_general(km, qhs[hh], NT_DIMS, precision=lax.Precision.HIGHEST,
                               preferred_element_type=F32)
        gate = jnp.where(bidx < own, gate, NEG_INF)
        sel = _select_top(gate, bidx_f, MOBA_TOPK, n_blk, 0)
        sel_ref[hh] = jnp.where(bidx == own, 1.0, sel)

    def softmax_stage(src_ref, n0, carry):
        vtb = vt_ref[0, :, pl.ds(pl.multiple_of(n0 * blk, span), span)]
        new = []
        for hh in range(HEADS_PER_TILE):
            m, l, acc = carry[3 * hh:3 * hh + 3]
            slope = slopes_ref[pair * HEADS_PER_TILE + hh] * LOG2E
            subs, m_new = [], m
            for j in range(KV_GROUP):
                n = n0 + j
                s = src_ref[hh, j * blk:(j + 1) * blk, :] + bias_ref[(n == own).astype(I32), hh]
                c = -slope * ((own - n) * blk).astype(F32)
                picked = sel_ref[hh, pl.ds(n, 1), :] > 0.0
                m_new = jnp.maximum(m_new, jnp.where(picked, jnp.max(s, axis=0, keepdims=True) + c, NEG_INF))
                subs.append((s, c, picked))
            alpha = jnp.exp2(m - m_new)
            l = alpha * l
            probs = []
            for s, c, picked in subs:
                p = jnp.exp2(s - jnp.where(picked, m_new - c, float("inf")))
                l = l + jnp.sum(p, axis=0, keepdims=True)
                probs.append(p.astype(BF16))
            pv = jnp.dot(vtb[hh * HEAD_DIM:(hh + 1) * HEAD_DIM, :], jnp.concatenate(probs, axis=0),
                         preferred_element_type=F32)
            new += [m_new, l, acc * alpha + pv]
        return tuple(new)

    def body(it, carry):
        n0 = it * (2 * KV_GROUP)
        score_stage(sb_ref, n0 + KV_GROUP)
        carry = softmax_stage(sa_ref, n0, carry)
        score_stage(sa_ref, n0 + 2 * KV_GROUP)
        return softmax_stage(sb_ref, n0 + KV_GROUP, carry)

    start = (jnp.full((1, blk), MAX_FLOOR, F32), jnp.zeros((1, blk), F32), jnp.zeros((HEAD_DIM, blk), F32))
    n_trips = own // (2 * KV_GROUP) + 1
    _, l0, acc0, _, l1, acc1 = lax.fori_loop(0, n_trips, body, start * HEADS_PER_TILE)
    out_t = jnp.concatenate([acc0 / l0, acc1 / l1], axis=0)
    o_ref[...] = out_t.T.astype(o_ref.dtype)


def _attend_prompt(q, k_bf, vt_bf, kmean, slopes, batch, seq):
    n_qb = seq // MOBA_BLOCK
    assert n_qb % (2 * KV_GROUP) == 0
    score_buf = pltpu.VMEM((HEADS_PER_TILE, KV_GROUP * MOBA_BLOCK, MOBA_BLOCK), F32)
    tile = lambda b, p, i: (b * n_qb + i, p)
    return pl.pallas_call(
        _attn_kernel,
        grid=(batch, N_HEADS // HEADS_PER_TILE, n_qb),
        in_specs=[pl.BlockSpec(memory_space=pltpu.SMEM),
                  pl.BlockSpec((MOBA_BLOCK, LANES), tile),
                  pl.BlockSpec((seq, LANES), lambda b, p, i: (b, p)),
                  pl.BlockSpec((1, LANES, seq), lambda b, p, i: (b, p, 0)),
                  pl.BlockSpec((1, n_qb, LANES), lambda b, p, i: (b, 0, p))],
        out_specs=pl.BlockSpec((MOBA_BLOCK, LANES), tile),
        out_shape=jax.ShapeDtypeStruct((batch * seq, D_ATTN), BF16),
        scratch_shapes=[pltpu.VMEM((2, HEADS_PER_TILE, MOBA_BLOCK, MOBA_BLOCK), F32),
                        pltpu.VMEM((HEADS_PER_TILE, n_qb, MOBA_BLOCK), F32), score_buf, score_buf],
        compiler_params=_params(3),
        name="attn_prompt",
    )(slopes, q, k_bf, vt_bf, kmean.reshape(batch, n_qb, D_ATTN))


def _decode_attn_kernel(pt_ref, qt_ref, knt_ref, vnt_ref, slope_ref, ck_ref, cv_ref, o_ref, kbuf, vbuf, sem,
                        *, n_pages):
    pages_per_blk = MOBA_BLOCK // PAGE_SIZE
    n_blk = n_pages // pages_per_blk
    past_len = n_pages * PAGE_SIZE
    req = pl.program_id(0)
    slot = req % 2
    mine = lax.broadcasted_iota(I32, (D_ATTN, LANES), 1) == req

    def page_copies(r, buf_slot):
        return [pltpu.make_async_copy(cache.at[pt_ref[r * n_pages + p]], buf.at[buf_slot, p], sem.at[buf_slot])
                for cache, buf in ((ck_ref, kbuf), (cv_ref, vbuf)) for p in range(n_pages)]

    @pl.when(req == 0)
    def _():
        o_ref[...] = jnp.zeros_like(o_ref)
        for c in page_copies(0, 0):
            c.start()

    @pl.when(req + 1 < pl.num_programs(0))
    def _():
        for c in page_copies(req + 1, 1 - slot):
            c.start()

    for c in page_copies(req, slot):
        c.wait()
    k_pages = [kbuf.at[slot, p] for p in range(n_pages)]
    v_pages = [vbuf.at[slot, p] for p in range(n_pages)]

    def column(ref):
        return jnp.sum(jnp.where(mine, ref[...], 0.0), axis=1, keepdims=True)

    def head_sums(x):
        head = lax.broadcasted_iota(I32, (N_HEADS, x.shape[1]), 0)
        out = jnp.zeros((N_HEADS, x.shape[1]), F32)
        for h in range(N_HEADS):
            part = jnp.sum(x[h * HEAD_DIM:(h + 1) * HEAD_DIM], axis=0, keepdims=True)
            out = jnp.where(head == h, part, out)
        return out

    def head_rows(x):
        return jnp.concatenate([jnp.broadcast_to(x[h:h + 1], (HEAD_DIM, x.shape[1]))
                                for h in range(N_HEADS)], axis=0)

    q, k_new, v_new = column(qt_ref), column(knt_ref), column(vnt_ref)
    raw = [head_sums(k_pages[p][...] * q) for p in range(n_pages)]

    gates = []
    for n in range(n_blk):
        tot = raw[n * pages_per_blk]
        for j in range(1, pages_per_blk):
            tot = tot + raw[n * pages_per_blk + j]
        gates.append(jnp.sum(tot, axis=1, keepdims=True) * (1.0 / MOBA_BLOCK))
    picked = []
    for n in range(n_blk):
        beaten = jnp.zeros_like(gates[n])
        for o in range(n_blk):
            if o != n:
                wins = (gates[o] >= gates[n]) if o < n else (gates[o] > gates[n])
                beaten = beaten + wins.astype(F32)
        picked.append(beaten < float(MOBA_TOPK))

    lane = lax.broadcasted_iota(I32, (N_HEADS, LANES), 1)
    slope = slope_ref[...]
    scale = HEAD_DIM ** -0.5
    scores = []
    for p in range(n_pages):
        dist = (past_len - p * PAGE_SIZE - lane).astype(F32)
        scores.append(jnp.where(picked[p // pages_per_blk], raw[p] * scale - slope * dist, NEG_INF))
    s_new = head_sums(q * k_new) * scale

    top = scores[0]
    for s in scores[1:]:
        top = jnp.maximum(top, s)
    m = jnp.maximum(s_new, jnp.max(top, axis=1, keepdims=True))
    p_new = jnp.exp(s_new - m)
    tot_prob = jnp.zeros((N_HEADS, PAGE_SIZE), F32)
    acc = jnp.zeros((D_ATTN, PAGE_SIZE), F32)
    for p in range(n_pages):
        prob = jnp.exp(scores[p] - m)
        tot_prob = tot_prob + prob
        acc = acc + v_pages[p][...] * head_rows(prob)
    l = p_new + jnp.sum(tot_prob, axis=1, keepdims=True)
    out = (jnp.sum(acc, axis=1, keepdims=True) + head_rows(p_new) * v_new) / head_rows(l)
    o_ref[...] = jnp.where(mine, out, o_ref[...])


def _attend_sample(q, k_new, v_new, cache_k, cache_v, page_table, slopes_rows):
    n_req, n_pages = page_table.shape
    assert n_req == LANES
    n_phys = cache_k.shape[0]
    as_tiles = lambda c: c.transpose(0, 2, 3, 1).reshape(n_phys, D_ATTN, PAGE_SIZE)
    ck, cv = as_tiles(cache_k), as_tiles(cache_v)
    cols = pl.BlockSpec((D_ATTN, n_req), lambda r, pt: (0, 0))
    anywhere = pl.BlockSpec(memory_space=pl.ANY)
    page_buf = pltpu.VMEM((2, n_pages, D_ATTN, PAGE_SIZE), F32)
    grid_spec = pltpu.PrefetchScalarGridSpec(
        num_scalar_prefetch=1,
        grid=(n_req,),
        in_specs=[cols, cols, cols, pl.BlockSpec((N_HEADS, LANES), lambda r, pt: (0, 0)), anywhere, anywhere],
        out_specs=cols,
        scratch_shapes=[page_buf, page_buf, pltpu.SemaphoreType.DMA((2,))],
    )
    out_t = pl.pallas_call(
        functools.partial(_decode_attn_kernel, n_pages=n_pages),
        grid_spec=grid_spec,
        out_shape=jax.ShapeDtypeStruct((D_ATTN, n_req), F32),
        compiler_params=_params(1),
        name="attn_sample",
    )(page_table.reshape(-1), q.T, k_new.T, v_new.T, slopes_rows, ck, cv)
    return out_t.T


def _onehots(e, lane):
    return [(e[:, kk:kk + 1] == lane) for kk in range(TOP_K)]


def _expert_ranks(e, carry_ref):
    tm = e.shape[0]
    lane = lax.broadcasted_iota(I32, (tm, LANES), 1)
    hots = _onehots(e, lane)
    hot = jnp.zeros((tm, LANES), F32)
    for h in hots:
        hot = hot + h.astype(F32)
    r = lax.broadcasted_iota(I32, (tm, tm), 0)
    c = lax.broadcasted_iota(I32, (tm, tm), 1)
    earlier = (c < r).astype(BF16)
    before = carry_ref[...] + jnp.dot(earlier, hot.astype(BF16), preferred_element_type=F32)
    rank = jnp.zeros((tm, LANES), F32)
    for kk, h in enumerate(hots):
        rank = jnp.where(lane == kk, jnp.sum(jnp.where(h, before, 0.0), axis=1, keepdims=True), rank)
    carry_ref[...] = carry_ref[...] + jnp.sum(hot, axis=0, keepdims=True)
    return rank.astype(I32)


def _branch_merge(y_conv, x, attn_bf, sg, mod, w, outs, carry_ref):
    (b_dw, ln_g, ln_b, w_ao, w_pw2, w_out, n2g, w_router, b_router, count_in) = w
    xo_ref, h2_ref, eidx_ref, gate_ref, rank_ref, count_ref = outs

    @pl.when(pl.program_id(0) == 0)
    def _():
        carry_ref[...] = count_in[0:1, :]
    y = y_conv + b_dw[...]
    mu = jnp.mean(y, axis=-1, keepdims=True)
    yc = y - mu
    var = jnp.mean(yc * yc, axis=-1, keepdims=True)
    y = (yc * lax.rsqrt(var + EPS)) * ln_g[...] + ln_b[...]
    y = y * _sigmoid(y)
    c_out = jnp.dot(y.astype(BF16), w_pw2[...], preferred_element_type=F32)
    a_out = jnp.dot(attn_bf, w_ao[...], preferred_element_type=F32)
    merged = sg[:, 0:D_MODEL] * a_out + sg[:, D_MODEL:2 * D_MODEL] * c_out
    g1 = mod[:, 2 * D_MODEL:3 * D_MODEL]
    xo = x + g1 * jnp.dot(merged.astype(BF16), w_out[...], preferred_element_type=F32)
    xo_ref[...] = xo
    sh2, sc2 = mod[:, 3 * D_MODEL:4 * D_MODEL], mod[:, 4 * D_MODEL:5 * D_MODEL]
    h2 = _rms(xo, n2g[...]) * (1.0 + sc2) + sh2
    _rows_to_tiles(h2_ref, h2)

    h2_hi = h2.astype(BF16)
    h2_lo = (h2 - h2_hi.astype(F32)).astype(BF16)
    logits = (jnp.dot(h2_hi, w_router[0], preferred_element_type=F32)
              + jnp.dot(h2_lo, w_router[0], preferred_element_type=F32)
              + jnp.dot(h2_hi, w_router[1], preferred_element_type=F32)) + b_router[...]
    tm = logits.shape[0]
    lane = lax.broadcasted_iota(I32, (tm, LANES), 1)
    logits = jnp.where(lane < N_EXPERTS, logits, NEG_INF)
    eidx = lane.astype(F32)
    vals, e_out = [], jnp.zeros((tm, LANES), F32)
    for kk in range(TOP_K):
        mx = jnp.max(logits, axis=1, keepdims=True)
        first = jnp.min(jnp.where(logits == mx, eidx, float(N_EXPERTS)), axis=1, keepdims=True)
        logits = jnp.where(eidx == first, NEG_INF, logits)
        vals.append(mx)
        e_out = jnp.where(lane == kk, first, e_out)
    ex = [jnp.exp(v - vals[0]) for v in vals]
    den = ex[0] + ex[1] + ex[2] + ex[3]
    g_out = jnp.zeros((tm, LANES), F32)
    for kk in range(TOP_K):
        g_out = jnp.where(lane == kk, ex[kk] / den, g_out)
    experts = e_out.astype(I32)
    eidx_ref[...] = experts
    gate_ref[...] = g_out
    rank_ref[...] = _expert_ranks(experts, carry_ref)
    count_ref[...] = jnp.broadcast_to(carry_ref[...], count_ref.shape)


def _merge_prompt_kernel(x_ref, attn_ref, u_ref, halo_ref, sg_ref, mod_ref, wdw_ref, *refs, tiles_per_seq):
    w, outs, (full_ref, shifted_ref, carry_ref) = refs[:10], refs[10:16], refs[16:]
    tm = u_ref.shape[0]
    first = (pl.program_id(0) % tiles_per_seq) == 0
    full_ref[0:HALO, :] = jnp.where(first, 0.0, halo_ref[...])
    full_ref[HALO:HALO + tm, :] = u_ref[...]
    base = HALO - (CONV_WIDTH - 1)
    y = jnp.zeros((tm, D_CONV), F32)
    for phase in range(SUBLANES):
        taps = [j for j in range(CONV_WIDTH) if (base + j) % SUBLANES == phase]
        rows = max(base + j for j in taps) - phase + tm
        shifted_ref[phase, 0:rows, :] = full_ref[phase:phase + rows, :]
        for j in taps:
            lo = base + j - phase
            y = y + wdw_ref[j:j + 1, :] * shifted_ref[phase, lo:lo + tm, :]
    _branch_merge(y, x_ref[...], attn_ref[...], sg_ref[...], mod_ref[0], w, outs, carry_ref)


def _merge_sample_kernel(x_ref, attn_ref, u_ref, state_ref, sg_ref, mod_ref, wdw_ref, *refs):
    w, outs, (carry_ref,) = refs[:10], refs[10:16], refs[16:]
    y = wdw_ref[CONV_WIDTH - 1:CONV_WIDTH, :] * u_ref[...]
    for j in range(CONV_WIDTH - 1):
        y = y + wdw_ref[j:j + 1, :] * state_ref[j]
    _branch_merge(y, x_ref[...], attn_ref[...].astype(BF16), sg_ref[...], mod_ref[...], w, outs, carry_ref)


def _merge(x, attn, u, hist, sg, mod, weights, *, per_token, tm, tiles_per_seq):
    n = x.shape[0]
    row = lambda i: (i, 0)
    if per_token:
        kernel = _merge_sample_kernel
        hist_spec = _const_spec(hist.shape)
        scratch = []
    else:
        kernel = functools.partial(_merge_prompt_kernel, tiles_per_seq=tiles_per_seq)
        hist_spec = pl.BlockSpec((HALO, D_CONV), lambda i: (jnp.maximum(i * (tm // HALO) - 1, 0), 0))
        scratch = [pltpu.VMEM((HALO + tm, D_CONV), F32), pltpu.VMEM((SUBLANES, HALO + tm, D_CONV), F32)]
    scratch.append(pltpu.VMEM((1, LANES), F32))
    flat = lambda w, dt: (jax.ShapeDtypeStruct((n, w), dt), pl.BlockSpec((tm, w), row))
    pairs = [flat(D_MODEL, F32),
             (jax.ShapeDtypeStruct(_tiled(n), F32), pl.BlockSpec(_tiled(tm), row)),
             flat(LANES, I32), flat(LANES, F32), flat(LANES, I32),
             (jax.ShapeDtypeStruct((8, LANES), F32), pl.BlockSpec((8, LANES), lambda i: (0, 0)))]
    return pl.pallas_call(
        kernel,
        grid=(n // tm,),
        in_specs=[pl.BlockSpec((tm, D_MODEL), row), pl.BlockSpec((tm, D_ATTN), row),
                  pl.BlockSpec((tm, D_CONV), row), hist_spec, pl.BlockSpec((tm, 2 * D_MODEL), row),
                  _mod_spec(per_token, tm, tiles_per_seq)] + [_const_spec(a.shape) for a in weights],
        out_specs=[p[1] for p in pairs],
        out_shape=[p[0] for p in pairs],
        scratch_shapes=scratch,
        compiler_params=_params(1),
        name="merge",
    )(x, attn, u, hist, sg, mod, *weights)


def _slot_kernel(e_ref, rank_ref, count_ref, dest_ref, blk_ref, info_ref, *, n_blocks_pad):
    tm = e_ref.shape[0]
    counts = count_ref[0:1, :].astype(I32)
    padded = ((counts + (EXPERT_BLOCK - 1)) // EXPERT_BLOCK) * EXPERT_BLOCK
    r = lax.broadcasted_iota(I32, (LANES, LANES), 0)
    c = lax.broadcasted_iota(I32, (LANES, LANES), 1)
    pstart = jnp.dot(jnp.broadcast_to(padded.astype(F32), (8, LANES)), (r < c).astype(F32),
                     precision=lax.Precision.HIGHEST, preferred_element_type=F32)[0:1]
    pend = pstart + padded.astype(F32)
    lane = lax.broadcasted_iota(I32, (tm, LANES), 1)
    e = e_ref[...]
    dest = rank_ref[...]
    for kk, h in enumerate(_onehots(e, lane)):
        off = jnp.sum(jnp.where(h, pstart, 0.0), axis=1, keepdims=True).astype(I32)
        dest = dest + jnp.where(lane == kk, off, 0)
    dest_ref[...] = dest

    lane1 = lax.broadcasted_iota(I32, (1, LANES), 1)
    used = jnp.max(pend, axis=1, keepdims=True)
    blk_start = (lax.broadcasted_iota(I32, (n_blocks_pad, 1), 0) * EXPERT_BLOCK).astype(F32)
    blk_start = jnp.minimum(blk_start, used - EXPERT_BLOCK)
    done = jnp.where((lane1 < N_EXPERTS) & (pend <= blk_start), 1.0, 0.0)
    blk_exp = jnp.minimum(jnp.sum(done, axis=1, keepdims=True), N_EXPERTS - 1.0)
    blk_ref[...] = jnp.broadcast_to(blk_exp, blk_ref.shape).astype(I32)
    row8 = lax.broadcasted_iota(I32, (8, LANES), 0)
    info = jnp.where(row8 == 0, pstart, jnp.where(row8 == 1, pend, jnp.where(row8 == 2, used, 0.0)))
    info_ref[...] = info.astype(I32)


def _slots(eidx, rank, counts, tm, n_blocks):
    n = eidx.shape[0]
    row = lambda i: (i, 0)
    fixed = lambda i: (0, 0)
    n_blocks_pad = -(-n_blocks // 8) * 8
    dest, blk, info = pl.pallas_call(
        functools.partial(_slot_kernel, n_blocks_pad=n_blocks_pad),
        grid=(n // tm,),
        in_specs=[pl.BlockSpec((tm, LANES), row), pl.BlockSpec((tm, LANES), row),
                  pl.BlockSpec((8, LANES), fixed)],
        out_specs=[pl.BlockSpec((tm, LANES), row), pl.BlockSpec((n_blocks_pad, LANES), fixed),
                   pl.BlockSpec((8, LANES), fixed)],
        out_shape=[jax.ShapeDtypeStruct((n, LANES), I32), jax.ShapeDtypeStruct((n_blocks_pad, LANES), I32),
                   jax.ShapeDtypeStruct((8, LANES), I32)],
        compiler_params=_params(1),
        name="moe_slot",
    )(eidx, rank, counts)
    dest_flat = dest[:, :TOP_K].reshape(-1)
    blk_exp = blk[:n_blocks, 0]
    pend = info[1, :N_EXPERTS]
    n_used = jnp.right_shift(info[2, 0:1], EXPERT_BLOCK.bit_length() - 1)
    return dest_flat, blk_exp, pend, n_used


def _for_each_row(n_rows, fn):
    @pl.loop(0, n_rows // SUBLANES)
    def _(g):
        first = pl.multiple_of(g * SUBLANES, SUBLANES)
        for s in range(SUBLANES):
            fn(first + s)


def _start_all(copies):
    for i, c in enumerate(copies):
        c.start(priority=i % 2)


def _row_copy(src_ref, src_row, dst_ref, dst_row, sem):
    tile = lambda row: pl.ds(pl.multiple_of(row * ROW_PITCH, ROW_PITCH), ROW_PITCH)
    return pltpu.make_async_copy(src_ref.at[tile(src_row)], dst_ref.at[tile(dst_row)], sem)


def _dispatch_kernel(dest_ref, pend_ref, h_ref, tail_ref, xs_ref, zero_ref, sem):
    @pl.when(pl.program_id(0) == 0)
    def _():
        zero_ref[...] = jnp.zeros_like(zero_ref)
        block_rows = zero_ref.shape[0]
        n_blocks = xs_ref.shape[0] // block_rows
        first_unused = pend_ref[N_EXPERTS - 1] // EXPERT_BLOCK

        def fill(start):
            rows = pl.ds(pl.multiple_of(start * ROW_PITCH, block_rows), block_rows)
            return pltpu.make_async_copy(zero_ref, xs_ref.at[rows], sem)

        def last_block(e):
            return jnp.maximum(pend_ref[e] - EXPERT_BLOCK, 0)

        @pl.loop(0, N_EXPERTS)
        def _(e):
            fill(last_block(e)).start()

        @pl.loop(first_unused, n_blocks)
        def _(j):
            fill(j * EXPERT_BLOCK).start()

        @pl.loop(0, N_EXPERTS)
        def _(e):
            fill(last_block(e)).wait()

        @pl.loop(first_unused, n_blocks)
        def _(j):
            fill(j * EXPERT_BLOCK).wait()

    def scatter(h_ref):
        def copies(t):
            return [_row_copy(h_ref, t, xs_ref, dest_ref[t * TOP_K + kk], sem) for kk in range(TOP_K)]

        n_rows = h_ref.shape[0] // ROW_PITCH
        _for_each_row(n_rows, lambda t: _start_all(copies(t)))
        _for_each_row(n_rows, lambda t: [c.wait() for c in copies(t)])

    is_tail = pl.program_id(0) == pl.num_programs(0) - 1

    @pl.when(jnp.logical_not(is_tail))
    def _():
        scatter(h_ref)

    @pl.when(is_tail)
    def _():
        scatter(tail_ref)


def _dispatch(h2, h2_tail, dest_flat, pend, tm, cap):
    n_tiles = h2.shape[0] // _tiled(tm)[0]
    assert h2_tail.shape[0] <= _tiled(tm)[0] and dest_flat.shape[0] == (n_tiles + 1) * tm * TOP_K
    return pl.pallas_call(
        _dispatch_kernel,
        grid=(n_tiles + 1,),
        in_specs=[pl.BlockSpec((tm * TOP_K,), lambda i: (i,), memory_space=pltpu.SMEM),
                  pl.BlockSpec(memory_space=pltpu.SMEM),
                  pl.BlockSpec(_tiled(tm), lambda i: (jnp.minimum(i, n_tiles - 1), 0)),
                  _const_spec(h2_tail.shape)],
        out_specs=pl.BlockSpec(memory_space=pl.ANY),
        out_shape=jax.ShapeDtypeStruct(_tiled(cap), F32),
        scratch_shapes=[pltpu.VMEM(_tiled(EXPERT_BLOCK), F32), pltpu.SemaphoreType.DMA(())],
        compiler_params=_params(1, has_side_effects=True),
        name="moe_dispatch",
    )(dest_flat, pend, h2, h2_tail)


def _expert_kernel(blk_ref, used_ref, x_ref, wg_ref, wu_ref, wd_ref, y_ref, wg_bf, wu_bf, wd_bf):
    j = pl.program_id(0)
    active = j < used_ref[0]
    changed = (j == 0) | (blk_ref[j] != blk_ref[jnp.maximum(j - 1, 0)])

    @pl.when(active & changed)
    def _():
        wg_bf[...] = wg_ref[0].astype(BF16)
        wu_bf[...] = wu_ref[0].astype(BF16)
        wd_bf[...] = wd_ref[0].astype(BF16)

    @pl.when(active)
    def _():
        x = _tiles_to_rows(x_ref).astype(BF16)
        gt = jnp.minimum(jnp.dot(x, wg_bf[...], preferred_element_type=F32), SWIGLU_LIMIT)
        up = jnp.clip(jnp.dot(x, wu_bf[...], preferred_element_type=F32), -SWIGLU_LIMIT, SWIGLU_LIMIT)
        act = gt * _sigmoid(SWIGLU_ALPHA * gt) * (up + 1.0)
        _rows_to_tiles(y_ref, jnp.dot(act.astype(BF16), wd_bf[...], preferred_element_type=F32))

    @pl.when(jnp.logical_not(active))
    def _():
        y_ref[...] = jnp.zeros_like(y_ref)


def _experts(xs, blk_exp, n_used, w_gate, w_up, w_down):
    n_blocks = blk_exp.shape[0]
    d_ff = w_gate.shape[2]
    xrow = lambda j, blk, used: (jnp.minimum(j, used[0] - 1), 0)
    wsel = lambda j, blk, used: (blk[j], 0, 0)
    grid_spec = pltpu.PrefetchScalarGridSpec(
        num_scalar_prefetch=2,
        grid=(n_blocks,),
        in_specs=[pl.BlockSpec(_tiled(EXPERT_BLOCK), xrow),
                  pl.BlockSpec((1, D_MODEL, d_ff), wsel),
                  pl.BlockSpec((1, D_MODEL, d_ff), wsel),
                  pl.BlockSpec((1, d_ff, D_MODEL), wsel)],
        out_specs=pl.BlockSpec(_tiled(EXPERT_BLOCK), lambda j, blk, used: (j, 0)),
        scratch_shapes=[pltpu.VMEM((D_MODEL, d_ff), BF16), pltpu.VMEM((D_MODEL, d_ff), BF16),
                        pltpu.VMEM((d_ff, D_MODEL), BF16)],
    )
    return pl.pallas_call(
        _expert_kernel,
        grid_spec=grid_spec,
        out_shape=jax.ShapeDtypeStruct(xs.shape, F32),
        compiler_params=_params(1),
        name="moe_experts",
    )(blk_exp, n_used, xs, w_gate, w_up, w_down)


def _combine_kernel(dest_ref, next_ref, y_ref, xo_ref, gate_ref, mod_ref, fg_ref, o_ref, buf_ref, sem, *, per_token):
    tm = xo_ref.shape[0]
    step = pl.program_id(0)
    slot = step % 2

    def copies(slots_ref, half, t):
        return [_row_copy(y_ref, slots_ref[t * TOP_K + kk], buf_ref.at[half, kk], t, sem.at[half])
                for kk in range(TOP_K)]

    @pl.when(step == 0)
    def _():
        _for_each_row(tm, lambda t: _start_all(copies(dest_ref, 0, t)))

    @pl.when(step + 1 < pl.num_programs(0))
    def _():
        _for_each_row(tm, lambda t: _start_all(copies(next_ref, 1 - slot, t)))

    _for_each_row(tm, lambda t: [c.wait() for c in copies(dest_ref, slot, t)])

    gates = gate_ref[...]
    moe = gates[:, 0:1] * _tiles_to_rows(buf_ref.at[slot, 0])
    for kk in range(1, TOP_K):
        moe = moe + gates[:, kk:kk + 1] * _tiles_to_rows(buf_ref.at[slot, kk])
    g2 = _load_mod(mod_ref, per_token)[:, 5 * D_MODEL:6 * D_MODEL]
    o_ref[...] = _rms(xo_ref[...] + g2 * moe, fg_ref[...])


def _combine(y, dest_flat, xo, gates, mod, final_g, *, per_token, tm, tiles_per_seq):
    n = xo.shape[0]
    n_tiles = n // tm
    row = lambda i: (i, 0)
    return pl.pallas_call(
        functools.partial(_combine_kernel, per_token=per_token),
        grid=(n_tiles,),
        in_specs=[pl.BlockSpec((tm * TOP_K,), lambda i: (i,), memory_space=pltpu.SMEM),
                  pl.BlockSpec((tm * TOP_K,), lambda i: (jnp.minimum(i + 1, n_tiles - 1),), memory_space=pltpu.SMEM),
                  pl.BlockSpec(memory_space=pl.ANY),
                  pl.BlockSpec((tm, D_MODEL), row), pl.BlockSpec((tm, LANES), row),
                  _mod_spec(per_token, tm, tiles_per_seq), _const_spec((1, D_MODEL))],
        out_specs=pl.BlockSpec((tm, D_MODEL), row),
        out_shape=jax.ShapeDtypeStruct((n, D_MODEL), F32),
        scratch_shapes=[pltpu.VMEM((2, TOP_K) + _tiled(tm), F32), pltpu.SemaphoreType.DMA((2,))],
        compiler_params=_params(1),
        name="moe_combine",
    )(dest_flat, dest_flat, y, xo, gates, mod, final_g)


def _moe(prompt, sample, counts, final_g, w_gate, w_up, w_down, *, tm, tiles_per_seq):
    n_p, n_s = prompt[1].shape[0], sample[1].shape[0]
    n_blocks = -(-(n_p + n_s) * TOP_K // EXPERT_BLOCK) + N_EXPERTS
    dest_p, blk_exp, pend, n_used = _slots(prompt[1], prompt[3], counts, SLOT_TILE, n_blocks)
    dest_s, _, _, _ = _slots(sample[1], sample[3], counts, n_s, n_blocks)
    dest_all = jnp.concatenate([dest_p, dest_s, jnp.zeros(((tm - n_s) * TOP_K,), I32)])
    xs = _dispatch(prompt[0], sample[0], dest_all, pend, tm, n_blocks * EXPERT_BLOCK)
    y = _experts(xs, blk_exp, n_used, w_gate, w_up, w_down)
    y_p = _combine(y, dest_p, prompt[4], prompt[2], prompt[5], final_g,
                   per_token=False, tm=tm, tiles_per_seq=tiles_per_seq)
    y_s = _combine(y, dest_s, sample[4], sample[2], sample[5], final_g,
                   per_token=True, tm=n_s, tiles_per_seq=1)
    return y_p, y_s


def kernel(x_prompt, x_sample, c_prompt, c_sample, cache_k, cache_v, state_conv, page_table, norm1_g, norm2_g,
           w_ada, b_ada, w_in, w_attn_out, w_dw, b_dw, ln_g, ln_b, w_pw2, w_out, w_router, b_router, w_gate,
           w_up, w_down, final_g):
    depth = w_in.shape[0]
    assert depth == 1, "single-layer trunk"
    batch, seq, _ = x_prompt.shape
    n_req = x_sample.shape[0]
    assert x_sample.shape[1] == 1 and seq % MOBA_BLOCK == 0 and n_req % 8 == 0
    layer = 0
    row2 = lambda a: a.reshape(1, -1)

    pad = (-batch) % 8
    c_all = jnp.concatenate([c_prompt, jnp.zeros((pad, D_MODEL), F32), c_sample], axis=0)
    ada = _ada(c_all, w_ada[layer], b_ada[layer])
    mod_p = ada[:batch].reshape(batch, 1, 6 * D_MODEL)
    mod_s = ada[batch + pad:]

    w_in_bf = w_in[layer].astype(BF16)
    merge_w = (row2(b_dw[layer]), row2(ln_g[layer]), row2(ln_b[layer]), w_attn_out[layer].astype(BF16),
               w_pw2[layer].astype(BF16), w_out[layer].astype(BF16), row2(norm2_g[layer]),
               _split_bf16(jnp.pad(w_router[layer], ((0, 0), (0, LANES - N_EXPERTS)))),
               jnp.pad(row2(b_router[layer]), ((0, 0), (0, LANES - N_EXPERTS))))
    w_dw_pad = jnp.pad(w_dw[layer], ((0, HALO - CONV_WIDTH), (0, 0)))
    slopes = jnp.exp2(-8.0 * jnp.arange(1, N_HEADS + 1, dtype=F32) / N_HEADS)
    slopes_rows = jnp.broadcast_to(slopes[:, None], (N_HEADS, LANES))
    fg = row2(final_g)
    experts = (w_gate[layer], w_up[layer], w_down[layer])

    tm = MOBA_BLOCK
    tps = seq // tm
    xp = x_prompt.reshape(batch * seq, D_MODEL)
    w_kvt_bf = w_in_bf[:, D_ATTN:3 * D_ATTN].T
    q, k_bf, kt, vt, vt_bf, u, sg, kmean = _inproj(xp, mod_p, row2(norm1_g[layer]), w_in_bf, w_kvt_bf,
                                                   per_token=False, tm=tm, tiles_per_seq=tps)
    attn = _attend_prompt(q, k_bf, vt_bf, kmean, slopes, batch, seq)
    no_counts = jnp.zeros((8, LANES), F32)
    xo, h2, eidx, gates, rank, counts_p = _merge(xp, attn, u, u, sg, mod_p, (w_dw_pad,) + merge_w + (no_counts,),
                                                 per_token=False, tm=tm, tiles_per_seq=tps)
    to_cache = lambda t: t.reshape(batch, N_HEADS, HEAD_DIM, seq).transpose(0, 3, 1, 2)[None]
    k_prompt, v_prompt = to_cache(kt), to_cache(vt)
    conv_prompt = u.reshape(batch, seq, D_CONV)[None, :, seq - (CONV_WIDTH - 1):]

    xs = x_sample.reshape(n_req, D_MODEL)
    qs, ks, vs, us, sgs = _inproj(xs, mod_s, row2(norm1_g[layer]), w_in_bf,
                                  per_token=True, tm=n_req, tiles_per_seq=1)
    attn_s = _attend_sample(qs, ks, vs, cache_k[layer], cache_v[layer], page_table, slopes_rows)
    state = state_conv[layer]
    xo_s, h2_s, eidx_s, gates_s, rank_s, counts = _merge(
        xs, attn_s, us, state.transpose(1, 0, 2), sgs, mod_s, (w_dw_pad,) + merge_w + (counts_p,),
        per_token=True, tm=n_req, tiles_per_seq=1)

    y_prompt, y_sample = _moe((h2, eidx, gates, rank, xo, mod_p), (h2_s, eidx_s, gates_s, rank_s, xo_s, mod_s),
                              counts, fg, *experts, tm=tm, tiles_per_seq=tps)
    k_sample = ks.reshape(1, n_req, 1, N_HEADS, HEAD_DIM)
    v_sample = vs.reshape(1, n_req, 1, N_HEADS, HEAD_DIM)
    conv_sample = jnp.concatenate([state[:, 1:], us[:, None, :]], axis=1)[None]

    return (y_prompt.reshape(batch, seq, D_MODEL), y_sample.reshape(n_req, 1, D_MODEL),
            k_prompt, v_prompt, conv_prompt, k_sample, v_sample, conv_sample)
```

```python
import functools

import jax
import jax.numpy as jnp
from jax import lax
from jax.experimental import pallas as pl
from jax.experimental.pallas import tpu as pltpu

F32 = jnp.float32
BF16 = jnp.bfloat16
I32 = jnp.int32

D_MODEL = 1024
N_HEADS = 8
HEAD_DIM = 64
D_ATTN = N_HEADS * HEAD_DIM
D_CONV = 512
CONV_WIDTH = 31
N_EXPERTS = 32
TOP_K = 4
MOBA_BLOCK = 256
MOBA_TOPK = 3
PAGE_SIZE = 128
SWIGLU_ALPHA = 1.702
SWIGLU_LIMIT = 7.0
EPS = 1e-6
D_IN = 3 * D_ATTN + 2 * D_CONV + 2 * D_MODEL

LANES = 128
SUBLANES = 8
HALO = 32
EXPERT_BLOCK = 512
SLOT_TILE = 1024
HEADS_PER_TILE = LANES // HEAD_DIM
KV_GROUP = 2
VMEM_LIMIT = 56 * 1024 * 1024

NT_DIMS = (((1,), (1,)), ((), ()))
NEG_INF = float("-inf")
LOG2E = 1.4426950408889634
MAX_FLOOR = -1e30


def _params(n_axes, **kw):
    return pltpu.CompilerParams(dimension_semantics=("arbitrary",) * n_axes,
                                vmem_limit_bytes=VMEM_LIMIT, **kw)


def _const_spec(shape):
    nd = len(shape)
    return pl.BlockSpec(shape, lambda *_: (0,) * nd, pipeline_mode=pl.Buffered(1))


def _rms(x, g):
    return (x * lax.rsqrt(jnp.mean(x * x, axis=-1, keepdims=True) + EPS)) * g


def _sigmoid(x):
    return 1.0 / (1.0 + jnp.exp(-x))


ROW_PITCH = D_MODEL // LANES


def _tiled(n_rows):
    return (n_rows * ROW_PITCH, LANES)


def _rows_to_tiles(ref, x):
    for c in range(ROW_PITCH):
        ref[pl.ds(c, x.shape[0], stride=ROW_PITCH), :] = x[:, c * LANES:(c + 1) * LANES]


def _tiles_to_rows(ref):
    r = ref.shape[0] // ROW_PITCH
    return jnp.concatenate([ref[pl.ds(c, r, stride=ROW_PITCH), :] for c in range(ROW_PITCH)], axis=1)


def _split_bf16(w):
    hi = w.astype(BF16)
    return jnp.stack([hi, (w - hi.astype(F32)).astype(BF16)])


def _ada_kernel(c_ref, w_ref, b_ref, o_ref):
    o_ref[...] = jnp.dot(c_ref[...].astype(BF16), w_ref[...].astype(BF16),
                         preferred_element_type=F32) + b_ref[...]


def _ada(c, w_ada, b_ada):
    rows = c.shape[0]
    return pl.pallas_call(
        _ada_kernel,
        grid=(6,),
        in_specs=[pl.BlockSpec((rows, D_MODEL), lambda j: (0, 0)),
                  pl.BlockSpec((D_MODEL, D_MODEL), lambda j: (0, j)),
                  pl.BlockSpec((1, D_MODEL), lambda j: (0, j))],
        out_specs=pl.BlockSpec((rows, D_MODEL), lambda j: (0, j)),
        out_shape=jax.ShapeDtypeStruct((rows, 6 * D_MODEL), F32),
        compiler_params=_params(1),
        name="ada",
    )(c, w_ada, b_ada.reshape(1, -1))


def _mod_spec(per_token, tm, tiles_per_seq):
    if per_token:
        return pl.BlockSpec((tm, 6 * D_MODEL), lambda i: (i, 0))
    return pl.BlockSpec((1, 1, 6 * D_MODEL), lambda i: (i // tiles_per_seq, 0, 0))


def _load_mod(mod_ref, per_token):
    return mod_ref[...] if per_token else mod_ref[0]


def _inproj_common(x_ref, mod_ref, g_ref, w_ref, u_ref, sg_ref, per_token):
    mod = _load_mod(mod_ref, per_token)
    sh1, sc1 = mod[:, 0:D_MODEL], mod[:, D_MODEL:2 * D_MODEL]
    h = (_rms(x_ref[...], g_ref[...]) * (1.0 + sc1) + sh1).astype(BF16)

    def proj(lo, hi):
        return jnp.dot(h, w_ref[:, lo:hi], preferred_element_type=F32)

    o = 3 * D_ATTN
    u_ref[...] = proj(o, o + D_CONV) * _sigmoid(proj(o + D_CONV, o + 2 * D_CONV))
    o += 2 * D_CONV
    sg_ref[...] = _sigmoid(proj(o, o + 2 * D_MODEL))
    return h, proj


def _inproj_sample_kernel(x_ref, mod_ref, g_ref, w_ref, q_ref, k_ref, v_ref, u_ref, sg_ref):
    _, proj = _inproj_common(x_ref, mod_ref, g_ref, w_ref, u_ref, sg_ref, True)
    q_ref[...] = proj(0, D_ATTN)
    k_ref[...] = proj(D_ATTN, 2 * D_ATTN)
    v_ref[...] = proj(2 * D_ATTN, 3 * D_ATTN)


def _inproj_prompt_kernel(x_ref, mod_ref, g_ref, w_ref, wt_ref, q_ref, kb_ref, kt_ref, vt_ref, vtb_ref,
                          u_ref, sg_ref, km_ref):
    h, proj = _inproj_common(x_ref, mod_ref, g_ref, w_ref, u_ref, sg_ref, False)
    q_ref[...] = proj(0, D_ATTN)
    k = proj(D_ATTN, 2 * D_ATTN)
    kb_ref[...] = k.astype(BF16)
    km_ref[0] = jnp.mean(k, axis=0, keepdims=True)
    kt_ref[0] = lax.dot_general(wt_ref[0:D_ATTN, :], h, NT_DIMS, preferred_element_type=F32)
    vt = lax.dot_general(wt_ref[D_ATTN:2 * D_ATTN, :], h, NT_DIMS, preferred_element_type=F32)
    vt_ref[0] = vt
    vtb_ref[0] = vt.astype(BF16)


def _inproj(x, mod, norm_g, w_in_bf, w_kvt_bf=None, *, per_token, tm, tiles_per_seq):
    n = x.shape[0]
    row = lambda i: (i, 0)
    in_specs = [pl.BlockSpec((tm, D_MODEL), row), _mod_spec(per_token, tm, tiles_per_seq),
                _const_spec((1, D_MODEL)), _const_spec((D_MODEL, D_IN))]
    tail = [(D_CONV, F32), (2 * D_MODEL, F32)]
    if per_token:
        kernel, args = _inproj_sample_kernel, (x, mod, norm_g, w_in_bf)
        outs = [(D_ATTN, F32)] * 3 + tail
        out_shape = [jax.ShapeDtypeStruct((n, w), dt) for w, dt in outs]
        out_specs = [pl.BlockSpec((tm, w), row) for w, _ in outs]
    else:
        assert tm == MOBA_BLOCK
        kernel, args = _inproj_prompt_kernel, (x, mod, norm_g, w_in_bf, w_kvt_bf)
        in_specs.append(_const_spec((2 * D_ATTN, D_MODEL)))
        batch = n // (tm * tiles_per_seq)
        seq = tm * tiles_per_seq
        tok = lambda w, dt: (jax.ShapeDtypeStruct((n, w), dt), pl.BlockSpec((tm, w), row))
        tr = lambda dt: (jax.ShapeDtypeStruct((batch, D_ATTN, seq), dt),
                         pl.BlockSpec((1, D_ATTN, tm), lambda i: (i // tiles_per_seq, 0, i % tiles_per_seq)))
        km = (jax.ShapeDtypeStruct((n // tm, 1, D_ATTN), F32), pl.BlockSpec((1, 1, D_ATTN), lambda i: (i, 0, 0)))
        pairs = [tok(D_ATTN, F32), tok(D_ATTN, BF16), tr(F32), tr(F32), tr(BF16)] + [tok(*t) for t in tail] + [km]
        out_shape, out_specs = [p[0] for p in pairs], [p[1] for p in pairs]
    return pl.pallas_call(
        kernel,
        grid=(n // tm,),
        in_specs=in_specs,
        out_specs=out_specs,
        out_shape=out_shape,
        compiler_params=_params(1),
        name="inproj",
    )(*args)


def _select_top(g, idx, n_pick, n_idx, axis):
    sel = jnp.zeros(g.shape, F32)
    for _ in range(n_pick):
        mx = jnp.max(g, axis=axis, keepdims=True)
        first = jnp.min(jnp.where(g == mx, idx, float(n_idx)), axis=axis, keepdims=True)
        hit = idx == first
        sel = jnp.where(hit & (mx > NEG_INF), 1.0, sel)
        g = jnp.where(hit, NEG_INF, g)
    return sel


def _attn_kernel(slopes_ref, q_ref, k_ref, vt_ref, km_ref, o_ref, bias_ref, sel_ref, sa_ref, sb_ref):
    pair, own = pl.program_id(1), pl.program_id(2)
    blk = MOBA_BLOCK
    n_blk = km_ref.shape[1]

    @pl.when(own == 0)
    def _():
        key = lax.broadcasted_iota(I32, (blk, blk), 0)
        qry = lax.broadcasted_iota(I32, (blk, blk), 1)
        rel = (qry - key).astype(F32)
        for hh in range(HEADS_PER_TILE):
            bias = -(slopes_ref[pair * HEADS_PER_TILE + hh] * LOG2E) * rel
            bias_ref[0, hh] = bias
            bias_ref[1, hh] = jnp.where(key <= qry, bias, NEG_INF)

    q = q_ref[...]
    lane = lax.broadcasted_iota(I32, (1, LANES), 1)
    km = km_ref[0]
    bidx = lax.broadcasted_iota(I32, (n_blk, blk), 0)
    bidx_f = bidx.astype(F32)

    qhs = [jnp.where((lane // HEAD_DIM) == hh, q, 0.0) for hh in range(HEADS_PER_TILE)]
    qs = [(qh * (HEAD_DIM ** -0.5 * LOG2E)).astype(BF16) for qh in qhs]
    span = KV_GROUP * blk

    def score_stage(dst_ref, n0):
        n0 = jnp.minimum(n0, n_blk - KV_GROUP)
        kb = k_ref[pl.ds(pl.multiple_of(n0 * blk, span), span), :]
        for hh in range(HEADS_PER_TILE):
            dst_ref[hh] = lax.dot_general(kb, qs[hh], NT_DIMS, preferred_element_type=F32)

    score_stage(sa_ref, 0)

    for hh in range(HEADS_PER_TILE):
        gate = lax.dot_general(km, qhs[hh], NT_DIMS, precision=lax.Precision.HIGHEST,
                               preferred_element_type=F32)
        gate = jnp.where(bidx < own, gate, NEG_INF)
        sel = _select_top(gate, bidx_f, MOBA_TOPK, n_blk, 0)
        sel_ref[hh] = jnp.where(bidx == own, 1.0, sel)

    def softmax_stage(src_ref, n0, carry):
        vtb = vt_ref[0, :, pl.ds(pl.multiple_of(n0 * blk, span), span)]
        new = []
        for hh in range(HEADS_PER_TILE):
            m, l, acc = carry[3 * hh:3 * hh + 3]
            slope = slopes_ref[pair * HEADS_PER_TILE + hh] * LOG2E
            subs, m_new = [], m
            for j in range(KV_GROUP):
                n = n0 + j
                s = src_ref[hh, j * blk:(j + 1) * blk, :] + bias_ref[(n == own).astype(I32), hh]
                c = -slope * ((own - n) * blk).astype(F32)
                picked = sel_ref[hh, pl.ds(n, 1), :] > 0.0
                m_new = jnp.maximum(m_new, jnp.where(picked, jnp.max(s, axis=0, keepdims=True) + c, NEG_INF))
                subs.append((s, c, picked))
            alpha = jnp.exp2(m - m_new)
            l = alpha * l
            probs = []
            for s, c, picked in subs:
                p = jnp.exp2(s - jnp.where(picked, m_new - c, float("inf")))
                l = l + jnp.sum(p, axis=0, keepdims=True)
                probs.append(p.astype(BF16))
            pv = jnp.dot(vtb[hh * HEAD_DIM:(hh + 1) * HEAD_DIM, :], jnp.concatenate(probs, axis=0),
                         preferred_element_type=F32)
            new += [m_new, l, acc * alpha + pv]
        return tuple(new)

    def body(it, carry):
        n0 = it * (2 * KV_GROUP)
        score_stage(sb_ref, n0 + KV_GROUP)
        carry = softmax_stage(sa_ref, n0, carry)
        score_stage(sa_ref, n0 + 2 * KV_GROUP)
        return softmax_stage(sb_ref, n0 + KV_GROUP, carry)

    start = (jnp.full((1, blk), MAX_FLOOR, F32), jnp.zeros((1, blk), F32), jnp.zeros((HEAD_DIM, blk), F32))
    n_trips = own // (2 * KV_GROUP) + 1
    _, l0, acc0, _, l1, acc1 = lax.fori_loop(0, n_trips, body, start * HEADS_PER_TILE)
    out_t = jnp.concatenate([acc0 / l0, acc1 / l1], axis=0)
    o_ref[...] = out_t.T.astype(o_ref.dtype)


def _attend_prompt(q, k_bf, vt_bf, kmean, slopes, batch, seq):
    n_qb = seq // MOBA_BLOCK
    assert n_qb % (2 * KV_GROUP) == 0
    score_buf = pltpu.VMEM((HEADS_PER_TILE, KV_GROUP * MOBA_BLOCK, MOBA_BLOCK), F32)
    tile = lambda b, p, i: (b * n_qb + i, p)
    return pl.pallas_call(
        _attn_kernel,
        grid=(batch, N_HEADS // HEADS_PER_TILE, n_qb),
        in_specs=[pl.BlockSpec(memory_space=pltpu.SMEM),
                  pl.BlockSpec((MOBA_BLOCK, LANES), tile),
                  pl.BlockSpec((seq, LANES), lambda b, p, i: (b, p)),
                  pl.BlockSpec((1, LANES, seq), lambda b, p, i: (b, p, 0)),
                  pl.BlockSpec((1, n_qb, LANES), lambda b, p, i: (b, 0, p))],
        out_specs=pl.BlockSpec((MOBA_BLOCK, LANES), tile),
        out_shape=jax.ShapeDtypeStruct((batch * seq, D_ATTN), BF16),
        scratch_shapes=[pltpu.VMEM((2, HEADS_PER_TILE, MOBA_BLOCK, MOBA_BLOCK), F32),
                        pltpu.VMEM((HEADS_PER_TILE, n_qb, MOBA_BLOCK), F32), score_buf, score_buf],
        compiler_params=_params(3),
        name="attn_prompt",
    )(slopes, q, k_bf, vt_bf, kmean.reshape(batch, n_qb, D_ATTN))


def _decode_attn_kernel(pt_ref, qt_ref, knt_ref, vnt_ref, slope_ref, ck_ref, cv_ref, o_ref, kbuf, vbuf, sem,
                        *, n_pages):
    pages_per_blk = MOBA_BLOCK // PAGE_SIZE
    n_blk = n_pages // pages_per_blk
    past_len = n_pages * PAGE_SIZE
    req = pl.program_id(0)
    slot = req % 2
    mine = lax.broadcasted_iota(I32, (D_ATTN, LANES), 1) == req

    def page_copies(r, buf_slot):
        return [pltpu.make_async_copy(cache.at[pt_ref[r * n_pages + p]], buf.at[buf_slot, p], sem.at[buf_slot])
                for cache, buf in ((ck_ref, kbuf), (cv_ref, vbuf)) for p in range(n_pages)]

    @pl.when(req == 0)
    def _():
        o_ref[...] = jnp.zeros_like(o_ref)
        for c in page_copies(0, 0):
            c.start()

    @pl.when(req + 1 < pl.num_programs(0))
    def _():
        for c in page_copies(req + 1, 1 - slot):
            c.start()

    for c in page_copies(req, slot):
        c.wait()
    k_pages = [kbuf.at[slot, p] for p in range(n_pages)]
    v_pages = [vbuf.at[slot, p] for p in range(n_pages)]

    def column(ref):
        return jnp.sum(jnp.where(mine, ref[...], 0.0), axis=1, keepdims=True)

    def head_sums(x):
        head = lax.broadcasted_iota(I32, (N_HEADS, x.shape[1]), 0)
        out = jnp.zeros((N_HEADS, x.shape[1]), F32)
        for h in range(N_HEADS):
            part = jnp.sum(x[h * HEAD_DIM:(h + 1) * HEAD_DIM], axis=0, keepdims=True)
            out = jnp.where(head == h, part, out)
        return out

    def head_rows(x):
        return jnp.concatenate([jnp.broadcast_to(x[h:h + 1], (HEAD_DIM, x.shape[1]))
                                for h in range(N_HEADS)], axis=0)

    q, k_new, v_new = column(qt_ref), column(knt_ref), column(vnt_ref)
    raw = [head_sums(k_pages[p][...] * q) for p in range(n_pages)]

    gates = []
    for n in range(n_blk):
        tot = raw[n * pages_per_blk]
        for j in range(1, pages_per_blk):
            tot = tot + raw[n * pages_per_blk + j]
        gates.append(jnp.sum(tot, axis=1, keepdims=True) * (1.0 / MOBA_BLOCK))
    picked = []
    for n in range(n_blk):
        beaten = jnp.zeros_like(gates[n])
        for o in range(n_blk):
            if o != n:
                wins = (gates[o] >= gates[n]) if o < n else (gates[o] > gates[n])
                beaten = beaten + wins.astype(F32)
        picked.append(beaten < float(MOBA_TOPK))

    lane = lax.broadcasted_iota(I32, (N_HEADS, LANES), 1)
    slope = slope_ref[...]
    scale = HEAD_DIM ** -0.5
    scores = []
    for p in range(n_pages):
        dist = (past_len - p * PAGE_SIZE - lane).astype(F32)
        scores.append(jnp.where(picked[p // pages_per_blk], raw[p] * scale - slope * dist, NEG_INF))
    s_new = head_sums(q * k_new) * scale

    top = scores[0]
    for s in scores[1:]:
        top = jnp.maximum(top, s)
    m = jnp.maximum(s_new, jnp.max(top, axis=1, keepdims=True))
    p_new = jnp.exp(s_new - m)
    tot_prob = jnp.zeros((N_HEADS, PAGE_SIZE), F32)
    acc = jnp.zeros((D_ATTN, PAGE_SIZE), F32)
    for p in range(n_pages):
        prob = jnp.exp(scores[p] - m)
        tot_prob = tot_prob + prob
        acc = acc + v_pages[p][...] * head_rows(prob)
    l = p_new + jnp.sum(tot_prob, axis=1, keepdims=True)
    out = (jnp.sum(acc, axis=1, keepdims=True) + head_rows(p_new) * v_new) / head_rows(l)
    o_ref[...] = jnp.where(mine, out, o_ref[...])


def _attend_sample(q, k_new, v_new, cache_k, cache_v, page_table, slopes_rows):
    n_req, n_pages = page_table.shape
    assert n_req == LANES
    n_phys = cache_k.shape[0]
    as_tiles = lambda c: c.transpose(0, 2, 3, 1).reshape(n_phys, D_ATTN, PAGE_SIZE)
    ck, cv = as_tiles(cache_k), as_tiles(cache_v)
    cols = pl.BlockSpec((D_ATTN, n_req), lambda r, pt: (0, 0))
    anywhere = pl.BlockSpec(memory_space=pl.ANY)
    page_buf = pltpu.VMEM((2, n_pages, D_ATTN, PAGE_SIZE), F32)
    grid_spec = pltpu.PrefetchScalarGridSpec(
        num_scalar_prefetch=1,
        grid=(n_req,),
        in_specs=[cols, cols, cols, pl.BlockSpec((N_HEADS, LANES), lambda r, pt: (0, 0)), anywhere, anywhere],
        out_specs=cols,
        scratch_shapes=[page_buf, page_buf, pltpu.SemaphoreType.DMA((2,))],
    )
    out_t = pl.pallas_call(
        functools.partial(_decode_attn_kernel, n_pages=n_pages),
        grid_spec=grid_spec,
        out_shape=jax.ShapeDtypeStruct((D_ATTN, n_req), F32),
        compiler_params=_params(1),
        name="attn_sample",
    )(page_table.reshape(-1), q.T, k_new.T, v_new.T, slopes_rows, ck, cv)
    return out_t.T


def _onehots(e, lane):
    return [(e[:, kk:kk + 1] == lane) for kk in range(TOP_K)]


def _expert_ranks(e, carry_ref):
    tm = e.shape[0]
    lane = lax.broadcasted_iota(I32, (tm, LANES), 1)
    hots = _onehots(e, lane)
    hot = jnp.zeros((tm, LANES), F32)
    for h in hots:
        hot = hot + h.astype(F32)
    r = lax.broadcasted_iota(I32, (tm, tm), 0)
    c = lax.broadcasted_iota(I32, (tm, tm), 1)
    earlier = (c < r).astype(BF16)
    before = carry_ref[...] + jnp.dot(earlier, hot.astype(BF16), preferred_element_type=F32)
    rank = jnp.zeros((tm, LANES), F32)
    for kk, h in enumerate(hots):
        rank = jnp.where(lane == kk, jnp.sum(jnp.where(h, before, 0.0), axis=1, keepdims=True), rank)
    carry_ref[...] = carry_ref[...] + jnp.sum(hot, axis=0, keepdims=True)
    return rank.astype(I32)


def _branch_merge(y_conv, x, attn_bf, sg, mod, w, outs, carry_ref):
    (b_dw, ln_g, ln_b, w_ao, w_pw2, w_out, n2g, w_router, b_router, count_in) = w
    xo_ref, h2_ref, eidx_ref, gate_ref, rank_ref, count_ref = outs

    @pl.when(pl.program_id(0) == 0)
    def _():
        carry_ref[...] = count_in[0:1, :]
    y = y_conv + b_dw[...]
    mu = jnp.mean(y, axis=-1, keepdims=True)
    yc = y - mu
    var = jnp.mean(yc * yc, axis=-1, keepdims=True)
    y = (yc * lax.rsqrt(var + EPS)) * ln_g[...] + ln_b[...]
    y = y * _sigmoid(y)
    c_out = jnp.dot(y.astype(BF16), w_pw2[...], preferred_element_type=F32)
    a_out = jnp.dot(attn_bf, w_ao[...], preferred_element_type=F32)
    merged = sg[:, 0:D_MODEL] * a_out + sg[:, D_MODEL:2 * D_MODEL] * c_out
    g1 = mod[:, 2 * D_MODEL:3 * D_MODEL]
    xo = x + g1 * jnp.dot(merged.astype(BF16), w_out[...], preferred_element_type=F32)
    xo_ref[...] = xo
    sh2, sc2 = mod[:, 3 * D_MODEL:4 * D_MODEL], mod[:, 4 * D_MODEL:5 * D_MODEL]
    h2 = _rms(xo, n2g[...]) * (1.0 + sc2) + sh2
    _rows_to_tiles(h2_ref, h2)

    h2_hi = h2.astype(BF16)
    h2_lo = (h2 - h2_hi.astype(F32)).astype(BF16)
    logits = (jnp.dot(h2_hi, w_router[0], preferred_element_type=F32)
              + jnp.dot(h2_lo, w_router[0], preferred_element_type=F32)
              + jnp.dot(h2_hi, w_router[1], preferred_element_type=F32)) + b_router[...]
    tm = logits.shape[0]
    lane = lax.broadcasted_iota(I32, (tm, LANES), 1)
    logits = jnp.where(lane < N_EXPERTS, logits, NEG_INF)
    eidx = lane.astype(F32)
    vals, e_out = [], jnp.zeros((tm, LANES), F32)
    for kk in range(TOP_K):
        mx = jnp.max(logits, axis=1, keepdims=True)
        first = jnp.min(jnp.where(logits == mx, eidx, float(N_EXPERTS)), axis=1, keepdims=True)
        logits = jnp.where(eidx == first, NEG_INF, logits)
        vals.append(mx)
        e_out = jnp.where(lane == kk, first, e_out)
    ex = [jnp.exp(v - vals[0]) for v in vals]
    den = ex[0] + ex[1] + ex[2] + ex[3]
    g_out = jnp.zeros((tm, LANES), F32)
    for kk in range(TOP_K):
        g_out = jnp.where(lane == kk, ex[kk] / den, g_out)
    experts = e_out.astype(I32)
    eidx_ref[...] = experts
    gate_ref[...] = g_out
    rank_ref[...] = _expert_ranks(experts, carry_ref)
    count_ref[...] = jnp.broadcast_to(carry_ref[...], count_ref.shape)


def _merge_prompt_kernel(x_ref, attn_ref, u_ref, halo_ref, sg_ref, mod_ref, wdw_ref, *refs, tiles_per_seq):
    w, outs, (full_ref, shifted_ref, carry_ref) = refs[:10], refs[10:16], refs[16:]
    tm = u_ref.shape[0]
    first = (pl.program_id(0) % tiles_per_seq) == 0
    full_ref[0:HALO, :] = jnp.where(first, 0.0, halo_ref[...])
    full_ref[HALO:HALO + tm, :] = u_ref[...]
    base = HALO - (CONV_WIDTH - 1)
    y = jnp.zeros((tm, D_CONV), F32)
    for phase in range(SUBLANES):
        taps = [j for j in range(CONV_WIDTH) if (base + j) % SUBLANES == phase]
        rows = max(base + j for j in taps) - phase + tm
        shifted_ref[phase, 0:rows, :] = full_ref[phase:phase + rows, :]
        for j in taps:
            lo = base + j - phase
            y = y + wdw_ref[j:j + 1, :] * shifted_ref[phase, lo:lo + tm, :]
    _branch_merge(y, x_ref[...], attn_ref[...], sg_ref[...], mod_ref[0], w, outs, carry_ref)


def _merge_sample_kernel(x_ref, attn_ref, u_ref, state_ref, sg_ref, mod_ref, wdw_ref, *refs):
    w, outs, (carry_ref,) = refs[:10], refs[10:16], refs[16:]
    y = wdw_ref[CONV_WIDTH - 1:CONV_WIDTH, :] * u_ref[...]
    for j in range(CONV_WIDTH - 1):
        y = y + wdw_ref[j:j + 1, :] * state_ref[j]
    _branch_merge(y, x_ref[...], attn_ref[...].astype(BF16), sg_ref[...], mod_ref[...], w, outs, carry_ref)


def _merge(x, attn, u, hist, sg, mod, weights, *, per_token, tm, tiles_per_seq):
    n = x.shape[0]
    row = lambda i: (i, 0)
    if per_token:
        kernel = _merge_sample_kernel
        hist_spec = _const_spec(hist.shape)
        scratch = []
    else:
        kernel = functools.partial(_merge_prompt_kernel, tiles_per_seq=tiles_per_seq)
        hist_spec = pl.BlockSpec((HALO, D_CONV), lambda i: (jnp.maximum(i * (tm // HALO) - 1, 0), 0))
        scratch = [pltpu.VMEM((HALO + tm, D_CONV), F32), pltpu.VMEM((SUBLANES, HALO + tm, D_CONV), F32)]
    scratch.append(pltpu.VMEM((1, LANES), F32))
    flat = lambda w, dt: (jax.ShapeDtypeStruct((n, w), dt), pl.BlockSpec((tm, w), row))
    pairs = [flat(D_MODEL, F32),
             (jax.ShapeDtypeStruct(_tiled(n), F32), pl.BlockSpec(_tiled(tm), row)),
             flat(LANES, I32), flat(LANES, F32), flat(LANES, I32),
             (jax.ShapeDtypeStruct((8, LANES), F32), pl.BlockSpec((8, LANES), lambda i: (0, 0)))]
    return pl.pallas_call(
        kernel,
        grid=(n // tm,),
        in_specs=[pl.BlockSpec((tm, D_MODEL), row), pl.BlockSpec((tm, D_ATTN), row),
                  pl.BlockSpec((tm, D_CONV), row), hist_spec, pl.BlockSpec((tm, 2 * D_MODEL), row),
                  _mod_spec(per_token, tm, tiles_per_seq)] + [_const_spec(a.shape) for a in weights],
        out_specs=[p[1] for p in pairs],
        out_shape=[p[0] for p in pairs],
        scratch_shapes=scratch,
        compiler_params=_params(1),
        name="merge",
    )(x, attn, u, hist, sg, mod, *weights)


def _slot_kernel(e_ref, rank_ref, count_ref, dest_ref, blk_ref, info_ref, *, n_blocks_pad):
    tm = e_ref.shape[0]
    counts = count_ref[0:1, :].astype(I32)
    padded = ((counts + (EXPERT_BLOCK - 1)) // EXPERT_BLOCK) * EXPERT_BLOCK
    r = lax.broadcasted_iota(I32, (LANES, LANES), 0)
    c = lax.broadcasted_iota(I32, (LANES, LANES), 1)
    pstart = jnp.dot(jnp.broadcast_to(padded.astype(F32), (8, LANES)), (r < c).astype(F32),
                     precision=lax.Precision.HIGHEST, preferred_element_type=F32)[0:1]
    pend = pstart + padded.astype(F32)
    lane = lax.broadcasted_iota(I32, (tm, LANES), 1)
    e = e_ref[...]
    dest = rank_ref[...]
    for kk, h in enumerate(_onehots(e, lane)):
        off = jnp.sum(jnp.where(h, pstart, 0.0), axis=1, keepdims=True).astype(I32)
        dest = dest + jnp.where(lane == kk, off, 0)
    dest_ref[...] = dest

    lane1 = lax.broadcasted_iota(I32, (1, LANES), 1)
    used = jnp.max(pend, axis=1, keepdims=True)
    blk_start = (lax.broadcasted_iota(I32, (n_blocks_pad, 1), 0) * EXPERT_BLOCK).astype(F32)
    blk_start = jnp.minimum(blk_start, used - EXPERT_BLOCK)
    done = jnp.where((lane1 < N_EXPERTS) & (pend <= blk_start), 1.0, 0.0)
    blk_exp = jnp.minimum(jnp.sum(done, axis=1, keepdims=True), N_EXPERTS - 1.0)
    blk_ref[...] = jnp.broadcast_to(blk_exp, blk_ref.shape).astype(I32)
    row8 = lax.broadcasted_iota(I32, (8, LANES), 0)
    info = jnp.where(row8 == 0, pstart, jnp.where(row8 == 1, pend, jnp.where(row8 == 2, used, 0.0)))
    info_ref[...] = info.astype(I32)


def _slots(eidx, rank, counts, tm, n_blocks):
    n = eidx.shape[0]
    row = lambda i: (i, 0)
    fixed = lambda i: (0, 0)
    n_blocks_pad = -(-n_blocks // 8) * 8
    dest, blk, info = pl.pallas_call(
        functools.partial(_slot_kernel, n_blocks_pad=n_blocks_pad),
        grid=(n // tm,),
        in_specs=[pl.BlockSpec((tm, LANES), row), pl.BlockSpec((tm, LANES), row),
                  pl.BlockSpec((8, LANES), fixed)],
        out_specs=[pl.BlockSpec((tm, LANES), row), pl.BlockSpec((n_blocks_pad, LANES), fixed),
                   pl.BlockSpec((8, LANES), fixed)],
        out_shape=[jax.ShapeDtypeStruct((n, LANES), I32), jax.ShapeDtypeStruct((n_blocks_pad, LANES), I32),
                   jax.ShapeDtypeStruct((8, LANES), I32)],
        compiler_params=_params(1),
        name="moe_slot",
    )(eidx, rank, counts)
    dest_flat = dest[:, :TOP_K].reshape(-1)
    blk_exp = blk[:n_blocks, 0]
    pend = info[1, :N_EXPERTS]
    n_used = jnp.right_shift(info[2, 0:1], EXPERT_BLOCK.bit_length() - 1)
    return dest_flat, blk_exp, pend, n_used


def _for_each_row(n_rows, fn):
    @pl.loop(0, n_rows // SUBLANES)
    def _(g):
        first = pl.multiple_of(g * SUBLANES, SUBLANES)
        for s in range(SUBLANES):
            fn(first + s)


def _start_all(copies):
    for i, c in enumerate(copies):
        c.start(priority=i % 2)


def _row_copy(src_ref, src_row, dst_ref, dst_row, sem):
    tile = lambda row: pl.ds(pl.multiple_of(row * ROW_PITCH, ROW_PITCH), ROW_PITCH)
    return pltpu.make_async_copy(src_ref.at[tile(src_row)], dst_ref.at[tile(dst_row)], sem)


def _dispatch_kernel(dest_ref, pend_ref, h_ref, tail_ref, xs_ref, zero_ref, sem):
    @pl.when(pl.program_id(0) == 0)
    def _():
        zero_ref[...] = jnp.zeros_like(zero_ref)
        block_rows = zero_ref.shape[0]
        n_blocks = xs_ref.shape[0] // block_rows
        first_unused = pend_ref[N_EXPERTS - 1] // EXPERT_BLOCK

        def fill(start):
            rows = pl.ds(pl.multiple_of(start * ROW_PITCH, block_rows), block_rows)
            return pltpu.make_async_copy(zero_ref, xs_ref.at[rows], sem)

        def last_block(e):
            return jnp.maximum(pend_ref[e] - EXPERT_BLOCK, 0)

        @pl.loop(0, N_EXPERTS)
        def _(e):
            fill(last_block(e)).start()

        @pl.loop(first_unused, n_blocks)
        def _(j):
            fill(j * EXPERT_BLOCK).start()

        @pl.loop(0, N_EXPERTS)
        def _(e):
            fill(last_block(e)).wait()

        @pl.loop(first_unused, n_blocks)
        def _(j):
            fill(j * EXPERT_BLOCK).wait()

    def scatter(h_ref):
        def copies(t):
            return [_row_copy(h_ref, t, xs_ref, dest_ref[t * TOP_K + kk], sem) for kk in range(TOP_K)]

        n_rows = h_ref.shape[0] // ROW_PITCH
        _for_each_row(n_rows, lambda t: _start_all(copies(t)))
        _for_each_row(n_rows, lambda t: [c.wait() for c in copies(t)])

    is_tail = pl.program_id(0) == pl.num_programs(0) - 1

    @pl.when(jnp.logical_not(is_tail))
    def _():
        scatter(h_ref)

    @pl.when(is_tail)
    def _():
        scatter(tail_ref)


def _dispatch(h2, h2_tail, dest_flat, pend, tm, cap):
    n_tiles = h2.shape[0] // _tiled(tm)[0]
    assert h2_tail.shape[0] <= _tiled(tm)[0] and dest_flat.shape[0] == (n_tiles + 1) * tm * TOP_K
    return pl.pallas_call(
        _dispatch_kernel,
        grid=(n_tiles + 1,),
        in_specs=[pl.BlockSpec((tm * TOP_K,), lambda i: (i,), memory_space=pltpu.SMEM),
                  pl.BlockSpec(memory_space=pltpu.SMEM),
                  pl.BlockSpec(_tiled(tm), lambda i: (jnp.minimum(i, n_tiles - 1), 0)),
                  _const_spec(h2_tail.shape)],
        out_specs=pl.BlockSpec(memory_space=pl.ANY),
        out_shape=jax.ShapeDtypeStruct(_tiled(cap), F32),
        scratch_shapes=[pltpu.VMEM(_tiled(EXPERT_BLOCK), F32), pltpu.SemaphoreType.DMA(())],
        compiler_params=_params(1, has_side_effects=True),
        name="moe_dispatch",
    )(dest_flat, pend, h2, h2_tail)


def _expert_kernel(blk_ref, used_ref, x_ref, wg_ref, wu_ref, wd_ref, y_ref, wg_bf, wu_bf, wd_bf):
    j = pl.program_id(0)
    active = j < used_ref[0]
    changed = (j == 0) | (blk_ref[j] != blk_ref[jnp.maximum(j - 1, 0)])

    @pl.when(active & changed)
    def _():
        wg_bf[...] = wg_ref[0].astype(BF16)
        wu_bf[...] = wu_ref[0].astype(BF16)
        wd_bf[...] = wd_ref[0].astype(BF16)

    @pl.when(active)
    def _():
        x = _tiles_to_rows(x_ref).astype(BF16)
        gt = jnp.minimum(jnp.dot(x, wg_bf[...], preferred_element_type=F32), SWIGLU_LIMIT)
        up = jnp.clip(jnp.dot(x, wu_bf[...], preferred_element_type=F32), -SWIGLU_LIMIT, SWIGLU_LIMIT)
        act = gt * _sigmoid(SWIGLU_ALPHA * gt) * (up + 1.0)
        _rows_to_tiles(y_ref, jnp.dot(act.astype(BF16), wd_bf[...], preferred_element_type=F32))

    @pl.when(jnp.logical_not(active))
    def _():
        y_ref[...] = jnp.zeros_like(y_ref)


def _experts(xs, blk_exp, n_used, w_gate, w_up, w_down):
    n_blocks = blk_exp.shape[0]
    d_ff = w_gate.shape[2]
    xrow = lambda j, blk, used: (jnp.minimum(j, used[0] - 1), 0)
    wsel = lambda j, blk, used: (blk[j], 0, 0)
    grid_spec = pltpu.PrefetchScalarGridSpec(
        num_scalar_prefetch=2,
        grid=(n_blocks,),
        in_specs=[pl.BlockSpec(_tiled(EXPERT_BLOCK), xrow),
                  pl.BlockSpec((1, D_MODEL, d_ff), wsel),
                  pl.BlockSpec((1, D_MODEL, d_ff), wsel),
                  pl.BlockSpec((1, d_ff, D_MODEL), wsel)],
        out_specs=pl.BlockSpec(_tiled(EXPERT_BLOCK), lambda j, blk, used: (j, 0)),
        scratch_shapes=[pltpu.VMEM((D_MODEL, d_ff), BF16), pltpu.VMEM((D_MODEL, d_ff), BF16),
                        pltpu.VMEM((d_ff, D_MODEL), BF16)],
    )
    return pl.pallas_call(
        _expert_kernel,
        grid_spec=grid_spec,
        out_shape=jax.ShapeDtypeStruct(xs.shape, F32),
        compiler_params=_params(1),
        name="moe_experts",
    )(blk_exp, n_used, xs, w_gate, w_up, w_down)


def _combine_kernel(dest_ref, next_ref, y_ref, xo_ref, gate_ref, mod_ref, fg_ref, o_ref, buf_ref, sem, *, per_token):
    tm = xo_ref.shape[0]
    step = pl.program_id(0)
    slot = step % 2

    def copies(slots_ref, half, t):
        return [_row_copy(y_ref, slots_ref[t * TOP_K + kk], buf_ref.at[half, kk], t, sem.at[half])
                for kk in range(TOP_K)]

    @pl.when(step == 0)
    def _():
        _for_each_row(tm, lambda t: _start_all(copies(dest_ref, 0, t)))

    @pl.when(step + 1 < pl.num_programs(0))
    def _():
        _for_each_row(tm, lambda t: _start_all(copies(next_ref, 1 - slot, t)))

    _for_each_row(tm, lambda t: [c.wait() for c in copies(dest_ref, slot, t)])

    gates = gate_ref[...]
    moe = gates[:, 0:1] * _tiles_to_rows(buf_ref.at[slot, 0])
    for kk in range(1, TOP_K):
        moe = moe + gates[:, kk:kk + 1] * _tiles_to_rows(buf_ref.at[slot, kk])
    g2 = _load_mod(mod_ref, per_token)[:, 5 * D_MODEL:6 * D_MODEL]
    o_ref[...] = _rms(xo_ref[...] + g2 * moe, fg_ref[...])


def _combine(y, dest_flat, xo, gates, mod, final_g, *, per_token, tm, tiles_per_seq):
    n = xo.shape[0]
    n_tiles = n // tm
    row = lambda i: (i, 0)
    return pl.pallas_call(
        functools.partial(_combine_kernel, per_token=per_token),
        grid=(n_tiles,),
        in_specs=[pl.BlockSpec((tm * TOP_K,), lambda i: (i,), memory_space=pltpu.SMEM),
                  pl.BlockSpec((tm * TOP_K,), lambda i: (jnp.minimum(i + 1, n_tiles - 1),), memory_space=pltpu.SMEM),
                  pl.BlockSpec(memory_space=pl.ANY),
                  pl.BlockSpec((tm, D_MODEL), row), pl.BlockSpec((tm, LANES), row),
                  _mod_spec(per_token, tm, tiles_per_seq), _const_spec((1, D_MODEL))],
        out_specs=pl.BlockSpec((tm, D_MODEL), row),
        out_shape=jax.ShapeDtypeStruct((n, D_MODEL), F32),
        scratch_shapes=[pltpu.VMEM((2, TOP_K) + _tiled(tm), F32), pltpu.SemaphoreType.DMA((2,))],
        compiler_params=_params(1),
        name="moe_combine",
    )(dest_flat, dest_flat, y, xo, gates, mod, final_g)


def _moe(prompt, sample, counts, final_g, w_gate, w_up, w_down, *, tm, tiles_per_seq):
    n_p, n_s = prompt[1].shape[0], sample[1].shape[0]
    n_blocks = -(-(n_p + n_s) * TOP_K // EXPERT_BLOCK) + N_EXPERTS
    dest_p, blk_exp, pend, n_used = _slots(prompt[1], prompt[3], counts, SLOT_TILE, n_blocks)
    dest_s, _, _, _ = _slots(sample[1], sample[3], counts, n_s, n_blocks)
    dest_all = jnp.concatenate([dest_p, dest_s, jnp.zeros(((tm - n_s) * TOP_K,), I32)])
    xs = _dispatch(prompt[0], sample[0], dest_all, pend, tm, n_blocks * EXPERT_BLOCK)
    y = _experts(xs, blk_exp, n_used, w_gate, w_up, w_down)
    y_p = _combine(y, dest_p, prompt[4], prompt[2], prompt[5], final_g,
                   per_token=False, tm=tm, tiles_per_seq=tiles_per_seq)
    y_s = _combine(y, dest_s, sample[4], sample[2], sample[5], final_g,
                   per_token=True, tm=n_s, tiles_per_seq=1)
    return y_p, y_s


def kernel(x_prompt, x_sample, c_prompt, c_sample, cache_k, cache_v, state_conv, page_table, norm1_g, norm2_g,
           w_ada, b_ada, w_in, w_attn_out, w_dw, b_dw, ln_g, ln_b, w_pw2, w_out, w_router, b_router, w_gate,
           w_up, w_down, final_g):
    depth = w_in.shape[0]
    assert depth == 1, "single-layer trunk"
    batch, seq, _ = x_prompt.shape
    n_req = x_sample.shape[0]
    assert x_sample.shape[1] == 1 and seq % MOBA_BLOCK == 0 and n_req % 8 == 0
    layer = 0
    row2 = lambda a: a.reshape(1, -1)

    pad = (-batch) % 8
    c_all = jnp.concatenate([c_prompt, jnp.zeros((pad, D_MODEL), F32), c_sample], axis=0)
    ada = _ada(c_all, w_ada[layer], b_ada[layer])
    mod_p = ada[:batch].reshape(batch, 1, 6 * D_MODEL)
    mod_s = ada[batch + pad:]

    w_in_bf = w_in[layer].astype(BF16)
    merge_w = (row2(b_dw[layer]), row2(ln_g[layer]), row2(ln_b[layer]), w_attn_out[layer].astype(BF16),
               w_pw2[layer].astype(BF16), w_out[layer].astype(BF16), row2(norm2_g[layer]),
               _split_bf16(jnp.pad(w_router[layer], ((0, 0), (0, LANES - N_EXPERTS)))),
               jnp.pad(row2(b_router[layer]), ((0, 0), (0, LANES - N_EXPERTS))))
    w_dw_pad = jnp.pad(w_dw[layer], ((0, HALO - CONV_WIDTH), (0, 0)))
    slopes = jnp.exp2(-8.0 * jnp.arange(1, N_HEADS + 1, dtype=F32) / N_HEADS)
    slopes_rows = jnp.broadcast_to(slopes[:, None], (N_HEADS, LANES))
    fg = row2(final_g)
    experts = (w_gate[layer], w_up[layer], w_down[layer])

    tm = MOBA_BLOCK
    tps = seq // tm
    xp = x_prompt.reshape(batch * seq, D_MODEL)
    w_kvt_bf = w_in_bf[:, D_ATTN:3 * D_ATTN].T
    q, k_bf, kt, vt, vt_bf, u, sg, kmean = _inproj(xp, mod_p, row2(norm1_g[layer]), w_in_bf, w_kvt_bf,
                                                   per_token=False, tm=tm, tiles_per_seq=tps)
    attn = _attend_prompt(q, k_bf, vt_bf, kmean, slopes, batch, seq)
    no_counts = jnp.zeros((8, LANES), F32)
    xo, h2, eidx, gates, rank, counts_p = _merge(xp, attn, u, u, sg, mod_p, (w_dw_pad,) + merge_w + (no_counts,),
                                                 per_token=False, tm=tm, tiles_per_seq=tps)
    to_cache = lambda t: t.reshape(batch, N_HEADS, HEAD_DIM, seq).transpose(0, 3, 1, 2)[None]
    k_prompt, v_prompt = to_cache(kt), to_cache(vt)
    conv_prompt = u.reshape(batch, seq, D_CONV)[None, :, seq - (CONV_WIDTH - 1):]

    xs = x_sample.reshape(n_req, D_MODEL)
    qs, ks, vs, us, sgs = _inproj(xs, mod_s, row2(norm1_g[layer]), w_in_bf,
                                  per_token=True, tm=n_req, tiles_per_seq=1)
    attn_s = _attend_sample(qs, ks, vs, cache_k[layer], cache_v[layer], page_table, slopes_rows)
    state = state_conv[layer]
    xo_s, h2_s, eidx_s, gates_s, rank_s, counts = _merge(
        xs, attn_s, us, state.transpose(1, 0, 2), sgs, mod_s, (w_dw_pad,) + merge_w + (counts_p,),
        per_token=True, tm=n_req, tiles_per_seq=1)

    y_prompt, y_sample = _moe((h2, eidx, gates, rank, xo, mod_p), (h2_s, eidx_s, gates_s, rank_s, xo_s, mod_s),
                              counts, fg, *experts, tm=tm, tiles_per_seq=tps)
    k_sample = ks.reshape(1, n_req, 1, N_HEADS, HEAD_DIM)
    v_sample = vs.reshape(1, n_req, 1, N_HEADS, HEAD_DIM)
    conv_sample = jnp.concatenate([state[:, 1:], us[:, None, :]], axis=1)[None]

    return (y_prompt.reshape(batch, seq, D_MODEL), y_sample.reshape(n_req, 1, D_MODEL),
            k_prompt, v_prompt, conv_prompt, k_sample, v_sample, conv_sample)
```

```python
import functools

import jax
import jax.numpy as jnp
from jax import lax
from jax.experimental import pallas as pl
from jax.experimental.pallas import tpu as pltpu

F32 = jnp.float32
BF16 = jnp.bfloat16
I32 = jnp.int32

D_MODEL = 1024
N_HEADS = 8
HEAD_DIM = 64
D_ATTN = N_HEADS * HEAD_DIM
D_CONV = 512
CONV_WIDTH = 31
N_EXPERTS = 32
TOP_K = 4
MOBA_BLOCK = 256
MOBA_TOPK = 3
PAGE_SIZE = 128
SWIGLU_ALPHA = 1.702
SWIGLU_LIMIT = 7.0
EPS = 1e-6
D_IN = 3 * D_ATTN + 2 * D_CONV + 2 * D_MODEL

LANES = 128
SUBLANES = 8
HALO = 32
EXPERT_BLOCK = 512
SLOT_TILE = 1024
HEADS_PER_TILE = LANES // HEAD_DIM
KV_GROUP = 2
VMEM_LIMIT = 56 * 1024 * 1024

NT_DIMS = (((1,), (1,)), ((), ()))
NEG_INF = float("-inf")
LOG2E = 1.4426950408889634
MAX_FLOOR = -1e30


def _params(n_axes, **kw):
    return pltpu.CompilerParams(dimension_semantics=("arbitrary",) * n_axes,
                                vmem_limit_bytes=VMEM_LIMIT, **kw)


def _const_spec(shape):
    nd = len(shape)
    return pl.BlockSpec(shape, lambda *_: (0,) * nd, pipeline_mode=pl.Buffered(1))


def _rms(x, g):
    return (x * lax.rsqrt(jnp.mean(x * x, axis=-1, keepdims=True) + EPS)) * g


def _sigmoid(x):
    return 1.0 / (1.0 + jnp.exp(-x))


ROW_PITCH = D_MODEL // LANES


def _tiled(n_rows):
    return (n_rows * ROW_PITCH, LANES)


def _rows_to_tiles(ref, x):
    for c in range(ROW_PITCH):
        ref[pl.ds(c, x.shape[0], stride=ROW_PITCH), :] = x[:, c * LANES:(c + 1) * LANES]


def _tiles_to_rows(ref):
    r = ref.shape[0] // ROW_PITCH
    return jnp.concatenate([ref[pl.ds(c, r, stride=ROW_PITCH), :] for c in range(ROW_PITCH)], axis=1)


def _split_bf16(w):
    hi = w.astype(BF16)
    return jnp.stack([hi, (w - hi.astype(F32)).astype(BF16)])


def _ada_kernel(c_ref, w_ref, b_ref, o_ref):
    o_ref[...] = jnp.dot(c_ref[...].astype(BF16), w_ref[...].astype(BF16),
                         preferred_element_type=F32) + b_ref[...]


def _ada(c, w_ada, b_ada):
    rows = c.shape[0]
    return pl.pallas_call(
        _ada_kernel,
        grid=(6,),
        in_specs=[pl.BlockSpec((rows, D_MODEL), lambda j: (0, 0)),
                  pl.BlockSpec((D_MODEL, D_MODEL), lambda j: (0, j)),
                  pl.BlockSpec((1, D_MODEL), lambda j: (0, j))],
        out_specs=pl.BlockSpec((rows, D_MODEL), lambda j: (0, j)),
        out_shape=jax.ShapeDtypeStruct((rows, 6 * D_MODEL), F32),
        compiler_params=_params(1),
        name="ada",
    )(c, w_ada, b_ada.reshape(1, -1))


def _mod_spec(per_token, tm, tiles_per_seq):
    if per_token:
        return pl.BlockSpec((tm, 6 * D_MODEL), lambda i: (i, 0))
    return pl.BlockSpec((1, 1, 6 * D_MODEL), lambda i: (i // tiles_per_seq, 0, 0))


def _load_mod(mod_ref, per_token):
    return mod_ref[...] if per_token else mod_ref[0]


def _inproj_common(x_ref, mod_ref, g_ref, w_ref, u_ref, sg_ref, per_token):
    mod = _load_mod(mod_ref, per_token)
    sh1, sc1 = mod[:, 0:D_MODEL], mod[:, D_MODEL:2 * D_MODEL]
    h = (_rms(x_ref[...], g_ref[...]) * (1.0 + sc1) + sh1).astype(BF16)

    def proj(lo, hi):
        return jnp.dot(h, w_ref[:, lo:hi], preferred_element_type=F32)

    o = 3 * D_ATTN
    u_ref[...] = proj(o, o + D_CONV) * _sigmoid(proj(o + D_CONV, o + 2 * D_CONV))
    o += 2 * D_CONV
    sg_ref[...] = _sigmoid(proj(o, o + 2 * D_MODEL))
    return h, proj


def _inproj_sample_kernel(x_ref, mod_ref, g_ref, w_ref, q_ref, k_ref, v_ref, u_ref, sg_ref):
    _, proj = _inproj_common(x_ref, mod_ref, g_ref, w_ref, u_ref, sg_ref, True)
    q_ref[...] = proj(0, D_ATTN)
    k_ref[...] = proj(D_ATTN, 2 * D_ATTN)
    v_ref[...] = proj(2 * D_ATTN, 3 * D_ATTN)


def _inproj_prompt_kernel(x_ref, mod_ref, g_ref, w_ref, wt_ref, q_ref, kb_ref, kt_ref, vt_ref, vtb_ref,
                          u_ref, sg_ref, km_ref):
    h, proj = _inproj_common(x_ref, mod_ref, g_ref, w_ref, u_ref, sg_ref, False)
    q_ref[...] = proj(0, D_ATTN)
    k = proj(D_ATTN, 2 * D_ATTN)
    kb_ref[...] = k.astype(BF16)
    km_ref[0] = jnp.mean(k, axis=0, keepdims=True)
    kt_ref[0] = lax.dot_general(wt_ref[0:D_ATTN, :], h, NT_DIMS, preferred_element_type=F32)
    vt = lax.dot_general(wt_ref[D_ATTN:2 * D_ATTN, :], h, NT_DIMS, preferred_element_type=F32)
    vt_ref[0] = vt
    vtb_ref[0] = vt.astype(BF16)


def _inproj(x, mod, norm_g, w_in_bf, w_kvt_bf=None, *, per_token, tm, tiles_per_seq):
    n = x.shape[0]
    row = lambda i: (i, 0)
    in_specs = [pl.BlockSpec((tm, D_MODEL), row), _mod_spec(per_token, tm, tiles_per_seq),
                _const_spec((1, D_MODEL)), _const_spec((D_MODEL, D_IN))]
    tail = [(D_CONV, F32), (2 * D_MODEL, F32)]
    if per_token:
        kernel, args = _inproj_sample_kernel, (x, mod, norm_g, w_in_bf)
        outs = [(D_ATTN, F32)] * 3 + tail
        out_shape = [jax.ShapeDtypeStruct((n, w), dt) for w, dt in outs]
        out_specs = [pl.BlockSpec((tm, w), row) for w, _ in outs]
    else:
        assert tm == MOBA_BLOCK
        kernel, args = _inproj_prompt_kernel, (x, mod, norm_g, w_in_bf, w_kvt_bf)
        in_specs.append(_const_spec((2 * D_ATTN, D_MODEL)))
        batch = n // (tm * tiles_per_seq)
        seq = tm * tiles_per_seq
        tok = lambda w, dt: (jax.ShapeDtypeStruct((n, w), dt), pl.BlockSpec((tm, w), row))
        tr = lambda dt: (jax.ShapeDtypeStruct((batch, D_ATTN, seq), dt),
                         pl.BlockSpec((1, D_ATTN, tm), lambda i: (i // tiles_per_seq, 0, i % tiles_per_seq)))
        km = (jax.ShapeDtypeStruct((n // tm, 1, D_ATTN), F32), pl.BlockSpec((1, 1, D_ATTN), lambda i: (i, 0, 0)))
        pairs = [tok(D_ATTN, F32), tok(D_ATTN, BF16), tr(F32), tr(F32), tr(BF16)] + [tok(*t) for t in tail] + [km]
        out_shape, out_specs = [p[0] for p in pairs], [p[1] for p in pairs]
    return pl.pallas_call(
        kernel,
        grid=(n // tm,),
        in_specs=in_specs,
        out_specs=out_specs,
        out_shape=out_shape,
        compiler_params=_params(1),
        name="inproj",
    )(*args)


def _select_top(g, idx, n_pick, n_idx, axis):
    sel = jnp.zeros(g.shape, F32)
    for _ in range(n_pick):
        mx = jnp.max(g, axis=axis, keepdims=True)
        first = jnp.min(jnp.where(g == mx, idx, float(n_idx)), axis=axis, keepdims=True)
        hit = idx == first
        sel = jnp.where(hit & (mx > NEG_INF), 1.0, sel)
        g = jnp.where(hit, NEG_INF, g)
    return sel


def _attn_kernel(slopes_ref, q_ref, k_ref, vt_ref, km_ref, o_ref, bias_ref, sel_ref, sa_ref, sb_ref):
    pair, own = pl.program_id(1), pl.program_id(2)
    blk = MOBA_BLOCK
    n_blk = km_ref.shape[1]

    @pl.when(own == 0)
    def _():
        key = lax.broadcasted_iota(I32, (blk, blk), 0)
        qry = lax.broadcasted_iota(I32, (blk, blk), 1)
        rel = (qry - key).astype(F32)
        for hh in range(HEADS_PER_TILE):
            bias = -(slopes_ref[pair * HEADS_PER_TILE + hh] * LOG2E) * rel
            bias_ref[0, hh] = bias
            bias_ref[1, hh] = jnp.where(key <= qry, bias, NEG_INF)

    q = q_ref[...]
    lane = lax.broadcasted_iota(I32, (1, LANES), 1)
    km = km_ref[0]
    bidx = lax.broadcasted_iota(I32, (n_blk, blk), 0)
    bidx_f = bidx.astype(F32)

    qhs = [jnp.where((lane // HEAD_DIM) == hh, q, 0.0) for hh in range(HEADS_PER_TILE)]
    qs = [(qh * (HEAD_DIM ** -0.5 * LOG2E)).astype(BF16) for qh in qhs]
    span = KV_GROUP * blk

    def score_stage(dst_ref, n0):
        n0 = jnp.minimum(n0, n_blk - KV_GROUP)
        kb = k_ref[pl.ds(pl.multiple_of(n0 * blk, span), span), :]
        for hh in range(HEADS_PER_TILE):
            dst_ref[hh] = lax.dot_general(kb, qs[hh], NT_DIMS, preferred_element_type=F32)

    score_stage(sa_ref, 0)

    for hh in range(HEADS_PER_TILE):
        gate = lax.dot_general(km, qhs[hh], NT_DIMS, precision=lax.Precision.HIGHEST,
                               preferred_element_type=F32)
        gate = jnp.where(bidx < own, gate, NEG_INF)
        sel = _select_top(gate, bidx_f, MOBA_TOPK, n_blk, 0)
        sel_ref[hh] = jnp.where(bidx == own, 1.0, sel)

    def softmax_stage(src_ref, n0, carry):
        vtb = vt_ref[0, :, pl.ds(pl.multiple_of(n0 * blk, span), span)]
        new = []
        for hh in range(HEADS_PER_TILE):
            m, l, acc = carry[3 * hh:3 * hh + 3]
            slope = slopes_ref[pair * HEADS_PER_TILE + hh] * LOG2E
            subs, m_new = [], m
            for j in range(KV_GROUP):
                n = n0 + j
                s = src_ref[hh, j * blk:(j + 1) * blk, :] + bias_ref[(n == own).astype(I32), hh]
                c = -slope * ((own - n) * blk).astype(F32)
                picked = sel_ref[hh, pl.ds(n, 1), :] > 0.0
                m_new = jnp.maximum(m_new, jnp.where(picked, jnp.max(s, axis=0, keepdims=True) + c, NEG_INF))
                subs.append((s, c, picked))
            alpha = jnp.exp2(m - m_new)
            l = alpha * l
            probs = []
            for s, c, picked in subs:
                p = jnp.exp2(s - jnp.where(picked, m_new - c, float("inf")))
                l = l + jnp.sum(p, axis=0, keepdims=True)
                probs.append(p.astype(BF16))
            pv = jnp.dot(vtb[hh * HEAD_DIM:(hh + 1) * HEAD_DIM, :], jnp.concatenate(probs, axis=0),
                         preferred_element_type=F32)
            new += [m_new, l, acc * alpha + pv]
        return tuple(new)

    def body(it, carry):
        n0 = it * (2 * KV_GROUP)
        score_stage(sb_ref, n0 + KV_GROUP)
        carry = softmax_stage(sa_ref, n0, carry)
        score_stage(sa_ref, n0 + 2 * KV_GROUP)
        return softmax_stage(sb_ref, n0 + KV_GROUP, carry)

    start = (jnp.full((1, blk), MAX_FLOOR, F32), jnp.zeros((1, blk), F32), jnp.zeros((HEAD_DIM, blk), F32))
    n_trips = own // (2 * KV_GROUP) + 1
    _, l0, acc0, _, l1, acc1 = lax.fori_loop(0, n_trips, body, start * HEADS_PER_TILE)
    out_t = jnp.concatenate([acc0 / l0, acc1 / l1], axis=0)
    o_ref[...] = out_t.T.astype(o_ref.dtype)


def _attend_prompt(q, k_bf, vt_bf, kmean, slopes, batch, seq):
    n_qb = seq // MOBA_BLOCK
    assert n_qb % (2 * KV_GROUP) == 0
    score_buf = pltpu.VMEM((HEADS_PER_TILE, KV_GROUP * MOBA_BLOCK, MOBA_BLOCK), F32)
    tile = lambda b, p, i: (b * n_qb + i, p)
    return pl.pallas_call(
        _attn_kernel,
        grid=(batch, N_HEADS // HEADS_PER_TILE, n_qb),
        in_specs=[pl.BlockSpec(memory_space=pltpu.SMEM),
                  pl.BlockSpec((MOBA_BLOCK, LANES), tile),
                  pl.BlockSpec((seq, LANES), lambda b, p, i: (b, p)),
                  pl.BlockSpec((1, LANES, seq), lambda b, p, i: (b, p, 0)),
                  pl.BlockSpec((1, n_qb, LANES), lambda b, p, i: (b, 0, p))],
        out_specs=pl.BlockSpec((MOBA_BLOCK, LANES), tile),
        out_shape=jax.ShapeDtypeStruct((batch * seq, D_ATTN), BF16),
        scratch_shapes=[pltpu.VMEM((2, HEADS_PER_TILE, MOBA_BLOCK, MOBA_BLOCK), F32),
                        pltpu.VMEM((HEADS_PER_TILE, n_qb, MOBA_BLOCK), F32), score_buf, score_buf],
        compiler_params=_params(3),
        name="attn_prompt",
    )(slopes, q, k_bf, vt_bf, kmean.reshape(batch, n_qb, D_ATTN))


def _decode_attn_kernel(pt_ref, qt_ref, knt_ref, vnt_ref, slope_ref, ck_ref, cv_ref, o_ref, kbuf, vbuf, sem,
                        *, n_pages):
    pages_per_blk = MOBA_BLOCK // PAGE_SIZE
    n_blk = n_pages // pages_per_blk
    past_len = n_pages * PAGE_SIZE
    req = pl.program_id(0)
    slot = req % 2
    mine = lax.broadcasted_iota(I32, (D_ATTN, LANES), 1) == req

    def page_copies(r, buf_slot):
        return [pltpu.make_async_copy(cache.at[pt_ref[r * n_pages + p]], buf.at[buf_slot, p], sem.at[buf_slot])
                for cache, buf in ((ck_ref, kbuf), (cv_ref, vbuf)) for p in range(n_pages)]

    @pl.when(req == 0)
    def _():
        o_ref[...] = jnp.zeros_like(o_ref)
        _start_all(page_copies(0, 0))

    @pl.when(req + 1 < pl.num_programs(0))
    def _():
        _start_all(page_copies(req + 1, 1 - slot))

    for c in page_copies(req, slot):
        c.wait()
    k_pages = [kbuf.at[slot, p] for p in range(n_pages)]
    v_pages = [vbuf.at[slot, p] for p in range(n_pages)]

    def column(ref):
        return jnp.sum(jnp.where(mine, ref[...], 0.0), axis=1, keepdims=True)

    def head_sums(x):
        head = lax.broadcasted_iota(I32, (N_HEADS, x.shape[1]), 0)
        out = jnp.zeros((N_HEADS, x.shape[1]), F32)
        for h in range(N_HEADS):
            part = jnp.sum(x[h * HEAD_DIM:(h + 1) * HEAD_DIM], axis=0, keepdims=True)
            out = jnp.where(head == h, part, out)
        return out

    def head_rows(x):
        return jnp.concatenate([jnp.broadcast_to(x[h:h + 1], (HEAD_DIM, x.shape[1]))
                                for h in range(N_HEADS)], axis=0)

    q, k_new, v_new = column(qt_ref), column(knt_ref), column(vnt_ref)
    raw = [head_sums(k_pages[p][...] * q) for p in range(n_pages)]

    gates = []
    for n in range(n_blk):
        tot = raw[n * pages_per_blk]
        for j in range(1, pages_per_blk):
            tot = tot + raw[n * pages_per_blk + j]
        gates.append(jnp.sum(tot, axis=1, keepdims=True) * (1.0 / MOBA_BLOCK))
    picked = []
    for n in range(n_blk):
        beaten = jnp.zeros_like(gates[n])
        for o in range(n_blk):
            if o != n:
                wins = (gates[o] >= gates[n]) if o < n else (gates[o] > gates[n])
                beaten = beaten + wins.astype(F32)
        picked.append(beaten < float(MOBA_TOPK))

    lane = lax.broadcasted_iota(I32, (N_HEADS, LANES), 1)
    slope = slope_ref[...]
    scale = HEAD_DIM ** -0.5
    scores = []
    for p in range(n_pages):
        dist = (past_len - p * PAGE_SIZE - lane).astype(F32)
        scores.append(jnp.where(picked[p // pages_per_blk], raw[p] * scale - slope * dist, NEG_INF))
    s_new = head_sums(q * k_new) * scale

    top = scores[0]
    for s in scores[1:]:
        top = jnp.maximum(top, s)
    m = jnp.maximum(s_new, jnp.max(top, axis=1, keepdims=True))
    p_new = jnp.exp(s_new - m)
    tot_prob = jnp.zeros((N_HEADS, PAGE_SIZE), F32)
    acc = jnp.zeros((D_ATTN, PAGE_SIZE), F32)
    for p in range(n_pages):
        prob = jnp.exp(scores[p] - m)
        tot_prob = tot_prob + prob
        acc = acc + v_pages[p][...] * head_rows(prob)
    l = p_new + jnp.sum(tot_prob, axis=1, keepdims=True)
    out = (jnp.sum(acc, axis=1, keepdims=True) + head_rows(p_new) * v_new) / head_rows(l)
    o_ref[...] = jnp.where(mine, out, o_ref[...])


def _attend_sample(q, k_new, v_new, cache_k, cache_v, page_table, slopes_rows):
    n_req, n_pages = page_table.shape
    assert n_req == LANES
    n_phys = cache_k.shape[0]
    as_tiles = lambda c: c.transpose(0, 2, 3, 1).reshape(n_phys, D_ATTN, PAGE_SIZE)
    ck, cv = as_tiles(cache_k), as_tiles(cache_v)
    cols = pl.BlockSpec((D_ATTN, n_req), lambda r, pt: (0, 0))
    anywhere = pl.BlockSpec(memory_space=pl.ANY)
    page_buf = pltpu.VMEM((2, n_pages, D_ATTN, PAGE_SIZE), F32)
    grid_spec = pltpu.PrefetchScalarGridSpec(
        num_scalar_prefetch=1,
        grid=(n_req,),
        in_specs=[cols, cols, cols, pl.BlockSpec((N_HEADS, LANES), lambda r, pt: (0, 0)), anywhere, anywhere],
        out_specs=cols,
        scratch_shapes=[page_buf, page_buf, pltpu.SemaphoreType.DMA((2,))],
    )
    out_t = pl.pallas_call(
        functools.partial(_decode_attn_kernel, n_pages=n_pages),
        grid_spec=grid_spec,
        out_shape=jax.ShapeDtypeStruct((D_ATTN, n_req), F32),
        compiler_params=_params(1),
        name="attn_sample",
    )(page_table.reshape(-1), q.T, k_new.T, v_new.T, slopes_rows, ck, cv)
    return out_t.T


def _onehots(e, lane):
    return [(e[:, kk:kk + 1] == lane) for kk in range(TOP_K)]


def _expert_ranks(e, carry_ref):
    tm = e.shape[0]
    lane = lax.broadcasted_iota(I32, (tm, LANES), 1)
    hots = _onehots(e, lane)
    hot = jnp.zeros((tm, LANES), F32)
    for h in hots:
        hot = hot + h.astype(F32)
    r = lax.broadcasted_iota(I32, (tm, tm), 0)
    c = lax.broadcasted_iota(I32, (tm, tm), 1)
    earlier = (c < r).astype(BF16)
    before = carry_ref[...] + jnp.dot(earlier, hot.astype(BF16), preferred_element_type=F32)
    rank = jnp.zeros((tm, LANES), F32)
    for kk, h in enumerate(hots):
        rank = jnp.where(lane == kk, jnp.sum(jnp.where(h, before, 0.0), axis=1, keepdims=True), rank)
    carry_ref[...] = carry_ref[...] + jnp.sum(hot, axis=0, keepdims=True)
    return rank.astype(I32)


def _branch_merge(y_conv, x, attn_bf, sg, mod, w, outs, carry_ref):
    (b_dw, ln_g, ln_b, w_ao, w_pw2, w_out, n2g, w_router, b_router, count_in) = w
    xo_ref, h2_ref, eidx_ref, gate_ref, rank_ref, count_ref = outs

    @pl.when(pl.program_id(0) == 0)
    def _():
        carry_ref[...] = count_in[0:1, :]
    y = y_conv + b_dw[...]
    mu = jnp.mean(y, axis=-1, keepdims=True)
    yc = y - mu
    var = jnp.mean(yc * yc, axis=-1, keepdims=True)
    y = (yc * lax.rsqrt(var + EPS)) * ln_g[...] + ln_b[...]
    y = y * _sigmoid(y)
    c_out = jnp.dot(y.astype(BF16), w_pw2[...], preferred_element_type=F32)
    a_out = jnp.dot(attn_bf, w_ao[...], preferred_element_type=F32)
    merged = sg[:, 0:D_MODEL] * a_out + sg[:, D_MODEL:2 * D_MODEL] * c_out
    g1 = mod[:, 2 * D_MODEL:3 * D_MODEL]
    xo = x + g1 * jnp.dot(merged.astype(BF16), w_out[...], preferred_element_type=F32)
    xo_ref[...] = xo
    sh2, sc2 = mod[:, 3 * D_MODEL:4 * D_MODEL], mod[:, 4 * D_MODEL:5 * D_MODEL]
    h2 = _rms(xo, n2g[...]) * (1.0 + sc2) + sh2
    _rows_to_tiles(h2_ref, h2)

    h2_hi = h2.astype(BF16)
    h2_lo = (h2 - h2_hi.astype(F32)).astype(BF16)
    logits = (jnp.dot(h2_hi, w_router[0], preferred_element_type=F32)
              + jnp.dot(h2_lo, w_router[0], preferred_element_type=F32)
              + jnp.dot(h2_hi, w_router[1], preferred_element_type=F32)) + b_router[...]
    tm = logits.shape[0]
    lane = lax.broadcasted_iota(I32, (tm, LANES), 1)
    logits = jnp.where(lane < N_EXPERTS, logits, NEG_INF)
    eidx = lane.astype(F32)
    vals, e_out = [], jnp.zeros((tm, LANES), F32)
    for kk in range(TOP_K):
        mx = jnp.max(logits, axis=1, keepdims=True)
        first = jnp.min(jnp.where(logits == mx, eidx, float(N_EXPERTS)), axis=1, keepdims=True)
        logits = jnp.where(eidx == first, NEG_INF, logits)
        vals.append(mx)
        e_out = jnp.where(lane == kk, first, e_out)
    ex = [jnp.exp(v - vals[0]) for v in vals]
    den = ex[0] + ex[1] + ex[2] + ex[3]
    g_out = jnp.zeros((tm, LANES), F32)
    for kk in range(TOP_K):
        g_out = jnp.where(lane == kk, ex[kk] / den, g_out)
    experts = e_out.astype(I32)
    eidx_ref[...] = experts
    gate_ref[...] = g_out
    rank_ref[...] = _expert_ranks(experts, carry_ref)
    count_ref[...] = jnp.broadcast_to(carry_ref[...], count_ref.shape)


def _merge_prompt_kernel(x_ref, attn_ref, u_ref, halo_ref, sg_ref, mod_ref, wdw_ref, *refs, tiles_per_seq):
    w, outs, (full_ref, shifted_ref, carry_ref) = refs[:10], refs[10:16], refs[16:]
    tm = u_ref.shape[0]
    first = (pl.program_id(0) % tiles_per_seq) == 0
    full_ref[0:HALO, :] = jnp.where(first, 0.0, halo_ref[...])
    full_ref[HALO:HALO + tm, :] = u_ref[...]
    base = HALO - (CONV_WIDTH - 1)
    y = jnp.zeros((tm, D_CONV), F32)
    for phase in range(SUBLANES):
        taps = [j for j in range(CONV_WIDTH) if (base + j) % SUBLANES == phase]
        rows = max(base + j for j in taps) - phase + tm
        shifted_ref[phase, 0:rows, :] = full_ref[phase:phase + rows, :]
        for j in taps:
            lo = base + j - phase
            y = y + wdw_ref[j:j + 1, :] * shifted_ref[phase, lo:lo + tm, :]
    _branch_merge(y, x_ref[...], attn_ref[...], sg_ref[...], mod_ref[0], w, outs, carry_ref)


def _merge_sample_kernel(x_ref, attn_ref, u_ref, state_ref, sg_ref, mod_ref, wdw_ref, *refs):
    w, outs, (carry_ref,) = refs[:10], refs[10:16], refs[16:]
    y = wdw_ref[CONV_WIDTH - 1:CONV_WIDTH, :] * u_ref[...]
    for j in range(CONV_WIDTH - 1):
        y = y + wdw_ref[j:j + 1, :] * state_ref[j]
    _branch_merge(y, x_ref[...], attn_ref[...].astype(BF16), sg_ref[...], mod_ref[...], w, outs, carry_ref)


def _merge(x, attn, u, hist, sg, mod, weights, *, per_token, tm, tiles_per_seq):
    n = x.shape[0]
    row = lambda i: (i, 0)
    if per_token:
        kernel = _merge_sample_kernel
        hist_spec = _const_spec(hist.shape)
        scratch = []
    else:
        kernel = functools.partial(_merge_prompt_kernel, tiles_per_seq=tiles_per_seq)
        hist_spec = pl.BlockSpec((HALO, D_CONV), lambda i: (jnp.maximum(i * (tm // HALO) - 1, 0), 0))
        scratch = [pltpu.VMEM((HALO + tm, D_CONV), F32), pltpu.VMEM((SUBLANES, HALO + tm, D_CONV), F32)]
    scratch.append(pltpu.VMEM((1, LANES), F32))
    flat = lambda w, dt: (jax.ShapeDtypeStruct((n, w), dt), pl.BlockSpec((tm, w), row))
    pairs = [flat(D_MODEL, F32),
             (jax.ShapeDtypeStruct(_tiled(n), F32), pl.BlockSpec(_tiled(tm), row)),
             flat(LANES, I32), flat(LANES, F32), flat(LANES, I32),
             (jax.ShapeDtypeStruct((8, LANES), F32), pl.BlockSpec((8, LANES), lambda i: (0, 0)))]
    return pl.pallas_call(
        kernel,
        grid=(n // tm,),
        in_specs=[pl.BlockSpec((tm, D_MODEL), row), pl.BlockSpec((tm, D_ATTN), row),
                  pl.BlockSpec((tm, D_CONV), row), hist_spec, pl.BlockSpec((tm, 2 * D_MODEL), row),
                  _mod_spec(per_token, tm, tiles_per_seq)] + [_const_spec(a.shape) for a in weights],
        out_specs=[p[1] for p in pairs],
        out_shape=[p[0] for p in pairs],
        scratch_shapes=scratch,
        compiler_params=_params(1),
        name="merge",
    )(x, attn, u, hist, sg, mod, *weights)


def _slot_kernel(e_ref, rank_ref, count_ref, dest_ref, blk_ref, info_ref, *, n_blocks_pad):
    tm = e_ref.shape[0]
    counts = count_ref[0:1, :].astype(I32)
    padded = ((counts + (EXPERT_BLOCK - 1)) // EXPERT_BLOCK) * EXPERT_BLOCK
    r = lax.broadcasted_iota(I32, (LANES, LANES), 0)
    c = lax.broadcasted_iota(I32, (LANES, LANES), 1)
    pstart = jnp.dot(jnp.broadcast_to(padded.astype(F32), (8, LANES)), (r < c).astype(F32),
                     precision=lax.Precision.HIGHEST, preferred_element_type=F32)[0:1]
    pend = pstart + padded.astype(F32)
    lane = lax.broadcasted_iota(I32, (tm, LANES), 1)
    e = e_ref[...]
    dest = rank_ref[...]
    for kk, h in enumerate(_onehots(e, lane)):
        off = jnp.sum(jnp.where(h, pstart, 0.0), axis=1, keepdims=True).astype(I32)
        dest = dest + jnp.where(lane == kk, off, 0)
    dest_ref[...] = dest

    lane1 = lax.broadcasted_iota(I32, (1, LANES), 1)
    used = jnp.max(pend, axis=1, keepdims=True)
    blk_start = (lax.broadcasted_iota(I32, (n_blocks_pad, 1), 0) * EXPERT_BLOCK).astype(F32)
    blk_start = jnp.minimum(blk_start, used - EXPERT_BLOCK)
    done = jnp.where((lane1 < N_EXPERTS) & (pend <= blk_start), 1.0, 0.0)
    blk_exp = jnp.minimum(jnp.sum(done, axis=1, keepdims=True), N_EXPERTS - 1.0)
    blk_ref[...] = jnp.broadcast_to(blk_exp, blk_ref.shape).astype(I32)
    row8 = lax.broadcasted_iota(I32, (8, LANES), 0)
    info = jnp.where(row8 == 0, pstart, jnp.where(row8 == 1, pend, jnp.where(row8 == 2, used, 0.0)))
    info_ref[...] = info.astype(I32)


def _slots(eidx, rank, counts, tm, n_blocks):
    n = eidx.shape[0]
    row = lambda i: (i, 0)
    fixed = lambda i: (0, 0)
    n_blocks_pad = -(-n_blocks // 8) * 8
    dest, blk, info = pl.pallas_call(
        functools.partial(_slot_kernel, n_blocks_pad=n_blocks_pad),
        grid=(n // tm,),
        in_specs=[pl.BlockSpec((tm, LANES), row), pl.BlockSpec((tm, LANES), row),
                  pl.BlockSpec((8, LANES), fixed)],
        out_specs=[pl.BlockSpec((tm, LANES), row), pl.BlockSpec((n_blocks_pad, LANES), fixed),
                   pl.BlockSpec((8, LANES), fixed)],
        out_shape=[jax.ShapeDtypeStruct((n, LANES), I32), jax.ShapeDtypeStruct((n_blocks_pad, LANES), I32),
                   jax.ShapeDtypeStruct((8, LANES), I32)],
        compiler_params=_params(1),
        name="moe_slot",
    )(eidx, rank, counts)
    dest_flat = dest[:, :TOP_K].reshape(-1)
    blk_exp = blk[:n_blocks, 0]
    pend = info[1, :N_EXPERTS]
    n_used = jnp.right_shift(info[2, 0:1], EXPERT_BLOCK.bit_length() - 1)
    return dest_flat, blk_exp, pend, n_used


def _for_each_row(n_rows, fn):
    @pl.loop(0, n_rows // SUBLANES)
    def _(g):
        first = pl.multiple_of(g * SUBLANES, SUBLANES)
        for s in range(SUBLANES):
            fn(first + s)


def _start_all(copies):
    for i, c in enumerate(copies):
        c.start(priority=i % 2)


def _row_copy(src_ref, src_row, dst_ref, dst_row, sem):
    tile = lambda row: pl.ds(pl.multiple_of(row * ROW_PITCH, ROW_PITCH), ROW_PITCH)
    return pltpu.make_async_copy(src_ref.at[tile(src_row)], dst_ref.at[tile(dst_row)], sem)


def _dispatch_kernel(dest_ref, pend_ref, h_ref, tail_ref, xs_ref, zero_ref, sem):
    @pl.when(pl.program_id(0) == 0)
    def _():
        zero_ref[...] = jnp.zeros_like(zero_ref)
        block_rows = zero_ref.shape[0]
        n_blocks = xs_ref.shape[0] // block_rows
        first_unused = pend_ref[N_EXPERTS - 1] // EXPERT_BLOCK

        def fill(start):
            rows = pl.ds(pl.multiple_of(start * ROW_PITCH, block_rows), block_rows)
            return pltpu.make_async_copy(zero_ref, xs_ref.at[rows], sem)

        def last_block(e):
            return jnp.maximum(pend_ref[e] - EXPERT_BLOCK, 0)

        @pl.loop(0, N_EXPERTS)
        def _(e):
            fill(last_block(e)).start()

        @pl.loop(first_unused, n_blocks)
        def _(j):
            fill(j * EXPERT_BLOCK).start()

        @pl.loop(0, N_EXPERTS)
        def _(e):
            fill(last_block(e)).wait()

        @pl.loop(first_unused, n_blocks)
        def _(j):
            fill(j * EXPERT_BLOCK).wait()

    def scatter(h_ref):
        def copies(t):
            return [_row_copy(h_ref, t, xs_ref, dest_ref[t * TOP_K + kk], sem) for kk in range(TOP_K)]

        n_rows = h_ref.shape[0] // ROW_PITCH
        _for_each_row(n_rows, lambda t: _start_all(copies(t)))
        _for_each_row(n_rows, lambda t: [c.wait() for c in copies(t)])

    is_tail = pl.program_id(0) == pl.num_programs(0) - 1

    @pl.when(jnp.logical_not(is_tail))
    def _():
        scatter(h_ref)

    @pl.when(is_tail)
    def _():
        scatter(tail_ref)


def _dispatch(h2, h2_tail, dest_flat, pend, tm, cap):
    n_tiles = h2.shape[0] // _tiled(tm)[0]
    assert h2_tail.shape[0] <= _tiled(tm)[0] and dest_flat.shape[0] == (n_tiles + 1) * tm * TOP_K
    return pl.pallas_call(
        _dispatch_kernel,
        grid=(n_tiles + 1,),
        in_specs=[pl.BlockSpec((tm * TOP_K,), lambda i: (i,), memory_space=pltpu.SMEM),
                  pl.BlockSpec(memory_space=pltpu.SMEM),
                  pl.BlockSpec(_tiled(tm), lambda i: (jnp.minimum(i, n_tiles - 1), 0)),
                  _const_spec(h2_tail.shape)],
        out_specs=pl.BlockSpec(memory_space=pl.ANY),
        out_shape=jax.ShapeDtypeStruct(_tiled(cap), F32),
        scratch_shapes=[pltpu.VMEM(_tiled(EXPERT_BLOCK), F32), pltpu.SemaphoreType.DMA(())],
        compiler_params=_params(1, has_side_effects=True),
        name="moe_dispatch",
    )(dest_flat, pend, h2, h2_tail)


def _expert_kernel(blk_ref, used_ref, x_ref, wg_ref, wu_ref, wd_ref, y_ref, wg_bf, wu_bf, wd_bf):
    j = pl.program_id(0)
    active = j < used_ref[0]
    changed = (j == 0) | (blk_ref[j] != blk_ref[jnp.maximum(j - 1, 0)])

    @pl.when(active & changed)
    def _():
        wg_bf[...] = wg_ref[0].astype(BF16)
        wu_bf[...] = wu_ref[0].astype(BF16)
        wd_bf[...] = wd_ref[0].astype(BF16)

    @pl.when(active)
    def _():
        x = _tiles_to_rows(x_ref).astype(BF16)
        gt = jnp.minimum(jnp.dot(x, wg_bf[...], preferred_element_type=F32), SWIGLU_LIMIT)
        up = jnp.clip(jnp.dot(x, wu_bf[...], preferred_element_type=F32), -SWIGLU_LIMIT, SWIGLU_LIMIT)
        act = gt * _sigmoid(SWIGLU_ALPHA * gt) * (up + 1.0)
        _rows_to_tiles(y_ref, jnp.dot(act.astype(BF16), wd_bf[...], preferred_element_type=F32))

    @pl.when(jnp.logical_not(active))
    def _():
        y_ref[...] = jnp.zeros_like(y_ref)


def _experts(xs, blk_exp, n_used, w_gate, w_up, w_down):
    n_blocks = blk_exp.shape[0]
    d_ff = w_gate.shape[2]
    xrow = lambda j, blk, used: (jnp.minimum(j, used[0] - 1), 0)
    wsel = lambda j, blk, used: (blk[j], 0, 0)
    grid_spec = pltpu.PrefetchScalarGridSpec(
        num_scalar_prefetch=2,
        grid=(n_blocks,),
        in_specs=[pl.BlockSpec(_tiled(EXPERT_BLOCK), xrow),
                  pl.BlockSpec((1, D_MODEL, d_ff), wsel),
                  pl.BlockSpec((1, D_MODEL, d_ff), wsel),
                  pl.BlockSpec((1, d_ff, D_MODEL), wsel)],
        out_specs=pl.BlockSpec(_tiled(EXPERT_BLOCK), lambda j, blk, used: (j, 0)),
        scratch_shapes=[pltpu.VMEM((D_MODEL, d_ff), BF16), pltpu.VMEM((D_MODEL, d_ff), BF16),
                        pltpu.VMEM((d_ff, D_MODEL), BF16)],
    )
    return pl.pallas_call(
        _expert_kernel,
        grid_spec=grid_spec,
        out_shape=jax.ShapeDtypeStruct(xs.shape, F32),
        compiler_params=_params(1),
        name="moe_experts",
    )(blk_exp, n_used, xs, w_gate, w_up, w_down)


def _combine_kernel(dest_ref, next_ref, y_ref, xo_ref, gate_ref, mod_ref, fg_ref, o_ref, buf_ref, sem, *, per_token):
    tm = xo_ref.shape[0]
    step = pl.program_id(0)
    slot = step % 2

    def copies(slots_ref, half, t):
        return [_row_copy(y_ref, slots_ref[t * TOP_K + kk], buf_ref.at[half, kk], t, sem.at[half])
                for kk in range(TOP_K)]

    @pl.when(step == 0)
    def _():
        _for_each_row(tm, lambda t: _start_all(copies(dest_ref, 0, t)))

    @pl.when(step + 1 < pl.num_programs(0))
    def _():
        _for_each_row(tm, lambda t: _start_all(copies(next_ref, 1 - slot, t)))

    _for_each_row(tm, lambda t: [c.wait() for c in copies(dest_ref, slot, t)])

    gates = gate_ref[...]
    moe = gates[:, 0:1] * _tiles_to_rows(buf_ref.at[slot, 0])
    for kk in range(1, TOP_K):
        moe = moe + gates[:, kk:kk + 1] * _tiles_to_rows(buf_ref.at[slot, kk])
    g2 = _load_mod(mod_ref, per_token)[:, 5 * D_MODEL:6 * D_MODEL]
    o_ref[...] = _rms(xo_ref[...] + g2 * moe, fg_ref[...])


def _combine(y, dest_flat, xo, gates, mod, final_g, *, per_token, tm, tiles_per_seq):
    n = xo.shape[0]
    n_tiles = n // tm
    row = lambda i: (i, 0)
    return pl.pallas_call(
        functools.partial(_combine_kernel, per_token=per_token),
        grid=(n_tiles,),
        in_specs=[pl.BlockSpec((tm * TOP_K,), lambda i: (i,), memory_space=pltpu.SMEM),
                  pl.BlockSpec((tm * TOP_K,), lambda i: (jnp.minimum(i + 1, n_tiles - 1),), memory_space=pltpu.SMEM),
                  pl.BlockSpec(memory_space=pl.ANY),
                  pl.BlockSpec((tm, D_MODEL), row), pl.BlockSpec((tm, LANES), row),
                  _mod_spec(per_token, tm, tiles_per_seq), _const_spec((1, D_MODEL))],
        out_specs=pl.BlockSpec((tm, D_MODEL), row),
        out_shape=jax.ShapeDtypeStruct((n, D_MODEL), F32),
        scratch_shapes=[pltpu.VMEM((2, TOP_K) + _tiled(tm), F32), pltpu.SemaphoreType.DMA((2,))],
        compiler_params=_params(1),
        name="moe_combine",
    )(dest_flat, dest_flat, y, xo, gates, mod, final_g)


def _moe(prompt, sample, counts, final_g, w_gate, w_up, w_down, *, tm, tiles_per_seq):
    n_p, n_s = prompt[1].shape[0], sample[1].shape[0]
    n_blocks = -(-(n_p + n_s) * TOP_K // EXPERT_BLOCK) + N_EXPERTS
    dest_p, blk_exp, pend, n_used = _slots(prompt[1], prompt[3], counts, SLOT_TILE, n_blocks)
    dest_s, _, _, _ = _slots(sample[1], sample[3], counts, n_s, n_blocks)
    dest_all = jnp.concatenate([dest_p, dest_s, jnp.zeros(((tm - n_s) * TOP_K,), I32)])
    xs = _dispatch(prompt[0], sample[0], dest_all, pend, tm, n_blocks * EXPERT_BLOCK)
    y = _experts(xs, blk_exp, n_used, w_gate, w_up, w_down)
    y_p = _combine(y, dest_p, prompt[4], prompt[2], prompt[5], final_g,
                   per_token=False, tm=tm, tiles_per_seq=tiles_per_seq)
    y_s = _combine(y, dest_s, sample[4], sample[2], sample[5], final_g,
                   per_token=True, tm=n_s, tiles_per_seq=1)
    return y_p, y_s


def kernel(x_prompt, x_sample, c_prompt, c_sample, cache_k, cache_v, state_conv, page_table, norm1_g, norm2_g,
           w_ada, b_ada, w_in, w_attn_out, w_dw, b_dw, ln_g, ln_b, w_pw2, w_out, w_router, b_router, w_gate,
           w_up, w_down, final_g):
    depth = w_in.shape[0]
    assert depth == 1, "single-layer trunk"
    batch, seq, _ = x_prompt.shape
    n_req = x_sample.shape[0]
    assert x_sample.shape[1] == 1 and seq % MOBA_BLOCK == 0 and n_req % 8 == 0
    layer = 0
    row2 = lambda a: a.reshape(1, -1)

    pad = (-batch) % 8
    c_all = jnp.concatenate([c_prompt, jnp.zeros((pad, D_MODEL), F32), c_sample], axis=0)
    ada = _ada(c_all, w_ada[layer], b_ada[layer])
    mod_p = ada[:batch].reshape(batch, 1, 6 * D_MODEL)
    mod_s = ada[batch + pad:]

    w_in_bf = w_in[layer].astype(BF16)
    merge_w = (row2(b_dw[layer]), row2(ln_g[layer]), row2(ln_b[layer]), w_attn_out[layer].astype(BF16),
               w_pw2[layer].astype(BF16), w_out[layer].astype(BF16), row2(norm2_g[layer]),
               _split_bf16(jnp.pad(w_router[layer], ((0, 0), (0, LANES - N_EXPERTS)))),
               jnp.pad(row2(b_router[layer]), ((0, 0), (0, LANES - N_EXPERTS))))
    w_dw_pad = jnp.pad(w_dw[layer], ((0, HALO - CONV_WIDTH), (0, 0)))
    slopes = jnp.exp2(-8.0 * jnp.arange(1, N_HEADS + 1, dtype=F32) / N_HEADS)
    slopes_rows = jnp.broadcast_to(slopes[:, None], (N_HEADS, LANES))
    fg = row2(final_g)
    experts = (w_gate[layer], w_up[layer], w_down[layer])

    tm = MOBA_BLOCK
    tps = seq // tm
    xp = x_prompt.reshape(batch * seq, D_MODEL)
    w_kvt_bf = w_in_bf[:, D_ATTN:3 * D_ATTN].T
    q, k_bf, kt, vt, vt_bf, u, sg, kmean = _inproj(xp, mod_p, row2(norm1_g[layer]), w_in_bf, w_kvt_bf,
                                                   per_token=False, tm=tm, tiles_per_seq=tps)
    attn = _attend_prompt(q, k_bf, vt_bf, kmean, slopes, batch, seq)
    no_counts = jnp.zeros((8, LANES), F32)
    xo, h2, eidx, gates, rank, counts_p = _merge(xp, attn, u, u, sg, mod_p, (w_dw_pad,) + merge_w + (no_counts,),
                                                 per_token=False, tm=tm, tiles_per_seq=tps)
    to_cache = lambda t: t.reshape(batch, N_HEADS, HEAD_DIM, seq).transpose(0, 3, 1, 2)[None]
    k_prompt, v_prompt = to_cache(kt), to_cache(vt)
    conv_prompt = u.reshape(batch, seq, D_CONV)[None, :, seq - (CONV_WIDTH - 1):]

    xs = x_sample.reshape(n_req, D_MODEL)
    qs, ks, vs, us, sgs = _inproj(xs, mod_s, row2(norm1_g[layer]), w_in_bf,
                                  per_token=True, tm=n_req, tiles_per_seq=1)
    attn_s = _attend_sample(qs, ks, vs, cache_k[layer], cache_v[layer], page_table, slopes_rows)
    state = state_conv[layer]
    xo_s, h2_s, eidx_s, gates_s, rank_s, counts = _merge(
        xs, attn_s, us, state.transpose(1, 0, 2), sgs, mod_s, (w_dw_pad,) + merge_w + (counts_p,),
        per_token=True, tm=n_req, tiles_per_seq=1)

    y_prompt, y_sample = _moe((h2, eidx, gates, rank, xo, mod_p), (h2_s, eidx_s, gates_s, rank_s, xo_s, mod_s),
                              counts, fg, *experts, tm=tm, tiles_per_seq=tps)
    k_sample = ks.reshape(1, n_req, 1, N_HEADS, HEAD_DIM)
    v_sample = vs.reshape(1, n_req, 1, N_HEADS, HEAD_DIM)
    conv_sample = jnp.concatenate([state[:, 1:], us[:, None, :]], axis=1)[None]

    return (y_prompt.reshape(batch, seq, D_MODEL), y_sample.reshape(n_req, 1, D_MODEL),
            k_prompt, v_prompt, conv_prompt, k_sample, v_sample, conv_sample)
```
